```python
import jax, jax.numpy as jnp
from jax import lax
import numpy as np

D_MODEL = 2048
BATCH = 16
SEQ = 256
DEPTH = 1
DEC_BATCH = 8
DEC_SEQ = 2048
PAST_LEN = 256

GRID_W = 64
MLSTM_WIDTH = D_MODEL // 2
N_HEADS = 8
HEAD_DIM = MLSTM_WIDTH // N_HEADS
POOL_WIDTH = D_MODEL - MLSTM_WIDTH
POOL_WINDOWS = (2, 4, 8, 16)
N_POOL_GROUPS = len(POOL_WINDOWS)
POOL_GROUP_DIM = POOL_WIDTH // N_POOL_GROUPS
N_GATES = 4
PROJ_WIDTH = 4 * MLSTM_WIDTH + N_GATES * N_HEADS + POOL_WIDTH
D_FF = 4 * D_MODEL
CHUNK = 128
N_MOD = 6
EPS = 1e-6

kernel_name = "hymba_mlstm_pool_flow_step"


def _rmsnorm(x, w):
    xf = x.astype(jnp.float32)
    xf = xf * lax.rsqrt(jnp.mean(xf * xf, axis=-1, keepdims=True) + EPS)
    return (xf * w.astype(jnp.float32)).astype(x.dtype)


def _adaln(cond, ada_w, ada_b):
    mod = jax.nn.silu(cond) @ ada_w + ada_b
    return mod.reshape(cond.shape[0], N_MOD, D_MODEL)


def _mlstm_scan(q, k, v, li, lf, C0, n0, m0):
    B, T, H, Dh = q.shape
    nc = T // CHUNK

    def to_chunks(a):
        a = a.reshape((B, nc, CHUNK, H) + a.shape[3:])
        return jnp.moveaxis(a, (1, 3), (0, 2))

    mask = jnp.tril(jnp.ones((CHUNK, CHUNK), dtype=bool))

    def step(carry, xs):
        C, n, m = carry
        qc, kc, vc, lic, lfc = xs
        b = jnp.cumsum(lfc, axis=-1)
        dmat = jnp.where(mask, b[..., :, None] - b[..., None, :] + lic[..., None, :], -jnp.inf)
        m_inter = b + m[..., None]
        m_t = jnp.maximum(m_inter, jnp.max(dmat, axis=-1))
        s = jnp.einsum('bhtd,bhsd->bhts', qc, kc) * jnp.exp(dmat - m_t[..., None])
        g = jnp.exp(m_inter - m_t)
        num = g[..., None] * jnp.einsum('bhtd,bhde->bhte', qc, C) + jnp.einsum('bhts,bhse->bhte', s, vc)
        den = g * jnp.einsum('bhtd,bhd->bht', qc, n) + jnp.sum(s, axis=-1)
        h = num / jnp.maximum(jnp.abs(den), jnp.exp(-m_t))[..., None]
        b_last = b[..., -1]
        decay = b_last[..., None] - b + lic
        m_new = jnp.maximum(b_last + m, jnp.max(decay, axis=-1))
        wk = kc * jnp.exp(decay - m_new[..., None])[..., None]
        g_state = jnp.exp(b_last + m - m_new)
        C_new = g_state[..., None, None] * C + jnp.einsum('bhsd,bhse->bhde', wk, vc)
        n_new = g_state[..., None] * n + jnp.sum(wk, axis=2)
        return (C_new, n_new, m_new), h

    (C, n, m), h = lax.scan(step, (C0, n0, m0),
                            (to_chunks(q), to_chunks(k), to_chunks(v), to_chunks(li), to_chunks(lf)))
    h = jnp.moveaxis(h, (0, 2), (1, 3)).reshape(B, T, H, Dh)
    return h, C, n, m


def _mlstm_bidir(q, k, v, gates, C0, n0, m0):
    li_f = gates[:, :, 0]
    lf_f = jax.nn.log_sigmoid(gates[:, :, 1])
    li_b = gates[:, :, 2]
    lf_b = jax.nn.log_sigmoid(gates[:, :, 3])
    rev = lambda a: jnp.flip(a, axis=1)
    h_f, Cf, nf, mf = _mlstm_scan(q, k, v, li_f, lf_f, C0[:, 0], n0[:, 0], m0[:, 0])
    h_b, Cb, nb, mb = _mlstm_scan(rev(q), rev(k), rev(v), rev(li_b), rev(lf_b), C0[:, 1], n0[:, 1], m0[:, 1])
    h = h_f + rev(h_b)
    return h, jnp.stack([Cf, Cb], axis=1), jnp.stack([nf, nb], axis=1), jnp.stack([mf, mb], axis=1)


def _centred_pool_minus_self(x, window):
    L = x.shape[-2]
    xf = x.astype(jnp.float32)
    cs = jnp.cumsum(xf, axis=-2)
    cs = jnp.concatenate([jnp.zeros_like(cs[..., :1, :]), cs], axis=-2)
    t = jnp.arange(L)
    lo = jnp.maximum(t - window // 2, 0)
    hi = jnp.minimum(t + window // 2, L)
    total = jnp.take(cs, hi, axis=-2) - jnp.take(cs, lo, axis=-2)
    mean = total / (hi - lo).astype(jnp.float32)[:, None]
    return (mean - xf).astype(x.dtype)


def _pool_mixer(u, pool_w, pool_scale, grid_w):
    B, L, _ = u.shape
    if grid_w is not None:
        rows = L // grid_w
        u = u.reshape(B, rows, grid_w, POOL_WIDTH)
    groups = [_centred_pool_minus_self(u[..., gi * POOL_GROUP_DIM:(gi + 1) * POOL_GROUP_DIM], w)
              for gi, w in enumerate(POOL_WINDOWS)]
    p = jnp.stack(groups, axis=-2)
    y = jnp.einsum('...gc,gcd->...gd', p, pool_w).reshape(B, L, POOL_WIDTH)
    return y * pool_scale


def _layer(x, mod, C0, n0, m0, w_in, gate_bias, mlstm_norm_w, pool_w, pool_scale, w_out, norm_w, w1, w2, grid_w):
    B, T, _ = x.shape
    shift_a, scale_a, gate_a, shift_f, scale_f, gate_f = [mod[:, i][:, None, :] for i in range(N_MOD)]
    h = _rmsnorm(x, norm_w[0]) * (1 + scale_a) + shift_a
    proj = h @ w_in
    M = MLSTM_WIDTH
    q, k, v, o, g, u = jnp.split(proj, [M, 2 * M, 3 * M, 4 * M, 4 * M + N_GATES * N_HEADS], axis=-1)
    f32 = jnp.float32
    q = q.reshape(B, T, N_HEADS, HEAD_DIM).astype(f32) * (HEAD_DIM ** -0.5)
    k = k.reshape(B, T, N_HEADS, HEAD_DIM).astype(f32)
    v = v.reshape(B, T, N_HEADS, HEAD_DIM).astype(f32)
    gates = g.reshape(B, T, N_GATES, N_HEADS).astype(f32) + gate_bias.astype(f32)
    hm, C, n, m = _mlstm_bidir(q, k, v, gates, C0.astype(f32), n0.astype(f32), m0.astype(f32))
    hm = _rmsnorm(hm, mlstm_norm_w.reshape(N_HEADS, HEAD_DIM)).astype(x.dtype)
    hm = hm.reshape(B, T, M) * jax.nn.sigmoid(o)
    hp = _pool_mixer(u, pool_w, pool_scale, grid_w)
    mix = jnp.concatenate([hm, hp], axis=-1) @ w_out
    x = x + gate_a * _rmsnorm(mix, norm_w[1])
    h = _rmsnorm(x, norm_w[2]) * (1 + scale_f) + shift_f
    y = jnp.square(jax.nn.relu(h @ w1)) @ w2
    x = x + gate_f * _rmsnorm(y, norm_w[3])
    return x, C, n, m


def setup_inputs(seed: int = 0) -> dict:
    key = jax.random.key(seed)
    ks = jax.random.split(key, 20)
    nrm = jax.random.normal
    f32 = jnp.float32
    x_prompt = nrm(ks[0], (BATCH, SEQ, D_MODEL), f32)
    x_sample = nrm(ks[1], (DEC_BATCH, DEC_SEQ, D_MODEL), f32)
    c = nrm(ks[2], (DEC_BATCH, D_MODEL), f32)
    state_C = 0.5 * nrm(ks[3], (DEC_BATCH, DEPTH, 2, N_HEADS, HEAD_DIM, HEAD_DIM), f32)
    state_n = 0.5 * nrm(ks[4], (DEC_BATCH, DEPTH, 2, N_HEADS, HEAD_DIM), f32)
    state_m = 0.5 * nrm(ks[5], (DEC_BATCH, DEPTH, 2, N_HEADS), f32)
    c_ctx = nrm(ks[6], (D_MODEL,), f32)
    w_in = nrm(ks[7], (DEPTH, D_MODEL, PROJ_WIDTH), f32) * D_MODEL ** -0.5
    ib = 0.1 * nrm(ks[8], (DEPTH, 2, N_HEADS), f32)
    fb = jnp.linspace(3.0, 6.0, N_HEADS, dtype=f32) + 0.1 * nrm(ks[9], (DEPTH, 2, N_HEADS), f32)
    gate_bias = jnp.stack([ib[:, 0], fb[:, 0], ib[:, 1], fb[:, 1]], axis=1)
    mlstm_norm_w = 1.0 + 0.1 * nrm(ks[10], (DEPTH, MLSTM_WIDTH), f32)
    pool_w = nrm(ks[11], (DEPTH, N_POOL_GROUPS, POOL_GROUP_DIM, POOL_GROUP_DIM), f32) * POOL_GROUP_DIM ** -0.5
    pool_scale = 1.0 + 0.1 * nrm(ks[12], (DEPTH, POOL_WIDTH), f32)
    w_out = nrm(ks[13], (DEPTH, MLSTM_WIDTH + POOL_WIDTH, D_MODEL), f32) * (MLSTM_WIDTH + POOL_WIDTH) ** -0.5
    ada_w = nrm(ks[14], (DEPTH, D_MODEL, N_MOD * D_MODEL), f32) * (0.5 * D_MODEL ** -0.5)
    ada_b = 0.02 * nrm(ks[15], (DEPTH, N_MOD * D_MODEL), f32)
    norm_w = 1.0 + 0.1 * nrm(ks[16], (DEPTH, 4, D_MODEL), f32)
    w1 = nrm(ks[17], (DEPTH, D_MODEL, D_FF), f32) * D_MODEL ** -0.5
    w2 = nrm(ks[18], (DEPTH, D_FF, D_MODEL), f32) * D_FF ** -0.5
    return {"x_prompt": x_prompt, "x_sample": x_sample, "c": c,
            "state_C": state_C, "state_n": state_n, "state_m": state_m,
            "c_ctx": c_ctx, "w_in": w_in, "gate_bias": gate_bias, "mlstm_norm_w": mlstm_norm_w,
            "pool_w": pool_w, "pool_scale": pool_scale, "w_out": w_out,
            "ada_w": ada_w, "ada_b": ada_b, "norm_w": norm_w, "w1": w1, "w2": w2}


def reference(x_prompt, x_sample, c, state_C, state_n, state_m, c_ctx, w_in, gate_bias, mlstm_norm_w,
              pool_w, pool_scale, w_out, ada_w, ada_b, norm_w, w1, w2):
    B = x_prompt.shape[0]
    yp = x_prompt
    ys = x_sample
    new_C, new_n, new_m = [], [], []
    for layer in range(DEPTH):
        params = (w_in[layer], gate_bias[layer], mlstm_norm_w[layer], pool_w[layer], pool_scale[layer],
                  w_out[layer], norm_w[layer], w1[layer], w2[layer])
        mod_ctx = _adaln(c_ctx[None, :], ada_w[layer], ada_b[layer])
        C0 = jnp.zeros((B, 2, N_HEADS, HEAD_DIM, HEAD_DIM), jnp.float32)
        n0 = jnp.zeros((B, 2, N_HEADS, HEAD_DIM), jnp.float32)
        m0 = jnp.zeros((B, 2, N_HEADS), jnp.float32)
        yp, Cc, nc_, mc = _layer(yp, mod_ctx, C0, n0, m0, *params, None)
        new_C.append(Cc)
        new_n.append(nc_)
        new_m.append(mc)
        mod_lat = _adaln(c, ada_w[layer], ada_b[layer])
        ys, _, _, _ = _layer(ys, mod_lat, state_C[:, layer], state_n[:, layer], state_m[:, layer], *params, GRID_W)
    new_state_C = jnp.stack(new_C, axis=1)
    new_state_n = jnp.stack(new_n, axis=1)
    new_state_m = jnp.stack(new_m, axis=1)
    return (yp, ys, new_state_C, new_state_n, new_state_m)
```

```python
import functools

import jax
import jax.numpy as jnp
from jax import lax
from jax.experimental import pallas as pl
from jax.experimental.pallas import tpu as pltpu

F32 = jnp.float32
BF16 = jnp.bfloat16

D_MODEL = 2048
MLSTM_WIDTH = 1024
N_HEADS = 8
HEAD_DIM = 128
POOL_WIDTH = 1024
POOL_WINDOWS = (2, 4, 8, 16)
POOL_GROUP_DIM = 256
N_GATES = 4
D_FF = 8192
CHUNK = 128
N_MOD = 6
GRID_W = 64
EPS = 1e-6

LANES = 128
SUBLANES = 8
VMEM_LIMIT = 56 * 1024 * 1024

HEADS_PER_STEP = 4
GATE_ROWS = N_GATES * SUBLANES


def _cparams(sem):
    return pltpu.CompilerParams(dimension_semantics=sem, vmem_limit_bytes=VMEM_LIMIT)


def _resident(shape):
    nd = len(shape)
    return pl.BlockSpec(shape, lambda *_: (0,) * nd, pipeline_mode=pl.Buffered(1))


def _rms(x, w):
    ms = jnp.mean(x * x, axis=-1, keepdims=True)
    return x * lax.rsqrt(ms + EPS) * w


def _mod_kernel(cond_ref, w_ref, b_ref, out_ref):
    c = cond_ref[...]
    s = c * jax.nn.sigmoid(c)
    out_ref[...] = jnp.dot(s.astype(BF16), w_ref[...].astype(BF16),
                           preferred_element_type=F32) + b_ref[...]


def _mod_call(cond, ada_w, ada_b):
    rows = cond.shape[0]
    n = ada_w.shape[1]
    tn = 1536
    return pl.pallas_call(
        _mod_kernel,
        grid=(n // tn,),
        in_specs=[pl.BlockSpec((rows, D_MODEL), lambda j: (0, 0)),
                  pl.BlockSpec((D_MODEL, tn), lambda j: (0, j)),
                  pl.BlockSpec((1, tn), lambda j: (0, j))],
        out_specs=pl.BlockSpec((rows, tn), lambda j: (0, j)),
        out_shape=jax.ShapeDtypeStruct((rows, n), F32),
        compiler_params=_cparams(("arbitrary",)),
        name="mod",
    )(cond, ada_w, ada_b.reshape(1, n))


def _inproj_kernel(x_ref, mod_ref, nw_ref, wq_ref, wkt_ref, wv_ref, wo_ref, wu_ref, wgt_ref, gb_ref,
                   q_ref, kt_ref, v_ref, o_ref, u_ref, g_ref, *, seqs, chunks):
    x = x_ref[...]
    mod = mod_ref[0]
    h = (_rms(x, nw_ref[0:1, :]) * (1.0 + mod[1:2, :]) + mod[0:1, :]).astype(BF16)
    q = jnp.dot(h, wq_ref[...], preferred_element_type=F32) * (HEAD_DIM ** -0.5)
    q_ref[...] = q.astype(BF16)
    v_ref[...] = jnp.dot(h, wv_ref[...], preferred_element_type=F32).astype(BF16)
    o_ref[...] = jnp.dot(h, wo_ref[...], preferred_element_type=F32)
    u_ref[...] = jnp.dot(h, wu_ref[...], preferred_element_type=F32)
    nt = (((1,), (1,)), ((), ()))
    kt = lax.dot_general(wkt_ref[...], h, nt, preferred_element_type=F32).astype(BF16)
    gt = lax.dot_general(wgt_ref[...], h, nt, preferred_element_type=F32) + gb_ref[...]
    for s in range(seqs):
        for c in range(chunks):
            lo = (s * chunks + c) * CHUNK
            kt_ref[s, c] = kt[:, lo:lo + CHUNK]
            g_ref[s, c] = gt[:, lo:lo + CHUNK]


def _inproj_call(x, mod, mod_row, norm_w, wts, *, batch, seq_len, tm):
    wq, wkt, wv, wo, wu, wgt, gb = wts
    tokens = batch * seq_len
    nc = seq_len // CHUNK
    grows = wgt.shape[0]
    if tm >= seq_len:
        seqs, chunks = tm // seq_len, nc
        kt_map = lambda i: (i, 0, 0, 0)
    else:
        seqs, chunks = 1, tm // CHUNK
        per = seq_len // tm
        kt_map = lambda i: (i // per, i % per, 0, 0)
    tiles_per_seq = max(seq_len // tm, 1)
    if mod_row is None:
        mod_map = lambda i: (1 + i // tiles_per_seq, 0, 0)
    else:
        mod_map = lambda i: (mod_row, 0, 0)
    tok = lambda i: (i, 0)
    gbb = jnp.broadcast_to(gb[:, None], (grows, tm))
    kern = functools.partial(_inproj_kernel, seqs=seqs, chunks=chunks)
    return pl.pallas_call(
        kern,
        grid=(tokens // tm,),
        in_specs=[pl.BlockSpec((tm, D_MODEL), tok),
                  pl.BlockSpec((1, N_MOD, D_MODEL), mod_map),
                  _resident((4, D_MODEL)),
                  _resident(wq.shape), _resident(wkt.shape), _resident(wv.shape),
                  _resident(wo.shape), _resident(wu.shape), _resident(wgt.shape),
                  _resident((grows, tm))],
        out_specs=[pl.BlockSpec((tm, MLSTM_WIDTH), tok),
                   pl.BlockSpec((seqs, chunks, MLSTM_WIDTH, CHUNK), kt_map),
                   pl.BlockSpec((tm, MLSTM_WIDTH), tok),
                   pl.BlockSpec((tm, MLSTM_WIDTH), tok),
                   pl.BlockSpec((tm, POOL_WIDTH), tok),
                   pl.BlockSpec((seqs, chunks, grows, CHUNK), kt_map)],
        out_shape=[jax.ShapeDtypeStruct((tokens, MLSTM_WIDTH), BF16),
                   jax.ShapeDtypeStruct((batch, nc, MLSTM_WIDTH, CHUNK), BF16),
                   jax.ShapeDtypeStruct((tokens, MLSTM_WIDTH), BF16),
                   jax.ShapeDtypeStruct((tokens, MLSTM_WIDTH), F32),
                   jax.ShapeDtypeStruct((tokens, POOL_WIDTH), F32),
                   jax.ShapeDtypeStruct((batch, nc, grows, CHUNK), F32)],
        compiler_params=_cparams(("arbitrary",)),
        name="in_proj",
    )(x, mod, norm_w, wq, wkt, wv, wo, wu, wgt, gbb)


def _lane_scan(x, op, ident, reverse):
    lane = lax.broadcasted_iota(jnp.int32, x.shape, 1)
    k = 1
    while k < LANES:
        if reverse:
            shifted = pltpu.roll(x, LANES - k, axis=1)
            x = op(x, jnp.where(lane < LANES - k, shifted, ident))
        else:
            shifted = pltpu.roll(x, k, axis=1)
            x = op(x, jnp.where(lane >= k, shifted, ident))
        k *= 2
    return x


def _chunk_start(c):
    return c * CHUNK if isinstance(c, int) else pl.multiple_of(c * CHUNK, CHUNK)


def _mlstm_kernel(*refs, nc, hb, has_init, emit_state):
    it = iter(refs)
    q_ref, kt_ref, v_ref, o_ref, g_ref, nw_ref = (next(it) for _ in range(6))
    if has_init:
        c0_ref, n0_ref, m0_ref = (next(it) for _ in range(3))
    out_ref = next(it)
    if emit_state:
        cout_ref, nout_ref, mout_ref = (next(it) for _ in range(3))
    hf_sc, hb_sc, cn_sc, m_sc = (next(it) for _ in range(4))

    if has_init:
        for d in range(2):
            for j in range(hb):
                cn_sc[d * hb + j, :, 0:HEAD_DIM] = c0_ref[0, d, j]
                cn_sc[d * hb + j, :, HEAD_DIM:] = n0_ref[0, d, j]
            m_sc[d] = m0_ref[0, 0, d]
    else:
        cn_sc[...] = jnp.zeros_like(cn_sc)
        m_sc[...] = jnp.zeros_like(m_sc)

    row = lax.broadcasted_iota(jnp.int32, (CHUNK, CHUNK), 0)
    col = lax.broadcasted_iota(jnp.int32, (CHUNK, CHUNK), 1)
    causal = (col <= row, col >= row)
    ones_blk = jnp.ones((CHUNK, HEAD_DIM), BF16)
    zpad = jnp.zeros((LANES - 6 * SUBLANES, LANES), F32)
    neg_inf = jnp.float32(-jnp.inf)

    def step(c, carry):
        chunk_of = (c, nc - 1 - c)
        rows = []
        per_dir = []
        for d in range(2):
            gates = g_ref[0, chunk_of[d]]
            li = gates[(2 * d) * SUBLANES:(2 * d + 1) * SUBLANES]
            lf = jax.nn.log_sigmoid(gates[(2 * d + 1) * SUBLANES:(2 * d + 2) * SUBLANES])
            b = _lane_scan(lf, jnp.add, 0.0, reverse=(d == 1))
            a = li - b
            pm = _lane_scan(a, jnp.maximum, neg_inf, reverse=(d == 1))
            m = m_sc[d]
            mu = jnp.maximum(m, pm)
            mu_last = jnp.maximum(m, jnp.max(a, axis=1, keepdims=True))
            b_last = jnp.sum(lf, axis=1, keepdims=True)
            per_dir.append(dict(a=a, e=jnp.exp(a - mu_last), gs=jnp.exp(m - mu_last)))
            rows += [mu, jnp.exp(m - mu), jnp.exp(-(b + mu))]
            m_sc[d] = jnp.broadcast_to(b_last + mu_last, m.shape)
        xt = jnp.concatenate(rows + [zpad], axis=0).T

        for d in range(2):
            ch = chunk_of[d]
            t0 = _chunk_start(ch)
            for j in range(hb):
                hs = slice(j * HEAD_DIM, (j + 1) * HEAD_DIM)
                qc = q_ref[0, pl.ds(t0, CHUNK), hs]
                vc = v_ref[0, pl.ds(t0, CHUNK), hs]
                ktc = kt_ref[0, ch, hs, :]
                base = 3 * d * SUBLANES + j
                mu_c = xt[:, base:base + 1]
                g_c = xt[:, base + SUBLANES:base + SUBLANES + 1]
                en_c = xt[:, base + 2 * SUBLANES:base + 2 * SUBLANES + 1]
                a_r = per_dir[d]["a"][j:j + 1, :]
                e_r = per_dir[d]["e"][j:j + 1, :]
                gs_r = per_dir[d]["gs"][j:j + 1, :]

                s = jnp.dot(qc, ktc, preferred_element_type=F32)
                p = s * jnp.exp(jnp.where(causal[d], a_r - mu_c, neg_inf))
                v1 = jnp.concatenate([vc, ones_blk], axis=1)
                intra = jnp.dot(p.astype(BF16), v1, preferred_element_type=F32)
                cn = cn_sc[d * hb + j]
                inter = jnp.dot(qc, cn.astype(BF16), preferred_element_type=F32)
                tot = g_c * inter + intra
                hval = tot[:, :HEAD_DIM] / jnp.maximum(jnp.abs(tot[:, HEAD_DIM:]), en_c)
                dst = hf_sc if d == 0 else hb_sc
                dst[pl.ds(t0, CHUNK), hs] = hval

                wkt = (ktc.astype(F32) * e_r).astype(BF16)
                dcn = jnp.dot(wkt, v1, preferred_element_type=F32)
                gs_full = jnp.concatenate([jnp.broadcast_to(gs_r, (HEAD_DIM, LANES))] * 2, axis=1)
                cn_sc[d * hb + j] = gs_full * cn + dcn
        return carry

    if nc <= 2:
        for c in range(nc):
            step(c, 0)
    else:
        lax.fori_loop(0, nc, step, 0)

    def finalize(c, carry):
        t0 = _chunk_start(c)
        hsum = hf_sc[pl.ds(t0, CHUNK), :] + hb_sc[pl.ds(t0, CHUNK), :]
        og = jax.nn.sigmoid(o_ref[0, pl.ds(t0, CHUNK), :])
        for j in range(hb):
            hs = slice(j * HEAD_DIM, (j + 1) * HEAD_DIM)
            hn = _rms(hsum[:, hs], nw_ref[0:1, hs])
            out_ref[0, pl.ds(t0, CHUNK), hs] = (hn * og[:, hs]).astype(BF16)
        return carry

    if nc <= 2:
        for c in range(nc):
            finalize(c, 0)
    else:
        lax.fori_loop(0, nc, finalize, 0)

    if emit_state:
        for d in range(2):
            for j in range(hb):
                cout_ref[0, d, j] = cn_sc[d * hb + j, :, 0:HEAD_DIM]
                nout_ref[0, d, j] = cn_sc[d * hb + j, :, HEAD_DIM:]
            mout_ref[0, 0, d] = m_sc[d]


def _mlstm_call(q, kt, v, o, gates, nw, init, *, batch, seq_len, emit_state):
    hb = HEADS_PER_STEP
    nhb = N_HEADS // hb
    nc = seq_len // CHUNK
    w = hb * HEAD_DIM
    seq_map = lambda b, k: (b, 0, k)
    in_specs = [pl.BlockSpec((1, seq_len, w), seq_map),
                pl.BlockSpec((1, nc, w, CHUNK), lambda b, k: (b, 0, k, 0)),
                pl.BlockSpec((1, seq_len, w), seq_map),
                pl.BlockSpec((1, seq_len, w), seq_map),
                pl.BlockSpec((1, nc, GATE_ROWS, CHUNK), lambda b, k: (b, 0, k, 0)),
                pl.BlockSpec((1, w), lambda b, k: (0, k))]
    args = [q, kt, v, o, gates, nw]
    state_map = lambda b, k: (b, 0, k, 0, 0)
    m_map = lambda b, k: (b, k, 0, 0, 0)
    if init is not None:
        c0, n0, m0 = init
        in_specs += [pl.BlockSpec((1, 2, hb, HEAD_DIM, HEAD_DIM), state_map),
                     pl.BlockSpec((1, 2, hb, HEAD_DIM, LANES), state_map),
                     pl.BlockSpec((1, 1, 2, SUBLANES, LANES), m_map)]
        args += [c0, n0, m0]
    out_specs = [pl.BlockSpec((1, seq_len, w), seq_map)]
    out_shape = [jax.ShapeDtypeStruct((batch, seq_len, MLSTM_WIDTH), BF16)]
    if emit_state:
        out_specs += [pl.BlockSpec((1, 2, hb, HEAD_DIM, HEAD_DIM), state_map),
                      pl.BlockSpec((1, 2, hb, HEAD_DIM, LANES), state_map),
                      pl.BlockSpec((1, 1, 2, SUBLANES, LANES), m_map)]
        out_shape += [jax.ShapeDtypeStruct((batch, 2, N_HEADS, HEAD_DIM, HEAD_DIM), F32),
                      jax.ShapeDtypeStruct((batch, 2, N_HEADS, HEAD_DIM, LANES), F32),
                      jax.ShapeDtypeStruct((batch, nhb, 2, SUBLANES, LANES), F32)]
    kern = functools.partial(_mlstm_kernel, nc=nc, hb=hb, has_init=init is not None,
                             emit_state=emit_state)
    return pl.pallas_call(
        kern,
        grid=(batch, nhb),
        in_specs=in_specs,
        out_specs=out_specs,
        out_shape=out_shape,
        scratch_shapes=[pltpu.VMEM((seq_len, w), F32),
                        pltpu.VMEM((seq_len, w), F32),
                        pltpu.VMEM((2 * hb, HEAD_DIM, 2 * HEAD_DIM), F32),
                        pltpu.VMEM((2, SUBLANES, LANES), F32)],
        compiler_params=_cparams(("arbitrary", "arbitrary")),
        name="mlstm",
    )(*args)


def _pool_minus_self(x, half, pos, seg):
    n = x.shape[0]
    total = x
    for dlt in list(range(-half, 0)) + list(range(1, half)):
        shifted = pltpu.roll(x, (-dlt) % n, axis=0)
        ok = (pos + dlt >= 0) & (pos + dlt < seg)
        total = total + jnp.where(ok, shifted, 0.0)
    cnt = (jnp.minimum(pos + half, seg) - jnp.maximum(pos - half, 0)).astype(F32)
    return total / cnt - x


def _outproj_kernel(hm_ref, u_ref, x_ref, mod_ref, nw_ref, pw_ref, ps_ref, wout_ref,
                    x1_ref, h2_ref, *, seg):
    tm = x_ref.shape[0]
    pos = lax.broadcasted_iota(jnp.int32, (tm, 1), 0) % seg
    parts = [hm_ref[...]]
    for gi, win in enumerate(POOL_WINDOWS):
        cs = slice(gi * POOL_GROUP_DIM, (gi + 1) * POOL_GROUP_DIM)
        p = _pool_minus_self(u_ref[:, cs], win // 2, pos, seg)
        y = jnp.dot(p.astype(BF16), pw_ref[gi], preferred_element_type=F32)
        parts.append((y * ps_ref[0:1, cs]).astype(BF16))
    mix = jnp.dot(jnp.concatenate(parts, axis=1), wout_ref[...], preferred_element_type=F32)
    mod = mod_ref[0]
    x1 = x_ref[...] + mod[2:3, :] * _rms(mix, nw_ref[1:2, :])
    x1_ref[...] = x1
    h2_ref[...] = (_rms(x1, nw_ref[2:3, :]) * (1.0 + mod[4:5, :]) + mod[3:4, :]).astype(BF16)


def _outproj_call(hm, u, x, mod, mod_map, norm_w, pool_w, pool_scale, w_out, *, seg, tm):
    tokens = x.shape[0]
    tok = lambda i: (i, 0)
    return pl.pallas_call(
        functools.partial(_outproj_kernel, seg=seg),
        grid=(tokens // tm,),
        in_specs=[pl.BlockSpec((tm, MLSTM_WIDTH), tok),
                  pl.BlockSpec((tm, POOL_WIDTH), tok),
                  pl.BlockSpec((tm, D_MODEL), tok),
                  pl.BlockSpec((1, N_MOD, D_MODEL), mod_map),
                  _resident((4, D_MODEL)),
                  _resident(pool_w.shape),
                  _resident((1, POOL_WIDTH)),
                  _resident(w_out.shape)],
        out_specs=[pl.BlockSpec((tm, D_MODEL), tok),
                   pl.BlockSpec((tm, D_MODEL), tok)],
        out_shape=[jax.ShapeDtypeStruct((tokens, D_MODEL), F32),
                   jax.ShapeDtypeStruct((tokens, D_MODEL), BF16)],
        compiler_params=_cparams(("arbitrary",)),
        name="out_proj",
    )(hm, u, x, mod, norm_w, pool_w, pool_scale.reshape(1, POOL_WIDTH), w_out)


def _mlp_kernel(h_ref, w1_ref, w2_ref, x1_ref, mod_ref, nw_ref, out_ref, acc_ref):
    k = pl.program_id(1)
    a = jnp.dot(h_ref[...], w1_ref[...], preferred_element_type=F32)
    a = jnp.square(jnp.maximum(a, 0.0)).astype(BF16)
    y = jnp.dot(a, w2_ref[...], preferred_element_type=F32)

    @pl.when(k == 0)
    def _():
        acc_ref[...] = y

    @pl.when(k > 0)
    def _():
        acc_ref[...] += y

    @pl.when(k == pl.num_programs(1) - 1)
    def _():
        out_ref[...] = x1_ref[...] + mod_ref[0, 5:6, :] * _rms(acc_ref[...], nw_ref[3:4, :])


def _mlp_call(h2, x1, mod, mod_map, norm_w, w1, w2, *, tm, tf):
    tokens = x1.shape[0]
    tok = lambda i, k: (i, 0)
    return pl.pallas_call(
        _mlp_kernel,
        grid=(tokens // tm, D_FF // tf),
        in_specs=[pl.BlockSpec((tm, D_MODEL), tok),
                  pl.BlockSpec((D_MODEL, tf), lambda i, k: (0, k)),
                  pl.BlockSpec((tf, D_MODEL), lambda i, k: (k, 0)),
                  pl.BlockSpec((tm, D_MODEL), tok),
                  pl.BlockSpec((1, N_MOD, D_MODEL), lambda i, k: mod_map(i)),
                  pl.BlockSpec((4, D_MODEL), lambda i, k: (0, 0))],
        out_specs=pl.BlockSpec((tm, D_MODEL), tok),
        out_shape=jax.ShapeDtypeStruct((tokens, D_MODEL), F32),
        scratch_shapes=[pltpu.VMEM((tm, D_MODEL), F32)],
        compiler_params=_cparams(("arbitrary", "arbitrary")),
        name="mlp",
    )(h2, w1, w2, x1, mod, norm_w)


def _gate_weights(w_g, gate_bias):
    hb = HEADS_PER_STEP
    nhb = N_HEADS // hb
    wg = w_g.reshape(D_MODEL, N_GATES, nhb, hb)
    wg = jnp.pad(wg, ((0, 0), (0, 0), (0, 0), (0, SUBLANES - hb)))
    wgt = wg.transpose(2, 1, 3, 0).reshape(nhb * GATE_ROWS, D_MODEL)
    gb = jnp.pad(gate_bias.reshape(N_GATES, nhb, hb), ((0, 0), (0, 0), (0, SUBLANES - hb)))
    gb = gb.transpose(1, 0, 2).reshape(nhb * GATE_ROWS)
    return wgt.astype(BF16), gb.astype(F32)


def _path(x, mod, mod_row, init, params, *, seg, tm, tm_mlp, emit_state):
    batch, seq_len, _ = x.shape
    wts, mlstm_nw, pool_w, pool_scale, w_out, norm_w, w1, w2 = params
    tokens = batch * seq_len
    xf = x.reshape(tokens, D_MODEL)
    q, kt, v, o, u, gates = _inproj_call(xf, mod, mod_row, norm_w, wts,
                                         batch=batch, seq_len=seq_len, tm=tm)
    shp = (batch, seq_len, MLSTM_WIDTH)
    res = _mlstm_call(q.reshape(shp), kt, v.reshape(shp), o.reshape(shp), gates, mlstm_nw, init,
                      batch=batch, seq_len=seq_len, emit_state=emit_state)
    hm = res[0].reshape(tokens, MLSTM_WIDTH)

    def mod_map_for(t):
        if mod_row is None:
            per = seq_len // t
            return lambda i: (1 + i // per, 0, 0)
        return lambda i: (mod_row, 0, 0)

    x1, h2 = _outproj_call(hm, u, xf, mod, mod_map_for(tm), norm_w, pool_w, pool_scale, w_out,
                           seg=seg, tm=tm)
    y = _mlp_call(h2, x1, mod, mod_map_for(tm_mlp), norm_w, w1, w2, tm=tm_mlp, tf=1024)
    return y.reshape(batch, seq_len, D_MODEL), res[1:]


def kernel(x_prompt, x_sample, c, state_C, state_n, state_m, c_ctx, w_in, gate_bias, mlstm_norm_w,
           pool_w, pool_scale, w_out, ada_w, ada_b, norm_w, w1, w2):
    hb = HEADS_PER_STEP
    nhb = N_HEADS // hb
    m4 = 4 * MLSTM_WIDTH
    yp, ys = x_prompt, x_sample
    new_c, new_n, new_m = [], [], []
    for layer in range(w_in.shape[0]):
        wl = w_in[layer]
        wgt, gb = _gate_weights(wl[:, m4:m4 + N_GATES * N_HEADS], gate_bias[layer])
        wts = (wl[:, 0:MLSTM_WIDTH].astype(BF16),
               wl[:, MLSTM_WIDTH:2 * MLSTM_WIDTH].T.astype(BF16),
               wl[:, 2 * MLSTM_WIDTH:3 * MLSTM_WIDTH].astype(BF16),
               wl[:, 3 * MLSTM_WIDTH:m4].astype(BF16),
               wl[:, m4 + N_GATES * N_HEADS:].astype(BF16),
               wgt, gb)
        params = (wts, mlstm_norm_w[layer].reshape(1, MLSTM_WIDTH), pool_w[layer].astype(BF16),
                  pool_scale[layer], w_out[layer].astype(BF16), norm_w[layer],
                  w1[layer].astype(BF16), w2[layer].astype(BF16))

        n_lat = c.shape[0]
        rows = 16
        cond = jnp.concatenate([c_ctx[None, :], c, jnp.zeros((rows - 1 - n_lat, D_MODEL), F32)], axis=0)
        mod = _mod_call(cond, ada_w[layer], ada_b[layer]).reshape(rows, N_MOD, D_MODEL)

        yp, st = _path(yp, mod, 0, None, params, seg=yp.shape[1], tm=512, tm_mlp=512, emit_state=True)
        cc, nn, mm = st
        new_c.append(cc)
        new_n.append(nn[..., 0])
        new_m.append(mm.reshape(mm.shape[0], nhb, 2, SUBLANES, LANES)[:, :, :, :hb, 0]
                     .transpose(0, 2, 1, 3).reshape(mm.shape[0], 2, N_HEADS))

        sc = state_C[:, layer].astype(F32)
        sn = jnp.broadcast_to(state_n[:, layer].astype(F32)[..., None],
                              (n_lat, 2, N_HEADS, HEAD_DIM, LANES))
        sm = state_m[:, layer].astype(F32).reshape(n_lat, 2, nhb, hb).transpose(0, 2, 1, 3)
        sm = jnp.pad(sm, ((0, 0), (0, 0), (0, 0), (0, SUBLANES - hb)))
        sm = jnp.broadcast_to(sm[..., None], (n_lat, nhb, 2, SUBLANES, LANES))
        ys, _ = _path(ys, mod, None, (sc, sn, sm), params, seg=GRID_W, tm=512, tm_mlp=512,
                      emit_state=False)
    return (yp, ys, jnp.stack(new_c, axis=1), jnp.stack(new_n, axis=1), jnp.stack(new_m, axis=1))
```

```python
import functools

import jax
import jax.numpy as jnp
from jax import lax
from jax.experimental import pallas as pl
from jax.experimental.pallas import tpu as pltpu

F32 = jnp.float32
BF16 = jnp.bfloat16

D_MODEL = 2048
MLSTM_WIDTH = 1024
N_HEADS = 8
HEAD_DIM = 128
POOL_WIDTH = 1024
POOL_WINDOWS = (2, 4, 8, 16)
POOL_GROUP_DIM = 256
N_GATES = 4
D_FF = 8192
CHUNK = 128
N_MOD = 6
GRID_W = 64
EPS = 1e-6

LANES = 128
SUBLANES = 8
VMEM_LIMIT = 56 * 1024 * 1024

HEADS_PER_STEP = 4
GATE_ROWS = N_GATES * SUBLANES


def _cparams(sem):
    return pltpu.CompilerParams(dimension_semantics=sem, vmem_limit_bytes=VMEM_LIMIT)


def _resident(shape):
    nd = len(shape)
    return pl.BlockSpec(shape, lambda *_: (0,) * nd, pipeline_mode=pl.Buffered(1))


def _rms(x, w):
    ms = jnp.mean(x * x, axis=-1, keepdims=True)
    return x * lax.rsqrt(ms + EPS) * w


def _mod_kernel(cond_ref, w_ref, b_ref, out_ref):
    c = cond_ref[...]
    s = c * jax.nn.sigmoid(c)
    out_ref[...] = jnp.dot(s.astype(BF16), w_ref[...].astype(BF16),
                           preferred_element_type=F32) + b_ref[...]


def _mod_call(cond, ada_w, ada_b):
    rows = cond.shape[0]
    n = ada_w.shape[1]
    tn = 1536
    return pl.pallas_call(
        _mod_kernel,
        grid=(n // tn,),
        in_specs=[pl.BlockSpec((rows, D_MODEL), lambda j: (0, 0)),
                  pl.BlockSpec((D_MODEL, tn), lambda j: (0, j)),
                  pl.BlockSpec((1, tn), lambda j: (0, j))],
        out_specs=pl.BlockSpec((rows, tn), lambda j: (0, j)),
        out_shape=jax.ShapeDtypeStruct((rows, n), F32),
        compiler_params=_cparams(("arbitrary",)),
        name="mod",
    )(cond, ada_w, ada_b.reshape(1, n))


def _inproj_kernel(x_ref, mod_ref, nw_ref, wq_ref, wkt_ref, wv_ref, wo_ref, wu_ref, wgt_ref, gb_ref,
                   q_ref, kt_ref, v_ref, o_ref, u_ref, g_ref, *, seqs, chunks):
    x = x_ref[...]
    mod = mod_ref[0]
    h = (_rms(x, nw_ref[0:1, :]) * (1.0 + mod[1:2, :]) + mod[0:1, :]).astype(BF16)
    q = jnp.dot(h, wq_ref[...], preferred_element_type=F32) * (HEAD_DIM ** -0.5)
    q_ref[...] = q.astype(BF16)
    v_ref[...] = jnp.dot(h, wv_ref[...], preferred_element_type=F32).astype(BF16)
    o_ref[...] = jnp.dot(h, wo_ref[...], preferred_element_type=F32)
    u_ref[...] = jnp.dot(h, wu_ref[...], preferred_element_type=F32)
    nt = (((1,), (1,)), ((), ()))
    kt = lax.dot_general(wkt_ref[...], h, nt, preferred_element_type=F32).astype(BF16)
    gt = lax.dot_general(wgt_ref[...], h, nt, preferred_element_type=F32) + gb_ref[...]
    for s in range(seqs):
        for c in range(chunks):
            lo = (s * chunks + c) * CHUNK
            kt_ref[s, c] = kt[:, lo:lo + CHUNK]
            g_ref[s, c] = gt[:, lo:lo + CHUNK]


def _inproj_call(x, mod, mod_row, norm_w, wts, *, batch, seq_len, tm):
    wq, wkt, wv, wo, wu, wgt, gb = wts
    tokens = batch * seq_len
    nc = seq_len // CHUNK
    grows = wgt.shape[0]
    if tm >= seq_len:
        seqs, chunks = tm // seq_len, nc
        kt_map = lambda i: (i, 0, 0, 0)
    else:
        seqs, chunks = 1, tm // CHUNK
        per = seq_len // tm
        kt_map = lambda i: (i // per, i % per, 0, 0)
    tiles_per_seq = max(seq_len // tm, 1)
    if mod_row is None:
        mod_map = lambda i: (1 + i // tiles_per_seq, 0, 0)
    else:
        mod_map = lambda i: (mod_row, 0, 0)
    tok = lambda i: (i, 0)
    gbb = jnp.broadcast_to(gb[:, None], (grows, tm))
    kern = functools.partial(_inproj_kernel, seqs=seqs, chunks=chunks)
    return pl.pallas_call(
        kern,
        grid=(tokens // tm,),
        in_specs=[pl.BlockSpec((tm, D_MODEL), tok),
                  pl.BlockSpec((1, N_MOD, D_MODEL), mod_map),
                  _resident((4, D_MODEL)),
                  _resident(wq.shape), _resident(wkt.shape), _resident(wv.shape),
                  _resident(wo.shape), _resident(wu.shape), _resident(wgt.shape),
                  _resident((grows, tm))],
        out_specs=[pl.BlockSpec((tm, MLSTM_WIDTH), tok),
                   pl.BlockSpec((seqs, chunks, MLSTM_WIDTH, CHUNK), kt_map),
                   pl.BlockSpec((tm, MLSTM_WIDTH), tok),
                   pl.BlockSpec((tm, MLSTM_WIDTH), tok),
                   pl.BlockSpec((tm, POOL_WIDTH), tok),
                   pl.BlockSpec((seqs, chunks, grows, CHUNK), kt_map)],
        out_shape=[jax.ShapeDtypeStruct((tokens, MLSTM_WIDTH), BF16),
                   jax.ShapeDtypeStruct((batch, nc, MLSTM_WIDTH, CHUNK), BF16),
                   jax.ShapeDtypeStruct((tokens, MLSTM_WIDTH), BF16),
                   jax.ShapeDtypeStruct((tokens, MLSTM_WIDTH), F32),
                   jax.ShapeDtypeStruct((tokens, POOL_WIDTH), F32),
                   jax.ShapeDtypeStruct((batch, nc, grows, CHUNK), F32)],
        compiler_params=_cparams(("arbitrary",)),
        name="in_proj",
    )(x, mod, norm_w, wq, wkt, wv, wo, wu, wgt, gbb)


def _lane_scan(x, op, ident, reverse):
    lane = lax.broadcasted_iota(jnp.int32, x.shape, 1)
    k = 1
    while k < LANES:
        if reverse:
            shifted = pltpu.roll(x, LANES - k, axis=1)
            x = op(x, jnp.where(lane < LANES - k, shifted, ident))
        else:
            shifted = pltpu.roll(x, k, axis=1)
            x = op(x, jnp.where(lane >= k, shifted, ident))
        k *= 2
    return x


def _chunk_start(c):
    return c * CHUNK if isinstance(c, int) else pl.multiple_of(c * CHUNK, CHUNK)


def _mlstm_kernel(*refs, nc, hb, has_init, emit_state):
    it = iter(refs)
    q_ref, kt_ref, v_ref, o_ref, g_ref, nw_ref = (next(it) for _ in range(6))
    if has_init:
        c0_ref, n0_ref, m0_ref = (next(it) for _ in range(3))
    out_ref = next(it)
    if emit_state:
        cout_ref, nout_ref, mout_ref = (next(it) for _ in range(3))
    hf_sc, hb_sc, cn_sc, m_sc, xt_sc, rows_sc = (next(it) for _ in range(6))

    if has_init:
        for d in range(2):
            for j in range(hb):
                cn_sc[d * hb + j, :, 0:HEAD_DIM] = c0_ref[0, d, j]
                cn_sc[d * hb + j, :, HEAD_DIM:] = n0_ref[0, d, j]
    else:
        cn_sc[...] = jnp.zeros_like(cn_sc)

    neg_inf = jnp.float32(-jnp.inf)

    cols = []
    for d in range(2):
        li = g_ref[0, :, (2 * d) * SUBLANES:(2 * d + 1) * SUBLANES, :].reshape(nc * SUBLANES, LANES)
        lf = jax.nn.log_sigmoid(
            g_ref[0, :, (2 * d + 1) * SUBLANES:(2 * d + 2) * SUBLANES, :].reshape(nc * SUBLANES, LANES))
        b = _lane_scan(lf, jnp.add, 0.0, reverse=(d == 1))
        a = li - b
        pm = _lane_scan(a, jnp.maximum, neg_inf, reverse=(d == 1))
        amax = jnp.max(a, axis=1, keepdims=True)
        bsum = jnp.sum(lf, axis=1, keepdims=True)
        m = m0_ref[0, 0, d] if has_init else jnp.zeros((SUBLANES, LANES), F32)
        m_prev = [None] * nc
        for ci in (range(nc) if d == 0 else range(nc - 1, -1, -1)):
            rs = slice(ci * SUBLANES, (ci + 1) * SUBLANES)
            m_prev[ci] = m
            m = bsum[rs] + jnp.maximum(m, amax[rs])
        m_sc[d] = m
        mp = jnp.concatenate(m_prev, axis=0)
        mu = jnp.maximum(mp, pm)
        mu_last = jnp.maximum(mp, amax)
        rows_sc[0, d] = a.reshape(nc, SUBLANES, LANES)
        rows_sc[1, d] = jnp.exp(a - mu_last).reshape(nc, SUBLANES, LANES)
        rows_sc[2, d] = jnp.exp(mp - mu_last).reshape(nc, SUBLANES, LANES)
        cols.append((mu, jnp.exp(mp - mu), jnp.exp(-(b + mu))))
    zpad = jnp.zeros((LANES - 6 * SUBLANES, LANES), F32)
    for c in range(nc):
        pieces = []
        for d, ci in ((0, c), (1, nc - 1 - c)):
            pieces += [x[ci * SUBLANES:(ci + 1) * SUBLANES] for x in cols[d]]
        xt_sc[c] = jnp.concatenate(pieces + [zpad], axis=0).T

    row = lax.broadcasted_iota(jnp.int32, (CHUNK, CHUNK), 0)
    col = lax.broadcasted_iota(jnp.int32, (CHUNK, CHUNK), 1)
    causal = (col <= row, col >= row)
    ones_blk = jnp.ones((CHUNK, HEAD_DIM), BF16)

    def step(c, carry):
        chunk_of = (c, nc - 1 - c)
        xt = xt_sc[c]
        chains = []
        for d in range(2):
            ch = chunk_of[d]
            t0 = _chunk_start(ch)
            a_rows = rows_sc[0, d, ch]
            for j in range(hb):
                hs = slice(j * HEAD_DIM, (j + 1) * HEAD_DIM)
                qc = q_ref[0, pl.ds(t0, CHUNK), hs]
                base = 3 * d * SUBLANES + j
                mu_c = xt[:, base:base + 1]
                g_c = xt[:, base + SUBLANES:base + SUBLANES + 1]
                s = jnp.dot(qc, kt_ref[0, ch, hs, :], preferred_element_type=F32)
                p = s * jnp.exp(jnp.where(causal[d], a_rows[j:j + 1, :] - mu_c, neg_inf))
                gq = (qc.astype(F32) * g_c).astype(BF16)
                chains.append((d, j, ch, t0, jnp.concatenate([p.astype(BF16), gq], axis=1)))

        for d, j, ch, t0, lhs in chains:
            hs = slice(j * HEAD_DIM, (j + 1) * HEAD_DIM)
            base = 3 * d * SUBLANES + j
            en_c = xt[:, base + 2 * SUBLANES:base + 2 * SUBLANES + 1]
            v1 = jnp.concatenate([v_ref[0, pl.ds(t0, CHUNK), hs], ones_blk], axis=1)
            cn = cn_sc[d * hb + j]
            tot = jnp.dot(lhs, jnp.concatenate([v1, cn.astype(BF16)], axis=0),
                          preferred_element_type=F32)
            hval = tot[:, :HEAD_DIM] / jnp.maximum(jnp.abs(tot[:, HEAD_DIM:]), en_c)
            dst = hf_sc if d == 0 else hb_sc
            dst[pl.ds(t0, CHUNK), hs] = hval

            e_row = rows_sc[1, d, ch][j:j + 1, :]
            gs_row = rows_sc[2, d, ch][j:j + 1, :]
            wkt = (kt_ref[0, ch, hs, :].astype(F32) * e_row).astype(BF16)
            dcn = jnp.dot(wkt, v1, preferred_element_type=F32)
            gs_blk = jnp.broadcast_to(gs_row, (HEAD_DIM, LANES))
            cn_sc[d * hb + j] = jnp.concatenate([gs_blk, gs_blk], axis=1) * cn + dcn
        return carry

    if nc <= 2:
        for c in range(nc):
            step(c, 0)
    else:
        lax.fori_loop(0, nc, step, 0)

    def finalize(c, carry):
        t0 = _chunk_start(c)
        hsum = hf_sc[pl.ds(t0, CHUNK), :] + hb_sc[pl.ds(t0, CHUNK), :]
        og = jax.nn.sigmoid(o_ref[0, pl.ds(t0, CHUNK), :])
        for j in range(hb):
            hs = slice(j * HEAD_DIM, (j + 1) * HEAD_DIM)
            hn = _rms(hsum[:, hs], nw_ref[0:1, hs])
            out_ref[0, pl.ds(t0, CHUNK), hs] = (hn * og[:, hs]).astype(BF16)
        return carry

    if nc <= 2:
        for c in range(nc):
            finalize(c, 0)
    else:
        lax.fori_loop(0, nc, finalize, 0)

    if emit_state:
        for d in range(2):
            for j in range(hb):
                cout_ref[0, d, j] = cn_sc[d * hb + j, :, 0:HEAD_DIM]
                nout_ref[0, d, j] = cn_sc[d * hb + j, :, HEAD_DIM:]
            mout_ref[0, 0, d] = m_sc[d]


def _mlstm_call(q, kt, v, o, gates, nw, init, *, batch, seq_len, emit_state):
    hb = HEADS_PER_STEP
    nhb = N_HEADS // hb
    nc = seq_len // CHUNK
    w = hb * HEAD_DIM
    seq_map = lambda b, k: (b, 0, k)
    in_specs = [pl.BlockSpec((1, seq_len, w), seq_map),
                pl.BlockSpec((1, nc, w, CHUNK), lambda b, k: (b, 0, k, 0)),
                pl.BlockSpec((1, seq_len, w), seq_map),
                pl.BlockSpec((1, seq_len, w), seq_map),
                pl.BlockSpec((1, nc, GATE_ROWS, CHUNK), lambda b, k: (b, 0, k, 0)),
                pl.BlockSpec((1, w), lambda b, k: (0, k))]
    args = [q, kt, v, o, gates, nw]
    state_map = lambda b, k: (b, 0, k, 0, 0)
    m_map = lambda b, k: (b, k, 0, 0, 0)
    if init is not None:
        c0, n0, m0 = init
        in_specs += [pl.BlockSpec((1, 2, hb, HEAD_DIM, HEAD_DIM), state_map),
                     pl.BlockSpec((1, 2, hb, HEAD_DIM, LANES), state_map),
                     pl.BlockSpec((1, 1, 2, SUBLANES, LANES), m_map)]
        args += [c0, n0, m0]
    out_specs = [pl.BlockSpec((1, seq_len, w), seq_map)]
    out_shape = [jax.ShapeDtypeStruct((batch, seq_len, MLSTM_WIDTH), BF16)]
    if emit_state:
        out_specs += [pl.BlockSpec((1, 2, hb, HEAD_DIM, HEAD_DIM), state_map),
                      pl.BlockSpec((1, 2, hb, HEAD_DIM, LANES), state_map),
                      pl.BlockSpec((1, 1, 2, SUBLANES, LANES), m_map)]
        out_shape += [jax.ShapeDtypeStruct((batch, 2, N_HEADS, HEAD_DIM, HEAD_DIM), F32),
                      jax.ShapeDtypeStruct((batch, 2, N_HEADS, HEAD_DIM, LANES), F32),
                      jax.ShapeDtypeStruct((batch, nhb, 2, SUBLANES, LANES), F32)]
    kern = functools.partial(_mlstm_kernel, nc=nc, hb=hb, has_init=init is not None,
                             emit_state=emit_state)
    return pl.pallas_call(
        kern,
        grid=(batch, nhb),
        in_specs=in_specs,
        out_specs=out_specs,
        out_shape=out_shape,
        scratch_shapes=[pltpu.VMEM((seq_len, w), F32),
                        pltpu.VMEM((seq_len, w), F32),
                        pltpu.VMEM((2 * hb, HEAD_DIM, 2 * HEAD_DIM), F32),
                        pltpu.VMEM((2, SUBLANES, LANES), F32),
                        pltpu.VMEM((nc, LANES, LANES), F32),
                        pltpu.VMEM((3, 2, nc, SUBLANES, LANES), F32)],
        compiler_params=_cparams(("arbitrary", "arbitrary")),
        name="mlstm",
    )(*args)


def _shift_rows(x, dlt):
    return pltpu.roll(x, (-dlt) % x.shape[0], axis=0)


def _pool_minus_self(x, half, pos, seg):
    fwd = x
    k = 1
    while k < half:
        fwd = fwd + jnp.where(pos + k < seg, _shift_rows(fwd, k), 0.0)
        k *= 2
    bwd = jnp.where(pos >= 1, _shift_rows(x, -1), 0.0)
    k = 1
    while k < half:
        bwd = bwd + jnp.where(pos >= k, _shift_rows(bwd, -k), 0.0)
        k *= 2
    cnt = (jnp.minimum(pos + half, seg) - jnp.maximum(pos - half, 0)).astype(F32)
    return (fwd + bwd) / cnt - x


def _outproj_kernel(hm_ref, u_ref, x_ref, mod_ref, nw_ref, pw_ref, ps_ref, wout_ref,
                    x1_ref, h2_ref, mix_ref, *, seg):
    tm = x_ref.shape[0]
    pos = lax.broadcasted_iota(jnp.int32, (tm, 1), 0) % seg
    parts = []
    nslice = 2 * len(POOL_WINDOWS)
    ncol = D_MODEL // nslice
    for gi, win in enumerate(POOL_WINDOWS):
        halves = []
        for hh in range(2):
            idx = 2 * gi + hh
            ns = slice(idx * ncol, (idx + 1) * ncol)
            mix_ref[:, ns] = jnp.dot(hm_ref[...], wout_ref[0:MLSTM_WIDTH, ns],
                                     preferred_element_type=F32)
            lo = gi * POOL_GROUP_DIM + hh * LANES
            halves.append(_pool_minus_self(u_ref[:, lo:lo + LANES], win // 2, pos, seg).astype(BF16))
        cs = slice(gi * POOL_GROUP_DIM, (gi + 1) * POOL_GROUP_DIM)
        y = jnp.dot(jnp.concatenate(halves, axis=1), pw_ref[gi], preferred_element_type=F32)
        parts.append((y * ps_ref[0:1, cs]).astype(BF16))
    mix = mix_ref[...] + jnp.dot(jnp.concatenate(parts, axis=1), wout_ref[MLSTM_WIDTH:, :],
                                 preferred_element_type=F32)
    mod = mod_ref[0]
    x1 = x_ref[...] + mod[2:3, :] * _rms(mix, nw_ref[1:2, :])
    x1_ref[...] = x1
    h2_ref[...] = (_rms(x1, nw_ref[2:3, :]) * (1.0 + mod[4:5, :]) + mod[3:4, :]).astype(BF16)


def _outproj_call(hm, u, x, mod, mod_map, norm_w, pool_w, pool_scale, w_out, *, seg, tm):
    tokens = x.shape[0]
    tok = lambda i: (i, 0)
    return pl.pallas_call(
        functools.partial(_outproj_kernel, seg=seg),
        grid=(tokens // tm,),
        in_specs=[pl.BlockSpec((tm, MLSTM_WIDTH), tok),
                  pl.BlockSpec((tm, POOL_WIDTH), tok),
                  pl.BlockSpec((tm, D_MODEL), tok),
                  pl.BlockSpec((1, N_MOD, D_MODEL), mod_map),
                  _resident((4, D_MODEL)),
                  _resident(pool_w.shape),
                  _resident((1, POOL_WIDTH)),
                  _resident(w_out.shape)],
        out_specs=[pl.BlockSpec((tm, D_MODEL), tok),
                   pl.BlockSpec((tm, D_MODEL), tok)],
        out_shape=[jax.ShapeDtypeStruct((tokens, D_MODEL), F32),
                   jax.ShapeDtypeStruct((tokens, D_MODEL), BF16)],
        scratch_shapes=[pltpu.VMEM((tm, D_MODEL), F32)],
        compiler_params=_cparams(("arbitrary",)),
        name="out_proj",
    )(hm, u, x, mod, norm_w, pool_w, pool_scale.reshape(1, POOL_WIDTH), w_out)


def _mlp_kernel(h_ref, w1_ref, w2_ref, x1_ref, mod_ref, nw_ref, out_ref, acc_ref):
    k = pl.program_id(1)

    @pl.when(k == 0)
    def _():
        acc_ref[...] = jnp.zeros_like(acc_ref)

    a = jnp.dot(h_ref[...], w1_ref[...], preferred_element_type=F32)
    a = jnp.square(jnp.maximum(a, 0.0)).astype(BF16)
    acc_ref[...] += jnp.dot(a, w2_ref[...], preferred_element_type=F32)

    @pl.when(k == pl.num_programs(1) - 1)
    def _():
        out_ref[...] = x1_ref[...] + mod_ref[0, 5:6, :] * _rms(acc_ref[...], nw_ref[3:4, :])


def _mlp_call(h2, x1, mod, mod_map, norm_w, w1, w2, *, tm, tf):
    tokens = x1.shape[0]
    tok = lambda i, k: (i, 0)
    return pl.pallas_call(
        _mlp_kernel,
        grid=(tokens // tm, D_FF // tf),
        in_specs=[pl.BlockSpec((tm, D_MODEL), tok),
                  pl.BlockSpec((D_MODEL, tf), lambda i, k: (0, k)),
                  pl.BlockSpec((tf, D_MODEL), lambda i, k: (k, 0)),
                  pl.BlockSpec((tm, D_MODEL), tok),
                  pl.BlockSpec((1, N_MOD, D_MODEL), lambda i, k: mod_map(i)),
                  pl.BlockSpec((4, D_MODEL), lambda i, k: (0, 0))],
        out_specs=pl.BlockSpec((tm, D_MODEL), tok),
        out_shape=jax.ShapeDtypeStruct((tokens, D_MODEL), F32),
        scratch_shapes=[pltpu.VMEM((tm, D_MODEL), F32)],
        compiler_params=_cparams(("arbitrary", "arbitrary")),
        name="mlp",
    )(h2, w1, w2, x1, mod, norm_w)


def _gate_weights(w_g, gate_bias):
    hb = HEADS_PER_STEP
    nhb = N_HEADS // hb
    wg = w_g.reshape(D_MODEL, N_GATES, nhb, hb)
    wg = jnp.pad(wg, ((0, 0), (0, 0), (0, 0), (0, SUBLANES - hb)))
    wgt = wg.transpose(2, 1, 3, 0).reshape(nhb * GATE_ROWS, D_MODEL)
    gb = jnp.pad(gate_bias.reshape(N_GATES, nhb, hb), ((0, 0), (0, 0), (0, SUBLANES - hb)))
    gb = gb.transpose(1, 0, 2).reshape(nhb * GATE_ROWS)
    return wgt.astype(BF16), gb.astype(F32)


def _path(x, mod, mod_row, init, params, *, seg, tm, tm_mlp, emit_state):
    batch, seq_len, _ = x.shape
    wts, mlstm_nw, pool_w, pool_scale, w_out, norm_w, w1, w2 = params
    tokens = batch * seq_len
    xf = x.reshape(tokens, D_MODEL)
    q, kt, v, o, u, gates = _inproj_call(xf, mod, mod_row, norm_w, wts,
                                         batch=batch, seq_len=seq_len, tm=tm)
    shp = (batch, seq_len, MLSTM_WIDTH)
    res = _mlstm_call(q.reshape(shp), kt, v.reshape(shp), o.reshape(shp), gates, mlstm_nw, init,
                      batch=batch, seq_len=seq_len, emit_state=emit_state)
    hm = res[0].reshape(tokens, MLSTM_WIDTH)

    def mod_map_for(t):
        if mod_row is None:
            per = seq_len // t
            return lambda i: (1 + i // per, 0, 0)
        return lambda i: (mod_row, 0, 0)

    x1, h2 = _outproj_call(hm, u, xf, mod, mod_map_for(tm), norm_w, pool_w, pool_scale, w_out,
                           seg=seg, tm=tm)
    y = _mlp_call(h2, x1, mod, mod_map_for(tm_mlp), norm_w, w1, w2, tm=tm_mlp, tf=1024)
    return y.reshape(batch, seq_len, D_MODEL), res[1:]


def kernel(x_prompt, x_sample, c, state_C, state_n, state_m, c_ctx, w_in, gate_bias, mlstm_norm_w,
           pool_w, pool_scale, w_out, ada_w, ada_b, norm_w, w1, w2):
    hb = HEADS_PER_STEP
    nhb = N_HEADS // hb
    m4 = 4 * MLSTM_WIDTH
    yp, ys = x_prompt, x_sample
    new_c, new_n, new_m = [], [], []
    for layer in range(w_in.shape[0]):
        wl = w_in[layer]
        wgt, gb = _gate_weights(wl[:, m4:m4 + N_GATES * N_HEADS], gate_bias[layer])
        wts = (wl[:, 0:MLSTM_WIDTH].astype(BF16),
               wl[:, MLSTM_WIDTH:2 * MLSTM_WIDTH].T.astype(BF16),
               wl[:, 2 * MLSTM_WIDTH:3 * MLSTM_WIDTH].astype(BF16),
               wl[:, 3 * MLSTM_WIDTH:m4].astype(BF16),
               wl[:, m4 + N_GATES * N_HEADS:].astype(BF16),
               wgt, gb)
        params = (wts, mlstm_norm_w[layer].reshape(1, MLSTM_WIDTH), pool_w[layer].astype(BF16),
                  pool_scale[layer], w_out[layer].astype(BF16), norm_w[layer],
                  w1[layer].astype(BF16), w2[layer].astype(BF16))

        n_lat = c.shape[0]
        rows = 16
        cond = jnp.concatenate([c_ctx[None, :], c, jnp.zeros((rows - 1 - n_lat, D_MODEL), F32)], axis=0)
        mod = _mod_call(cond, ada_w[layer], ada_b[layer]).reshape(rows, N_MOD, D_MODEL)

        yp, st = _path(yp, mod, 0, None, params, seg=yp.shape[1], tm=512, tm_mlp=512, emit_state=True)
        cc, nn, mm = st
        new_c.append(cc)
        new_n.append(nn[..., 0])
        new_m.append(mm.reshape(mm.shape[0], nhb, 2, SUBLANES, LANES)[:, :, :, :hb, 0]
                     .transpose(0, 2, 1, 3).reshape(mm.shape[0], 2, N_HEADS))

        sc = state_C[:, layer].astype(F32)
        sn = jnp.broadcast_to(state_n[:, layer].astype(F32)[..., None],
                              (n_lat, 2, N_HEADS, HEAD_DIM, LANES))
        sm = state_m[:, layer].astype(F32).reshape(n_lat, 2, nhb, hb).transpose(0, 2, 1, 3)
        sm = jnp.pad(sm, ((0, 0), (0, 0), (0, 0), (0, SUBLANES - hb)))
        sm = jnp.broadcast_to(sm[..., None], (n_lat, nhb, 2, SUBLANES, LANES))
        ys, _ = _path(ys, mod, None, (sc, sn, sm), params, seg=GRID_W, tm=512, tm_mlp=512,
                      emit_state=False)
    return (yp, ys, jnp.stack(new_c, axis=1), jnp.stack(new_n, axis=1), jnp.stack(new_m, axis=1))
```

```python
import functools

import jax
import jax.numpy as jnp
from jax import lax
from jax.experimental import pallas as pl
from jax.experimental.pallas import tpu as pltpu

F32 = jnp.float32
BF16 = jnp.bfloat16

D_MODEL = 2048
MLSTM_WIDTH = 1024
N_HEADS = 8
HEAD_DIM = 128
POOL_WIDTH = 1024
POOL_WINDOWS = (2, 4, 8, 16)
POOL_GROUP_DIM = 256
N_GATES = 4
D_FF = 8192
CHUNK = 128
N_MOD = 6
GRID_W = 64
EPS = 1e-6

LANES = 128
SUBLANES = 8
VMEM_LIMIT = 56 * 1024 * 1024

HEADS_PER_STEP = 4
GATE_ROWS = N_GATES * SUBLANES


def _cparams(sem):
    return pltpu.CompilerParams(dimension_semantics=sem, vmem_limit_bytes=VMEM_LIMIT)


def _resident(shape):
    nd = len(shape)
    return pl.BlockSpec(shape, lambda *_: (0,) * nd, pipeline_mode=pl.Buffered(1))


def _rms(x, w):
    ms = jnp.mean(x * x, axis=-1, keepdims=True)
    return x * lax.rsqrt(ms + EPS) * w


def _mod_kernel(cond_ref, w_ref, b_ref, out_ref):
    c = cond_ref[...]
    s = c * jax.nn.sigmoid(c)
    out_ref[...] = jnp.dot(s.astype(BF16), w_ref[...].astype(BF16),
                           preferred_element_type=F32) + b_ref[...]


def _mod_call(cond, ada_w, ada_b):
    rows = cond.shape[0]
    n = ada_w.shape[1]
    tn = 1536
    return pl.pallas_call(
        _mod_kernel,
        grid=(n // tn,),
        in_specs=[pl.BlockSpec((rows, D_MODEL), lambda j: (0, 0)),
                  pl.BlockSpec((D_MODEL, tn), lambda j: (0, j)),
                  pl.BlockSpec((1, tn), lambda j: (0, j))],
        out_specs=pl.BlockSpec((rows, tn), lambda j: (0, j)),
        out_shape=jax.ShapeDtypeStruct((rows, n), F32),
        compiler_params=_cparams(("arbitrary",)),
        name="mod",
    )(cond, ada_w, ada_b.reshape(1, n))


NORM_SLICE_ROWS = 64


def _inproj_kernel(x_ref, mod_ref, nw_ref, wq_ref, wkg_ref, wv_ref, wo_ref, wu_ref, gb_ref,
                   q_ref, kt_ref, v_ref, o_ref, u_ref, g_ref, *, seqs, chunks):
    x = x_ref[...]
    mod = mod_ref[0]
    h = (_rms(x, nw_ref[0:1, :]) * (1.0 + mod[1:2, :]) + mod[0:1, :]).astype(BF16)
    q = jnp.dot(h, wq_ref[...], preferred_element_type=F32) * (HEAD_DIM ** -0.5)
    q_ref[...] = q.astype(BF16)
    v_ref[...] = jnp.dot(h, wv_ref[...], preferred_element_type=F32).astype(BF16)
    o_ref[...] = jnp.dot(h, wo_ref[...], preferred_element_type=F32)
    u_ref[...] = jnp.dot(h, wu_ref[...], preferred_element_type=F32)
    ktg = lax.dot_general(wkg_ref[...], h, (((1,), (1,)), ((), ())), preferred_element_type=F32)
    kt = ktg[0:MLSTM_WIDTH].astype(BF16)
    gt = ktg[MLSTM_WIDTH:] + gb_ref[...]
    for s in range(seqs):
        for c in range(chunks):
            lo = (s * chunks + c) * CHUNK
            kt_ref[s, c] = kt[:, lo:lo + CHUNK]
            g_ref[s, c] = gt[:, lo:lo + CHUNK]


def _inproj_call(x, mod, mod_row, norm_w, wts, *, batch, seq_len, tm):
    wq, wkg, wv, wo, wu, gb = wts
    tokens = batch * seq_len
    nc = seq_len // CHUNK
    grows = wkg.shape[0] - MLSTM_WIDTH
    if tm >= seq_len:
        seqs, chunks = tm // seq_len, nc
        kt_map = lambda i: (i, 0, 0, 0)
    else:
        seqs, chunks = 1, tm // CHUNK
        per = seq_len // tm
        kt_map = lambda i: (i // per, i % per, 0, 0)
    tiles_per_seq = max(seq_len // tm, 1)
    if mod_row is None:
        mod_map = lambda i: (1 + i // tiles_per_seq, 0, 0)
    else:
        mod_map = lambda i: (mod_row, 0, 0)
    tok = lambda i: (i, 0)
    gbb = jnp.broadcast_to(gb[:, None], (grows, tm))
    kern = functools.partial(_inproj_kernel, seqs=seqs, chunks=chunks)
    return pl.pallas_call(
        kern,
        grid=(tokens // tm,),
        in_specs=[pl.BlockSpec((tm, D_MODEL), tok),
                  pl.BlockSpec((1, N_MOD, D_MODEL), mod_map),
                  _resident((4, D_MODEL)),
                  _resident(wq.shape), _resident(wkg.shape), _resident(wv.shape),
                  _resident(wo.shape), _resident(wu.shape),
                  _resident((grows, tm))],
        out_specs=[pl.BlockSpec((tm, MLSTM_WIDTH), tok),
                   pl.BlockSpec((seqs, chunks, MLSTM_WIDTH, CHUNK), kt_map),
                   pl.BlockSpec((tm, MLSTM_WIDTH), tok),
                   pl.BlockSpec((tm, MLSTM_WIDTH), tok),
                   pl.BlockSpec((tm, POOL_WIDTH), tok),
                   pl.BlockSpec((seqs, chunks, grows, CHUNK), kt_map)],
        out_shape=[jax.ShapeDtypeStruct((tokens, MLSTM_WIDTH), BF16),
                   jax.ShapeDtypeStruct((batch, nc, MLSTM_WIDTH, CHUNK), BF16),
                   jax.ShapeDtypeStruct((tokens, MLSTM_WIDTH), BF16),
                   jax.ShapeDtypeStruct((tokens, MLSTM_WIDTH), F32),
                   jax.ShapeDtypeStruct((tokens, POOL_WIDTH), F32),
                   jax.ShapeDtypeStruct((batch, nc, grows, CHUNK), F32)],
        compiler_params=_cparams(("arbitrary",)),
        name="in_proj",
    )(x, mod, norm_w, wq, wkg, wv, wo, wu, gbb)


def _lane_scan(x, op, ident, reverse):
    lane = lax.broadcasted_iota(jnp.int32, x.shape, 1)
    k = 1
    while k < LANES:
        if reverse:
            shifted = pltpu.roll(x, LANES - k, axis=1)
            x = op(x, jnp.where(lane < LANES - k, shifted, ident))
        else:
            shifted = pltpu.roll(x, k, axis=1)
            x = op(x, jnp.where(lane >= k, shifted, ident))
        k *= 2
    return x


def _chunk_start(c):
    return c * CHUNK if isinstance(c, int) else pl.multiple_of(c * CHUNK, CHUNK)


def _mlstm_kernel(*refs, nc, hb, has_init, emit_state):
    it = iter(refs)
    q_ref, kt_ref, v_ref, o_ref, g_ref, nw_ref = (next(it) for _ in range(6))
    if has_init:
        c0_ref, n0_ref, m0_ref = (next(it) for _ in range(3))
    out_ref = next(it)
    if emit_state:
        cout_ref, nout_ref, mout_ref = (next(it) for _ in range(3))
    hf_sc, hb_sc, cn_sc, m_sc, xt_sc, rows_sc = (next(it) for _ in range(6))

    if has_init:
        for d in range(2):
            for j in range(hb):
                cn_sc[d * hb + j, :, 0:HEAD_DIM] = c0_ref[0, d, j]
                cn_sc[d * hb + j, :, HEAD_DIM:] = n0_ref[0, d, j]
    else:
        cn_sc[...] = jnp.zeros_like(cn_sc)

    neg_inf = jnp.float32(-jnp.inf)

    cols = []
    for d in range(2):
        li = g_ref[0, :, (2 * d) * SUBLANES:(2 * d + 1) * SUBLANES, :].reshape(nc * SUBLANES, LANES)
        lf = jax.nn.log_sigmoid(
            g_ref[0, :, (2 * d + 1) * SUBLANES:(2 * d + 2) * SUBLANES, :].reshape(nc * SUBLANES, LANES))
        b = _lane_scan(lf, jnp.add, 0.0, reverse=(d == 1))
        a = li - b
        pm = _lane_scan(a, jnp.maximum, neg_inf, reverse=(d == 1))
        amax = jnp.max(a, axis=1, keepdims=True)
        bsum = jnp.sum(lf, axis=1, keepdims=True)
        m = m0_ref[0, 0, d] if has_init else jnp.zeros((SUBLANES, LANES), F32)
        m_prev = [None] * nc
        for ci in (range(nc) if d == 0 else range(nc - 1, -1, -1)):
            rs = slice(ci * SUBLANES, (ci + 1) * SUBLANES)
            m_prev[ci] = m
            m = bsum[rs] + jnp.maximum(m, amax[rs])
        m_sc[d] = m
        mp = jnp.concatenate(m_prev, axis=0)
        mu = jnp.maximum(mp, pm)
        mu_last = jnp.maximum(mp, amax)
        rows_sc[0, d] = a.reshape(nc, SUBLANES, LANES)
        rows_sc[1, d] = jnp.exp(a - mu_last).reshape(nc, SUBLANES, LANES)
        rows_sc[2, d] = jnp.exp(mp - mu_last).reshape(nc, SUBLANES, LANES)
        cols.append((mu, jnp.exp(mp - mu), jnp.exp(-(b + mu))))
    zpad = jnp.zeros((LANES - 6 * SUBLANES, LANES), F32)
    for c in range(nc):
        pieces = []
        for d, ci in ((0, c), (1, nc - 1 - c)):
            pieces += [x[ci * SUBLANES:(ci + 1) * SUBLANES] for x in cols[d]]
        xt_sc[c] = jnp.concatenate(pieces + [zpad], axis=0).T

    row = lax.broadcasted_iota(jnp.int32, (CHUNK, CHUNK), 0)
    col = lax.broadcasted_iota(jnp.int32, (CHUNK, CHUNK), 1)
    causal = (col <= row, col >= row)
    ones_blk = jnp.ones((CHUNK, HEAD_DIM), BF16)

    def step(c, carry):
        chunk_of = (c, nc - 1 - c)
        xt = xt_sc[c]
        chains = []
        for d in range(2):
            ch = chunk_of[d]
            t0 = _chunk_start(ch)
            a_rows = rows_sc[0, d, ch]
            for j in range(hb):
                hs = slice(j * HEAD_DIM, (j + 1) * HEAD_DIM)
                qc = q_ref[0, pl.ds(t0, CHUNK), hs]
                base = 3 * d * SUBLANES + j
                mu_c = xt[:, base:base + 1]
                g_c = xt[:, base + SUBLANES:base + SUBLANES + 1]
                s = jnp.dot(qc, kt_ref[0, ch, hs, :], preferred_element_type=F32)
                p = s * jnp.exp(jnp.where(causal[d], a_rows[j:j + 1, :] - mu_c, neg_inf))
                gq = (qc.astype(F32) * g_c).astype(BF16)
                chains.append((d, j, ch, t0, jnp.concatenate([p.astype(BF16), gq], axis=1)))

        for d, j, ch, t0, lhs in chains:
            hs = slice(j * HEAD_DIM, (j + 1) * HEAD_DIM)
            base = 3 * d * SUBLANES + j
            en_c = xt[:, base + 2 * SUBLANES:base + 2 * SUBLANES + 1]
            v1 = jnp.concatenate([v_ref[0, pl.ds(t0, CHUNK), hs], ones_blk], axis=1)
            cn = cn_sc[d * hb + j]
            tot = jnp.dot(lhs, jnp.concatenate([v1, cn.astype(BF16)], axis=0),
                          preferred_element_type=F32)
            hval = tot[:, :HEAD_DIM] / jnp.maximum(jnp.abs(tot[:, HEAD_DIM:]), en_c)
            dst = hf_sc if d == 0 else hb_sc
            dst[pl.ds(t0, CHUNK), hs] = hval

            e_row = rows_sc[1, d, ch][j:j + 1, :]
            gs_row = rows_sc[2, d, ch][j:j + 1, :]
            wkt = (kt_ref[0, ch, hs, :].astype(F32) * e_row).astype(BF16)
            dcn = jnp.dot(wkt, v1, preferred_element_type=F32)
            gs_blk = jnp.broadcast_to(gs_row, (HEAD_DIM, LANES))
            cn_sc[d * hb + j] = jnp.concatenate([gs_blk, gs_blk], axis=1) * cn + dcn
        return carry

    if nc <= 2:
        for c in range(nc):
            step(c, 0)
    else:
        lax.fori_loop(0, nc, step, 0)

    def finalize(c, carry):
        t0 = _chunk_start(c)
        hsum = hf_sc[pl.ds(t0, CHUNK), :] + hb_sc[pl.ds(t0, CHUNK), :]
        og = jax.nn.sigmoid(o_ref[0, pl.ds(t0, CHUNK), :])
        for j in range(hb):
            hs = slice(j * HEAD_DIM, (j + 1) * HEAD_DIM)
            hn = _rms(hsum[:, hs], nw_ref[0:1, hs])
            out_ref[0, pl.ds(t0, CHUNK), hs] = (hn * og[:, hs]).astype(BF16)
        return carry

    if nc <= 2:
        for c in range(nc):
            finalize(c, 0)
    else:
        lax.fori_loop(0, nc, finalize, 0)

    if emit_state:
        for d in range(2):
            for j in range(hb):
                cout_ref[0, d, j] = cn_sc[d * hb + j, :, 0:HEAD_DIM]
                nout_ref[0, d, j] = cn_sc[d * hb + j, :, HEAD_DIM:]
            mout_ref[0, 0, d] = m_sc[d]


def _mlstm_call(q, kt, v, o, gates, nw, init, *, batch, seq_len, emit_state):
    hb = HEADS_PER_STEP
    nhb = N_HEADS // hb
    nc = seq_len // CHUNK
    w = hb * HEAD_DIM
    seq_map = lambda b, k: (b, 0, k)
    in_specs = [pl.BlockSpec((1, seq_len, w), seq_map),
                pl.BlockSpec((1, nc, w, CHUNK), lambda b, k: (b, 0, k, 0)),
                pl.BlockSpec((1, seq_len, w), seq_map),
                pl.BlockSpec((1, seq_len, w), seq_map),
                pl.BlockSpec((1, nc, GATE_ROWS, CHUNK), lambda b, k: (b, 0, k, 0)),
                pl.BlockSpec((1, w), lambda b, k: (0, k))]
    args = [q, kt, v, o, gates, nw]
    state_map = lambda b, k: (b, 0, k, 0, 0)
    m_map = lambda b, k: (b, k, 0, 0, 0)
    if init is not None:
        c0, n0, m0 = init
        in_specs += [pl.BlockSpec((1, 2, hb, HEAD_DIM, HEAD_DIM), state_map),
                     pl.BlockSpec((1, 2, hb, HEAD_DIM, LANES), state_map),
                     pl.BlockSpec((1, 1, 2, SUBLANES, LANES), m_map)]
        args += [c0, n0, m0]
    out_specs = [pl.BlockSpec((1, seq_len, w), seq_map)]
    out_shape = [jax.ShapeDtypeStruct((batch, seq_len, MLSTM_WIDTH), BF16)]
    if emit_state:
        out_specs += [pl.BlockSpec((1, 2, hb, HEAD_DIM, HEAD_DIM), state_map),
                      pl.BlockSpec((1, 2, hb, HEAD_DIM, LANES), state_map),
                      pl.BlockSpec((1, 1, 2, SUBLANES, LANES), m_map)]
        out_shape += [jax.ShapeDtypeStruct((batch, 2, N_HEADS, HEAD_DIM, HEAD_DIM), F32),
                      jax.ShapeDtypeStruct((batch, 2, N_HEADS, HEAD_DIM, LANES), F32),
                      jax.ShapeDtypeStruct((batch, nhb, 2, SUBLANES, LANES), F32)]
    kern = functools.partial(_mlstm_kernel, nc=nc, hb=hb, has_init=init is not None,
                             emit_state=emit_state)
    return pl.pallas_call(
        kern,
        grid=(batch, nhb),
        in_specs=in_specs,
        out_specs=out_specs,
        out_shape=out_shape,
        scratch_shapes=[pltpu.VMEM((seq_len, w), F32),
                        pltpu.VMEM((seq_len, w), F32),
                        pltpu.VMEM((2 * hb, HEAD_DIM, 2 * HEAD_DIM), F32),
                        pltpu.VMEM((2, SUBLANES, LANES), F32),
                        pltpu.VMEM((nc, LANES, LANES), F32),
                        pltpu.VMEM((3, 2, nc, SUBLANES, LANES), F32)],
        compiler_params=_cparams(("arbitrary", "arbitrary")),
        name="mlstm",
    )(*args)


def _shift_rows(x, dlt):
    return pltpu.roll(x, (-dlt) % x.shape[0], axis=0)


def _pool_minus_self(x, half, pos, seg, inv_cnt):
    fwd = x
    k = 1
    while k < half:
        fwd = fwd + jnp.where(pos + k < seg, _shift_rows(fwd, k), 0.0)
        k *= 2
    bwd = jnp.where(pos >= 1, _shift_rows(x, -1), 0.0)
    k = 1
    while k < half:
        bwd = bwd + jnp.where(pos >= k, _shift_rows(bwd, -k), 0.0)
        k *= 2
    return (fwd + bwd) * inv_cnt - x


OUTPROJ_ROW_BLOCKS = 4
OUTPROJ_EPI_ROWS = 64


def _outproj_kernel(hm_ref, u_ref, x_ref, mod_ref, nw_ref, pw_ref, ps_ref, wout_ref,
                    x1_ref, h2_ref, mix_ref, *, seg):
    tm = x_ref.shape[0]
    nblk = min(OUTPROJ_ROW_BLOCKS, tm // seg)
    rb = tm // nblk
    nslice = 2 * len(POOL_WINDOWS)
    ncol = D_MODEL // nslice
    mod = mod_ref[0]
    gain1 = nw_ref[1:2, :] * mod[2:3, :]
    gain2 = nw_ref[2:3, :] * (1.0 + mod[4:5, :])
    shift2 = mod[3:4, :]
    pos = lax.broadcasted_iota(jnp.int32, (rb, 1), 0) % seg
    inv_cnt = [1.0 / (jnp.minimum(pos + w // 2, seg) - jnp.maximum(pos - w // 2, 0)).astype(F32)
               for w in POOL_WINDOWS]

    def epilogue_slices(r):
        def one(lo):
            rows = slice(lo, lo + OUTPROJ_EPI_ROWS)
            mix = mix_ref[rows, :]
            ms = jnp.mean(mix * mix, axis=-1, keepdims=True)
            x1 = x_ref[rows, :] + (mix * lax.rsqrt(ms + EPS)) * gain1
            x1_ref[rows, :] = x1
            ms2 = jnp.mean(x1 * x1, axis=-1, keepdims=True)
            h2_ref[rows, :] = ((x1 * lax.rsqrt(ms2 + EPS)) * gain2 + shift2).astype(BF16)
        return [functools.partial(one, r * rb + k * OUTPROJ_EPI_ROWS)
                for k in range(rb // OUTPROJ_EPI_ROWS)]

    def run_block(r, pending):
        rows = slice(r * rb, (r + 1) * rb)
        hm = hm_ref[rows, :]
        parts = []
        for gi, win in enumerate(POOL_WINDOWS):
            halves = []
            for hh in range(2):
                idx = 2 * gi + hh
                ns = slice(idx * ncol, (idx + 1) * ncol)
                mix_ref[rows, ns] = jnp.dot(hm, wout_ref[0:MLSTM_WIDTH, ns], preferred_element_type=F32)
                lo = gi * POOL_GROUP_DIM + hh * LANES
                halves.append(_pool_minus_self(u_ref[rows, lo:lo + LANES], win // 2, pos, seg,
                                               inv_cnt[gi]).astype(BF16))
                if pending and idx % 4 == 3:
                    pending.pop(0)()
            cs = slice(gi * POOL_GROUP_DIM, (gi + 1) * POOL_GROUP_DIM)
            y = jnp.dot(jnp.concatenate(halves, axis=1), pw_ref[gi], preferred_element_type=F32)
            parts.append((y * ps_ref[0:1, cs]).astype(BF16))
        hp = jnp.concatenate(parts, axis=1)
        for idx in range(nslice):
            ns = slice(idx * ncol, (idx + 1) * ncol)
            mix_ref[rows, ns] += jnp.dot(hp, wout_ref[MLSTM_WIDTH:, ns], preferred_element_type=F32)
            if pending and idx % 4 == 3:
                pending.pop(0)()
        while pending:
            pending.pop(0)()

    pending = []
    for r in range(nblk):
        run_block(r, pending)
        pending = epilogue_slices(r)
    for fn in pending:
        fn()


def _outproj_call(hm, u, x, mod, mod_map, norm_w, pool_w, pool_scale, w_out, *, seg, tm):
    tokens = x.shape[0]
    tok = lambda i: (i, 0)
    return pl.pallas_call(
        functools.partial(_outproj_kernel, seg=seg),
        grid=(tokens // tm,),
        in_specs=[pl.BlockSpec((tm, MLSTM_WIDTH), tok),
                  pl.BlockSpec((tm, POOL_WIDTH), tok),
                  pl.BlockSpec((tm, D_MODEL), tok),
                  pl.BlockSpec((1, N_MOD, D_MODEL), mod_map),
                  _resident((4, D_MODEL)),
                  _resident(pool_w.shape),
                  _resident((1, POOL_WIDTH)),
                  _resident(w_out.shape)],
        out_specs=[pl.BlockSpec((tm, D_MODEL), tok),
                   pl.BlockSpec((tm, D_MODEL), tok)],
        out_shape=[jax.ShapeDtypeStruct((tokens, D_MODEL), F32),
                   jax.ShapeDtypeStruct((tokens, D_MODEL), BF16)],
        scratch_shapes=[pltpu.VMEM((tm, D_MODEL), F32)],
        compiler_params=_cparams(("arbitrary",)),
        name="out_proj",
    )(hm, u, x, mod, norm_w, pool_w, pool_scale.reshape(1, POOL_WIDTH), w_out)


MLP_LAST_ROW_BLOCKS = 4
MLP_SLICE_COLS = 256


def _mlp_kernel(h_ref, w1_ref, w2_ref, x1_ref, mod_ref, nw_ref, out_ref, acc_ref):
    k = pl.program_id(1)
    last = pl.num_programs(1) - 1
    tm, tf = h_ref.shape[0], w1_ref.shape[1]

    def hidden(rows, cols=slice(None)):
        a = jnp.dot(h_ref[rows, :], w1_ref[:, cols], preferred_element_type=F32)
        return jnp.square(jnp.maximum(a, 0.0)).astype(BF16)

    @pl.when(k == 0)
    def _():
        acc_ref[...] = jnp.dot(hidden(slice(None)), w2_ref[...], preferred_element_type=F32)

    @pl.when((k > 0) & (k < last))
    def _():
        acc_ref[...] += jnp.dot(hidden(slice(None)), w2_ref[...], preferred_element_type=F32)

    @pl.when(k == last)
    def _():
        rb = tm // MLP_LAST_ROW_BLOCKS
        gain = nw_ref[3:4, :] * mod_ref[0, 5:6, :]

        def epilogue_slices(r):
            def one(lo):
                rows = slice(lo, lo + NORM_SLICE_ROWS)
                y = acc_ref[rows, :]
                ms = jnp.mean(y * y, axis=-1, keepdims=True)
                out_ref[rows, :] = x1_ref[rows, :] + (y * lax.rsqrt(ms + EPS)) * gain
            return [functools.partial(one, r * rb + j * NORM_SLICE_ROWS)
                    for j in range(rb // NORM_SLICE_ROWS)]

        pending = []
        n1, n2 = tf // MLP_SLICE_COLS, D_MODEL // MLP_SLICE_COLS
        for r in range(MLP_LAST_ROW_BLOCKS):
            rows = slice(r * rb, (r + 1) * rb)
            parts = []
            for idx in range(n1):
                parts.append(hidden(rows, slice(idx * MLP_SLICE_COLS, (idx + 1) * MLP_SLICE_COLS)))
                if pending and idx % 2 == 1:
                    pending.pop(0)()
            a = jnp.concatenate(parts, axis=1)
            for idx in range(n2):
                ns = slice(idx * MLP_SLICE_COLS, (idx + 1) * MLP_SLICE_COLS)
                acc_ref[rows, ns] += jnp.dot(a, w2_ref[:, ns], preferred_element_type=F32)
                if pending and idx % 2 == 1:
                    pending.pop(0)()
            while pending:
                pending.pop(0)()
            pending = epilogue_slices(r)
        for fn in pending:
            fn()


def _mlp_call(h2, x1, mod, mod_map, norm_w, w1, w2, *, tm, tf):
    tokens = x1.shape[0]
    tok = lambda i, k: (i, 0)
    return pl.pallas_call(
        _mlp_kernel,
        grid=(tokens // tm, D_FF // tf),
        in_specs=[pl.BlockSpec((tm, D_MODEL), tok),
                  pl.BlockSpec((D_MODEL, tf), lambda i, k: (0, k)),
                  pl.BlockSpec((tf, D_MODEL), lambda i, k: (k, 0)),
                  pl.BlockSpec((tm, D_MODEL), tok),
                  pl.BlockSpec((1, N_MOD, D_MODEL), lambda i, k: mod_map(i)),
                  pl.BlockSpec((4, D_MODEL), lambda i, k: (0, 0))],
        out_specs=pl.BlockSpec((tm, D_MODEL), tok),
        out_shape=jax.ShapeDtypeStruct((tokens, D_MODEL), F32),
        scratch_shapes=[pltpu.VMEM((tm, D_MODEL), F32)],
        compiler_params=_cparams(("arbitrary", "arbitrary")),
        name="mlp",
    )(h2, w1, w2, x1, mod, norm_w)


def _gate_weights(w_g, gate_bias):
    hb = HEADS_PER_STEP
    nhb = N_HEADS // hb
    wg = w_g.reshape(D_MODEL, N_GATES, nhb, hb)
    wg = jnp.pad(wg, ((0, 0), (0, 0), (0, 0), (0, SUBLANES - hb)))
    wgt = wg.transpose(2, 1, 3, 0).reshape(nhb * GATE_ROWS, D_MODEL)
    gb = jnp.pad(gate_bias.reshape(N_GATES, nhb, hb), ((0, 0), (0, 0), (0, SUBLANES - hb)))
    gb = gb.transpose(1, 0, 2).reshape(nhb * GATE_ROWS)
    return wgt.astype(BF16), gb.astype(F32)


def _path(x, mod, mod_row, init, params, *, seg, tm, tm_mlp, emit_state):
    batch, seq_len, _ = x.shape
    wts, mlstm_nw, pool_w, pool_scale, w_out, norm_w, w1, w2 = params
    tokens = batch * seq_len
    xf = x.reshape(tokens, D_MODEL)
    q, kt, v, o, u, gates = _inproj_call(xf, mod, mod_row, norm_w, wts,
                                         batch=batch, seq_len=seq_len, tm=tm)
    shp = (batch, seq_len, MLSTM_WIDTH)
    res = _mlstm_call(q.reshape(shp), kt, v.reshape(shp), o.reshape(shp), gates, mlstm_nw, init,
                      batch=batch, seq_len=seq_len, emit_state=emit_state)
    hm = res[0].reshape(tokens, MLSTM_WIDTH)

    def mod_map_for(t):
        if mod_row is None:
            per = seq_len // t
            return lambda i: (1 + i // per, 0, 0)
        return lambda i: (mod_row, 0, 0)

    x1, h2 = _outproj_call(hm, u, xf, mod, mod_map_for(tm), norm_w, pool_w, pool_scale, w_out,
                           seg=seg, tm=tm)
    y = _mlp_call(h2, x1, mod, mod_map_for(tm_mlp), norm_w, w1, w2, tm=tm_mlp, tf=1024)
    return y.reshape(batch, seq_len, D_MODEL), res[1:]


def kernel(x_prompt, x_sample, c, state_C, state_n, state_m, c_ctx, w_in, gate_bias, mlstm_norm_w,
           pool_w, pool_scale, w_out, ada_w, ada_b, norm_w, w1, w2):
    hb = HEADS_PER_STEP
    nhb = N_HEADS // hb
    m4 = 4 * MLSTM_WIDTH
    yp, ys = x_prompt, x_sample
    new_c, new_n, new_m = [], [], []
    for layer in range(w_in.shape[0]):
        wl = w_in[layer]
        wgt, gb = _gate_weights(wl[:, m4:m4 + N_GATES * N_HEADS], gate_bias[layer])
        wkg = jnp.concatenate([wl[:, MLSTM_WIDTH:2 * MLSTM_WIDTH].T.astype(BF16), wgt], axis=0)
        wts = (wl[:, 0:MLSTM_WIDTH].astype(BF16),
               wkg,
               wl[:, 2 * MLSTM_WIDTH:3 * MLSTM_WIDTH].astype(BF16),
               wl[:, 3 * MLSTM_WIDTH:m4].astype(BF16),
               wl[:, m4 + N_GATES * N_HEADS:].astype(BF16),
               gb)
        params = (wts, mlstm_norm_w[layer].reshape(1, MLSTM_WIDTH), pool_w[layer].astype(BF16),
                  pool_scale[layer], w_out[layer].astype(BF16), norm_w[layer],
                  w1[layer].astype(BF16), w2[layer].astype(BF16))

        n_lat = c.shape[0]
        rows = 16
        cond = jnp.concatenate([c_ctx[None, :], c, jnp.zeros((rows - 1 - n_lat, D_MODEL), F32)], axis=0)
        mod = _mod_call(cond, ada_w[layer], ada_b[layer]).reshape(rows, N_MOD, D_MODEL)

        yp, st = _path(yp, mod, 0, None, params, seg=yp.shape[1], tm=512, tm_mlp=512, emit_state=True)
        cc, nn, mm = st
        new_c.append(cc)
        new_n.append(nn[..., 0])
        new_m.append(mm.reshape(mm.shape[0], nhb, 2, SUBLANES, LANES)[:, :, :, :hb, 0]
                     .transpose(0, 2, 1, 3).reshape(mm.shape[0], 2, N_HEADS))

        sc = state_C[:, layer].astype(F32)
        sn = jnp.broadcast_to(state_n[:, layer].astype(F32)[..., None],
                              (n_lat, 2, N_HEADS, HEAD_DIM, LANES))
        sm = state_m[:, layer].astype(F32).reshape(n_lat, 2, nhb, hb).transpose(0, 2, 1, 3)
        sm = jnp.pad(sm, ((0, 0), (0, 0), (0, 0), (0, SUBLANES - hb)))
        sm = jnp.broadcast_to(sm[..., None], (n_lat, nhb, 2, SUBLANES, LANES))
        ys, _ = _path(ys, mod, None, (sc, sn, sm), params, seg=GRID_W, tm=512, tm_mlp=512,
                      emit_state=False)
    return (yp, ys, jnp.stack(new_c, axis=1), jnp.stack(new_n, axis=1), jnp.stack(new_m, axis=1))
```

```python
import functools

import jax
import jax.numpy as jnp
from jax import lax
from jax.experimental import pallas as pl
from jax.experimental.pallas import tpu as pltpu

F32 = jnp.float32
BF16 = jnp.bfloat16

D_MODEL = 2048
MLSTM_WIDTH = 1024
N_HEADS = 8
HEAD_DIM = 128
POOL_WIDTH = 1024
POOL_WINDOWS = (2, 4, 8, 16)
POOL_GROUP_DIM = 256
N_GATES = 4
D_FF = 8192
CHUNK = 128
N_MOD = 6
GRID_W = 64
EPS = 1e-6

LANES = 128
SUBLANES = 8
VMEM_LIMIT = 56 * 1024 * 1024

HEADS_PER_STEP = 4
HEAD_BLOCK = HEADS_PER_STEP * HEAD_DIM
GATE_ROWS = N_GATES * SUBLANES


def _cparams(sem):
    return pltpu.CompilerParams(dimension_semantics=sem, vmem_limit_bytes=VMEM_LIMIT)


def _resident(shape):
    nd = len(shape)
    return pl.BlockSpec(shape, lambda *_: (0,) * nd, pipeline_mode=pl.Buffered(1))


def _rms(x, w):
    ms = jnp.mean(x * x, axis=-1, keepdims=True)
    return x * lax.rsqrt(ms + EPS) * w


def _mod_kernel(cond_ref, w_ref, b_ref, out_ref):
    c = cond_ref[...]
    s = c * jax.nn.sigmoid(c)
    out_ref[...] = jnp.dot(s.astype(BF16), w_ref[...].astype(BF16),
                           preferred_element_type=F32) + b_ref[...]


def _mod_call(cond, ada_w, ada_b):
    rows = cond.shape[0]
    n = ada_w.shape[1]
    tn = 1536
    return pl.pallas_call(
        _mod_kernel,
        grid=(n // tn,),
        in_specs=[pl.BlockSpec((rows, D_MODEL), lambda j: (0, 0)),
                  pl.BlockSpec((D_MODEL, tn), lambda j: (0, j)),
                  pl.BlockSpec((1, tn), lambda j: (0, j))],
        out_specs=pl.BlockSpec((rows, tn), lambda j: (0, j)),
        out_shape=jax.ShapeDtypeStruct((rows, n), F32),
        compiler_params=_cparams(("arbitrary",)),
        name="mod",
    )(cond, ada_w, ada_b.reshape(1, n))


NORM_SLICE_ROWS = 64


def _inproj_kernel(x_ref, mod_ref, nw_ref, wq_ref, wkg_ref, wv_ref, wo_ref, wu_ref, gb_ref,
                   q_ref, kt_ref, v_ref, o_ref, u_ref, g_ref, *, seqs, chunks):
    x = x_ref[...]
    mod = mod_ref[0]
    h = (_rms(x, nw_ref[0:1, :]) * (1.0 + mod[1:2, :]) + mod[0:1, :]).astype(BF16)
    q = (jnp.dot(h, wq_ref[...], preferred_element_type=F32) * (HEAD_DIM ** -0.5)).astype(BF16)
    v = jnp.dot(h, wv_ref[...], preferred_element_type=F32).astype(BF16)
    o = jnp.dot(h, wo_ref[...], preferred_element_type=F32)
    srows = x.shape[0] // seqs
    for s in range(seqs):
        for hbk in range(q_ref.shape[1]):
            blk = (slice(s * srows, (s + 1) * srows), slice(hbk * HEAD_BLOCK, (hbk + 1) * HEAD_BLOCK))
            q_ref[s, hbk] = q[blk]
            v_ref[s, hbk] = v[blk]
            o_ref[s, hbk] = o[blk]
    u_ref[...] = jnp.dot(h, wu_ref[...], preferred_element_type=F32)
    ktg = lax.dot_general(wkg_ref[...], h, (((1,), (1,)), ((), ())), preferred_element_type=F32)
    kt = ktg[0:MLSTM_WIDTH].astype(BF16)
    gt = ktg[MLSTM_WIDTH:] + gb_ref[...]
    for s in range(seqs):
        for c in range(chunks):
            lo = (s * chunks + c) * CHUNK
            kt_ref[s, c] = kt[:, lo:lo + CHUNK]
            g_ref[s, c] = gt[:, lo:lo + CHUNK]


def _inproj_call(x, mod, mod_row, norm_w, wts, *, batch, seq_len, tm):
    wq, wkg, wv, wo, wu, gb = wts
    tokens = batch * seq_len
    nc = seq_len // CHUNK
    grows = wkg.shape[0] - MLSTM_WIDTH
    nhb = N_HEADS // HEADS_PER_STEP
    if tm >= seq_len:
        seqs, chunks = tm // seq_len, nc
        kt_map = lambda i: (i, 0, 0, 0)
        hb_spec = pl.BlockSpec((seqs, nhb, seq_len, HEAD_BLOCK), kt_map)
    else:
        seqs, chunks = 1, tm // CHUNK
        per = seq_len // tm
        kt_map = lambda i: (i // per, i % per, 0, 0)
        hb_spec = pl.BlockSpec((1, nhb, tm, HEAD_BLOCK), lambda i: (i // per, 0, i % per, 0))
    hb_shape = (batch, nhb, seq_len, HEAD_BLOCK)
    tiles_per_seq = max(seq_len // tm, 1)
    if mod_row is None:
        mod_map = lambda i: (1 + i // tiles_per_seq, 0, 0)
    else:
        mod_map = lambda i: (mod_row, 0, 0)
    tok = lambda i: (i, 0)
    gbb = jnp.broadcast_to(gb[:, None], (grows, tm))
    kern = functools.partial(_inproj_kernel, seqs=seqs, chunks=chunks)
    return pl.pallas_call(
        kern,
        grid=(tokens // tm,),
        in_specs=[pl.BlockSpec((tm, D_MODEL), tok),
                  pl.BlockSpec((1, N_MOD, D_MODEL), mod_map),
                  _resident((4, D_MODEL)),
                  _resident(wq.shape), _resident(wkg.shape), _resident(wv.shape),
                  _resident(wo.shape), _resident(wu.shape),
                  _resident((grows, tm))],
        out_specs=[hb_spec,
                   pl.BlockSpec((seqs, chunks, MLSTM_WIDTH, CHUNK), kt_map),
                   hb_spec,
                   hb_spec,
                   pl.BlockSpec((tm, POOL_WIDTH), tok),
                   pl.BlockSpec((seqs, chunks, grows, CHUNK), kt_map)],
        out_shape=[jax.ShapeDtypeStruct(hb_shape, BF16),
                   jax.ShapeDtypeStruct((batch, nc, MLSTM_WIDTH, CHUNK), BF16),
                   jax.ShapeDtypeStruct(hb_shape, BF16),
                   jax.ShapeDtypeStruct(hb_shape, F32),
                   jax.ShapeDtypeStruct((tokens, POOL_WIDTH), F32),
                   jax.ShapeDtypeStruct((batch, nc, grows, CHUNK), F32)],
        compiler_params=_cparams(("arbitrary",)),
        name="in_proj",
    )(x, mod, norm_w, wq, wkg, wv, wo, wu, gbb)


def _lane_scan_pair(fwd, bwd, op, ident):
    lane = lax.broadcasted_iota(jnp.int32, fwd.shape, 1)
    k = 1
    while k < LANES:
        f_shift = pltpu.roll(fwd, k, axis=1)
        b_shift = pltpu.roll(bwd, LANES - k, axis=1)
        fwd = op(fwd, jnp.where(lane >= k, f_shift, ident))
        bwd = op(bwd, jnp.where(lane < LANES - k, b_shift, ident))
        k *= 2
    return fwd, bwd


def _chunk_start(c):
    return c * CHUNK if isinstance(c, int) else pl.multiple_of(c * CHUNK, CHUNK)


def _mlstm_kernel(*refs, nc, hb, has_init, emit_state):
    it = iter(refs)
    q_ref, kt_ref, v_ref, o_ref, g_ref, nw_ref = (next(it) for _ in range(6))
    if has_init:
        c0_ref, n0_ref, m0_ref = (next(it) for _ in range(3))
    out_ref = next(it)
    if emit_state:
        cout_ref, nout_ref, mout_ref = (next(it) for _ in range(3))
    hf_sc, hb_sc, cn_sc, m_sc, xt_sc, rows_sc = (next(it) for _ in range(6))

    if has_init:
        for d in range(2):
            for j in range(hb):
                cn_sc[d * hb + j, :, 0:HEAD_DIM] = c0_ref[0, d, j]
                cn_sc[d * hb + j, :, HEAD_DIM:] = n0_ref[0, d, j]
    else:
        cn_sc[...] = jnp.zeros_like(cn_sc)

    neg_inf = jnp.float32(-jnp.inf)

    def gate_rows(g):
        return g_ref[0, :, g * SUBLANES:(g + 1) * SUBLANES, :].reshape(nc * SUBLANES, LANES)

    li = [gate_rows(0), gate_rows(2)]
    lf = [jax.nn.log_sigmoid(gate_rows(1)), jax.nn.log_sigmoid(gate_rows(3))]
    b = _lane_scan_pair(lf[0], lf[1], jnp.add, 0.0)
    a = [li[0] - b[0], li[1] - b[1]]
    pm = _lane_scan_pair(a[0], a[1], jnp.maximum, neg_inf)
    cols = []
    for d in range(2):
        amax = jnp.max(a[d], axis=1, keepdims=True)
        bsum = jnp.sum(lf[d], axis=1, keepdims=True)
        m = m0_ref[0, 0, d] if has_init else jnp.zeros((SUBLANES, LANES), F32)
        m_prev = [None] * nc
        for ci in (range(nc) if d == 0 else range(nc - 1, -1, -1)):
            rs = slice(ci * SUBLANES, (ci + 1) * SUBLANES)
            m_prev[ci] = m
            m = bsum[rs] + jnp.maximum(m, amax[rs])
        m_sc[d] = m
        mp = jnp.concatenate(m_prev, axis=0)
        mu = jnp.maximum(mp, pm[d])
        mu_last = jnp.maximum(mp, amax)
        rows_sc[0, d] = a[d].reshape(nc, SUBLANES, LANES)
        rows_sc[1, d] = jnp.exp(a[d] - mu_last).reshape(nc, SUBLANES, LANES)
        rows_sc[2, d] = jnp.exp(mp - mu_last).reshape(nc, SUBLANES, LANES)
        cols.append((mu, jnp.exp(mp - mu), jnp.exp(-(b[d] + mu))))
    zpad = jnp.zeros((LANES - 6 * SUBLANES, LANES), F32)
    for c in range(nc):
        pieces = []
        for d, ci in ((0, c), (1, nc - 1 - c)):
            pieces += [x[ci * SUBLANES:(ci + 1) * SUBLANES] for x in cols[d]]
        xt_sc[c] = jnp.concatenate(pieces + [zpad], axis=0).T

    row = lax.broadcasted_iota(jnp.int32, (CHUNK, CHUNK), 0)
    col = lax.broadcasted_iota(jnp.int32, (CHUNK, CHUNK), 1)
    causal = (col <= row, col >= row)
    ones_blk = jnp.ones((CHUNK, HEAD_DIM), BF16)

    def step(c, carry):
        chunk_of = (c, nc - 1 - c)
        xt = xt_sc[c]
        chains = []
        for d in range(2):
            ch = chunk_of[d]
            t0 = _chunk_start(ch)
            a_rows = rows_sc[0, d, ch]
            for j in range(hb):
                hs = slice(j * HEAD_DIM, (j + 1) * HEAD_DIM)
                qc = q_ref[0, 0, pl.ds(t0, CHUNK), hs]
                base = 3 * d * SUBLANES + j
                mu_c = xt[:, base:base + 1]
                g_c = xt[:, base + SUBLANES:base + SUBLANES + 1]
                s = jnp.dot(qc, kt_ref[0, ch, hs, :], preferred_element_type=F32)
                p = s * jnp.exp(jnp.where(causal[d], a_rows[j:j + 1, :] - mu_c, neg_inf))
                gq = (qc.astype(F32) * g_c).astype(BF16)
                chains.append((d, j, ch, t0, jnp.concatenate([p.astype(BF16), gq], axis=1)))

        for d, j, ch, t0, lhs in chains:
            hs = slice(j * HEAD_DIM, (j + 1) * HEAD_DIM)
            base = 3 * d * SUBLANES + j
            en_c = xt[:, base + 2 * SUBLANES:base + 2 * SUBLANES + 1]
            v1 = jnp.concatenate([v_ref[0, 0, pl.ds(t0, CHUNK), hs], ones_blk], axis=1)
            cn = cn_sc[d * hb + j]
            tot = jnp.dot(lhs, jnp.concatenate([v1, cn.astype(BF16)], axis=0),
                          preferred_element_type=F32)
            hval = tot[:, :HEAD_DIM] / jnp.maximum(jnp.abs(tot[:, HEAD_DIM:]), en_c)
            dst = hf_sc if d == 0 else hb_sc
            dst[pl.ds(t0, CHUNK), hs] = hval

            e_row = rows_sc[1, d, ch][j:j + 1, :]
            gs_row = rows_sc[2, d, ch][j:j + 1, :]
            wkt = (kt_ref[0, ch, hs, :].astype(F32) * e_row).astype(BF16)
            dcn = jnp.dot(wkt, v1, preferred_element_type=F32)
            gs_blk = jnp.broadcast_to(gs_row, (HEAD_DIM, LANES))
            cn_sc[d * hb + j] = jnp.concatenate([gs_blk, gs_blk], axis=1) * cn + dcn
        return carry

    if nc <= 2:
        for c in range(nc):
            step(c, 0)
    else:
        lax.fori_loop(0, nc, step, 0)

    def finalize(c, carry):
        t0 = _chunk_start(c)
        hsum = hf_sc[pl.ds(t0, CHUNK), :] + hb_sc[pl.ds(t0, CHUNK), :]
        og = jax.nn.sigmoid(o_ref[0, 0, pl.ds(t0, CHUNK), :])
        for j in range(hb):
            hs = slice(j * HEAD_DIM, (j + 1) * HEAD_DIM)
            hn = _rms(hsum[:, hs], nw_ref[0:1, hs])
            out_ref[0, 0, pl.ds(t0, CHUNK), hs] = (hn * og[:, hs]).astype(BF16)
        return carry

    if nc <= 2:
        for c in range(nc):
            finalize(c, 0)
    else:
        lax.fori_loop(0, nc, finalize, 0)

    if emit_state:
        for d in range(2):
            for j in range(hb):
                cout_ref[0, d, j] = cn_sc[d * hb + j, :, 0:HEAD_DIM]
                nout_ref[0, d, j] = cn_sc[d * hb + j, :, HEAD_DIM:]
            mout_ref[0, 0, d] = m_sc[d]


def _mlstm_call(q, kt, v, o, gates, nw, init, *, batch, seq_len, emit_state):
    hb = HEADS_PER_STEP
    nhb = N_HEADS // hb
    nc = seq_len // CHUNK
    w = HEAD_BLOCK
    seq_map = lambda b, k: (b, k, 0, 0)
    in_specs = [pl.BlockSpec((1, 1, seq_len, w), seq_map),
                pl.BlockSpec((1, nc, w, CHUNK), lambda b, k: (b, 0, k, 0)),
                pl.BlockSpec((1, 1, seq_len, w), seq_map),
                pl.BlockSpec((1, 1, seq_len, w), seq_map),
                pl.BlockSpec((1, nc, GATE_ROWS, CHUNK), lambda b, k: (b, 0, k, 0)),
                pl.BlockSpec((1, w), lambda b, k: (0, k))]
    args = [q, kt, v, o, gates, nw]
    state_map = lambda b, k: (b, 0, k, 0, 0)
    m_map = lambda b, k: (b, k, 0, 0, 0)
    if init is not None:
        c0, n0, m0 = init
        in_specs += [pl.BlockSpec((1, 2, hb, HEAD_DIM, HEAD_DIM), state_map),
                     pl.BlockSpec((1, 2, hb, HEAD_DIM, LANES), state_map),
                     pl.BlockSpec((1, 1, 2, SUBLANES, LANES), m_map)]
        args += [c0, n0, m0]
    out_specs = [pl.BlockSpec((1, 1, seq_len, w), seq_map)]
    out_shape = [jax.ShapeDtypeStruct((batch, nhb, seq_len, w), BF16)]
    if emit_state:
        out_specs += [pl.BlockSpec((1, 2, hb, HEAD_DIM, HEAD_DIM), state_map),
                      pl.BlockSpec((1, 2, hb, HEAD_DIM, LANES), state_map),
                      pl.BlockSpec((1, 1, 2, SUBLANES, LANES), m_map)]
        out_shape += [jax.ShapeDtypeStruct((batch, 2, N_HEADS, HEAD_DIM, HEAD_DIM), F32),
                      jax.ShapeDtypeStruct((batch, 2, N_HEADS, HEAD_DIM, LANES), F32),
                      jax.ShapeDtypeStruct((batch, nhb, 2, SUBLANES, LANES), F32)]
    kern = functools.partial(_mlstm_kernel, nc=nc, hb=hb, has_init=init is not None,
                             emit_state=emit_state)
    return pl.pallas_call(
        kern,
        grid=(batch, nhb),
        in_specs=in_specs,
        out_specs=out_specs,
        out_shape=out_shape,
        scratch_shapes=[pltpu.VMEM((seq_len, w), F32),
                        pltpu.VMEM((seq_len, w), F32),
                        pltpu.VMEM((2 * hb, HEAD_DIM, 2 * HEAD_DIM), F32),
                        pltpu.VMEM((2, SUBLANES, LANES), F32),
                        pltpu.VMEM((nc, LANES, LANES), F32),
                        pltpu.VMEM((3, 2, nc, SUBLANES, LANES), F32)],
        compiler_params=_cparams(("arbitrary", "arbitrary")),
        name="mlstm",
    )(*args)


def _shift_rows(x, dlt):
    return pltpu.roll(x, (-dlt) % x.shape[0], axis=0)


def _pool_minus_self(x, half, pos, seg, inv_cnt):
    fwd = x
    k = 1
    while k < half:
        fwd = fwd + jnp.where(pos + k < seg, _shift_rows(fwd, k), 0.0)
        k *= 2
    bwd = jnp.where(pos >= 1, _shift_rows(x, -1), 0.0)
    k = 1
    while k < half:
        bwd = bwd + jnp.where(pos >= k, _shift_rows(bwd, -k), 0.0)
        k *= 2
    return (fwd + bwd) * inv_cnt - x


OUTPROJ_ROW_BLOCKS = 4
OUTPROJ_EPI_ROWS = 64


def _outproj_kernel(hm_ref, u_ref, x_ref, mod_ref, nw_ref, pw_ref, ps_ref, wout_ref,
                    x1_ref, h2_ref, mix_ref, *, seg):
    tm = x_ref.shape[0]
    nblk = min(OUTPROJ_ROW_BLOCKS, tm // seg)
    rb = tm // nblk
    nslice = 2 * len(POOL_WINDOWS)
    ncol = D_MODEL // nslice
    mod = mod_ref[0]
    gain1 = nw_ref[1:2, :] * mod[2:3, :]
    gain2 = nw_ref[2:3, :] * (1.0 + mod[4:5, :])
    shift2 = mod[3:4, :]
    pos = lax.broadcasted_iota(jnp.int32, (rb, 1), 0) % seg
    inv_cnt = [1.0 / (jnp.minimum(pos + w // 2, seg) - jnp.maximum(pos - w // 2, 0)).astype(F32)
               for w in POOL_WINDOWS]

    def epilogue_slices(r):
        def one(lo):
            rows = slice(lo, lo + OUTPROJ_EPI_ROWS)
            mix = mix_ref[rows, :]
            ms = jnp.mean(mix * mix, axis=-1, keepdims=True)
            x1 = x_ref[rows, :] + (mix * lax.rsqrt(ms + EPS)) * gain1
            x1_ref[rows, :] = x1
            ms2 = jnp.mean(x1 * x1, axis=-1, keepdims=True)
            h2_ref[rows, :] = ((x1 * lax.rsqrt(ms2 + EPS)) * gain2 + shift2).astype(BF16)
        return [functools.partial(one, r * rb + k * OUTPROJ_EPI_ROWS)
                for k in range(rb // OUTPROJ_EPI_ROWS)]

    def run_block(r, pending):
        rows = slice(r * rb, (r + 1) * rb)
        srows = hm_ref.shape[2]
        sidx, off = (r * rb) // srows, (r * rb) % srows
        hm = jnp.concatenate([hm_ref[sidx, hbk, off:off + rb, :] for hbk in range(hm_ref.shape[1])],
                             axis=1)
        parts = []
        for gi, win in enumerate(POOL_WINDOWS):
            halves = []
            for hh in range(2):
                idx = 2 * gi + hh
                ns = slice(idx * ncol, (idx + 1) * ncol)
                mix_ref[rows, ns] = jnp.dot(hm, wout_ref[0:MLSTM_WIDTH, ns], preferred_element_type=F32)
                lo = gi * POOL_GROUP_DIM + hh * LANES
                halves.append(_pool_minus_self(u_ref[rows, lo:lo + LANES], win // 2, pos, seg,
                                               inv_cnt[gi]).astype(BF16))
                if pending and idx % 4 == 3:
                    pending.pop(0)()
            cs = slice(gi * POOL_GROUP_DIM, (gi + 1) * POOL_GROUP_DIM)
            y = jnp.dot(jnp.concatenate(halves, axis=1), pw_ref[gi], preferred_element_type=F32)
            parts.append((y * ps_ref[0:1, cs]).astype(BF16))
        hp = jnp.concatenate(parts, axis=1)
        for idx in range(nslice):
            ns = slice(idx * ncol, (idx + 1) * ncol)
            mix_ref[rows, ns] += jnp.dot(hp, wout_ref[MLSTM_WIDTH:, ns], preferred_element_type=F32)
            if pending and idx % 4 == 3:
                pending.pop(0)()
        while pending:
            pending.pop(0)()

    pending = []
    for r in range(nblk):
        run_block(r, pending)
        pending = epilogue_slices(r)
    for fn in pending:
        fn()


def _outproj_call(hm, u, x, mod, mod_map, norm_w, pool_w, pool_scale, w_out, *, seg, tm):
    tokens = x.shape[0]
    _, nhb, seq_len, w = hm.shape
    tok = lambda i: (i, 0)
    if tm >= seq_len:
        hm_spec = pl.BlockSpec((tm // seq_len, nhb, seq_len, w), lambda i: (i, 0, 0, 0))
    else:
        per = seq_len // tm
        hm_spec = pl.BlockSpec((1, nhb, tm, w), lambda i: (i // per, 0, i % per, 0))
    return pl.pallas_call(
        functools.partial(_outproj_kernel, seg=seg),
        grid=(tokens // tm,),
        in_specs=[hm_spec,
                  pl.BlockSpec((tm, POOL_WIDTH), tok),
                  pl.BlockSpec((tm, D_MODEL), tok),
                  pl.BlockSpec((1, N_MOD, D_MODEL), mod_map),
                  _resident((4, D_MODEL)),
                  _resident(pool_w.shape),
                  _resident((1, POOL_WIDTH)),
                  _resident(w_out.shape)],
        out_specs=[pl.BlockSpec((tm, D_MODEL), tok),
                   pl.BlockSpec((tm, D_MODEL), tok)],
        out_shape=[jax.ShapeDtypeStruct((tokens, D_MODEL), F32),
                   jax.ShapeDtypeStruct((tokens, D_MODEL), BF16)],
        scratch_shapes=[pltpu.VMEM((tm, D_MODEL), F32)],
        compiler_params=_cparams(("arbitrary",)),
        name="out_proj",
    )(hm, u, x, mod, norm_w, pool_w, pool_scale.reshape(1, POOL_WIDTH), w_out)


MLP_TF = 1024
MLP_LAST_ROW_BLOCKS = 4
MLP_SLICE_COLS = 256


def _mlp_kernel(h_ref, w1_ref, w2_ref, x1_ref, mod_ref, nw_ref, out_ref, acc_ref):
    k = pl.program_id(1)
    last = pl.num_programs(1) - 1
    tm, tf = h_ref.shape[0], w1_ref.shape[2]

    def hidden(rows, cols=slice(None)):
        a = jnp.dot(h_ref[rows, :], w1_ref[0, :, cols], preferred_element_type=F32)
        return jnp.square(jnp.maximum(a, 0.0)).astype(BF16)

    @pl.when(k == 0)
    def _():
        acc_ref[...] = jnp.dot(hidden(slice(None)), w2_ref[...], preferred_element_type=F32)

    @pl.when((k > 0) & (k < last))
    def _():
        acc_ref[...] += jnp.dot(hidden(slice(None)), w2_ref[...], preferred_element_type=F32)

    @pl.when(k == last)
    def _():
        rb = tm // MLP_LAST_ROW_BLOCKS
        gain = nw_ref[3:4, :] * mod_ref[0, 5:6, :]

        def epilogue_slices(r):
            def one(lo):
                rows = slice(lo, lo + NORM_SLICE_ROWS)
                y = acc_ref[rows, :]
                ms = jnp.mean(y * y, axis=-1, keepdims=True)
                out_ref[rows, :] = x1_ref[rows, :] + (y * lax.rsqrt(ms + EPS)) * gain
            return [functools.partial(one, r * rb + j * NORM_SLICE_ROWS)
                    for j in range(rb // NORM_SLICE_ROWS)]

        pending = []
        n1, n2 = tf // MLP_SLICE_COLS, D_MODEL // MLP_SLICE_COLS
        for r in range(MLP_LAST_ROW_BLOCKS):
            rows = slice(r * rb, (r + 1) * rb)
            parts = []
            for idx in range(n1):
                parts.append(hidden(rows, slice(idx * MLP_SLICE_COLS, (idx + 1) * MLP_SLICE_COLS)))
                if pending and idx % 2 == 1:
                    pending.pop(0)()
            a = jnp.concatenate(parts, axis=1)
            for idx in range(n2):
                ns = slice(idx * MLP_SLICE_COLS, (idx + 1) * MLP_SLICE_COLS)
                acc_ref[rows, ns] += jnp.dot(a, w2_ref[:, ns], preferred_element_type=F32)
                if pending and idx % 2 == 1:
                    pending.pop(0)()
            while pending:
                pending.pop(0)()
            pending = epilogue_slices(r)
        for fn in pending:
            fn()


def _mlp_call(h2, x1, mod, mod_map, norm_w, w1, w2, *, tm):
    tokens = x1.shape[0]
    nk, _, tf = w1.shape
    tok = lambda i, k: (i, 0)
    return pl.pallas_call(
        _mlp_kernel,
        grid=(tokens // tm, nk),
        in_specs=[pl.BlockSpec((tm, D_MODEL), tok),
                  pl.BlockSpec((1, D_MODEL, tf), lambda i, k: (k, 0, 0)),
                  pl.BlockSpec((tf, D_MODEL), lambda i, k: (k, 0)),
                  pl.BlockSpec((tm, D_MODEL), tok),
                  pl.BlockSpec((1, N_MOD, D_MODEL), lambda i, k: mod_map(i)),
                  pl.BlockSpec((4, D_MODEL), lambda i, k: (0, 0))],
        out_specs=pl.BlockSpec((tm, D_MODEL), tok),
        out_shape=jax.ShapeDtypeStruct((tokens, D_MODEL), F32),
        scratch_shapes=[pltpu.VMEM((tm, D_MODEL), F32)],
        compiler_params=_cparams(("arbitrary", "arbitrary")),
        name="mlp",
    )(h2, w1, w2, x1, mod, norm_w)


def _gate_weights(w_g, gate_bias):
    hb = HEADS_PER_STEP
    nhb = N_HEADS // hb
    wg = w_g.reshape(D_MODEL, N_GATES, nhb, hb)
    wg = jnp.pad(wg, ((0, 0), (0, 0), (0, 0), (0, SUBLANES - hb)))
    wgt = wg.transpose(2, 1, 3, 0).reshape(nhb * GATE_ROWS, D_MODEL)
    gb = jnp.pad(gate_bias.reshape(N_GATES, nhb, hb), ((0, 0), (0, 0), (0, SUBLANES - hb)))
    gb = gb.transpose(1, 0, 2).reshape(nhb * GATE_ROWS)
    return wgt.astype(BF16), gb.astype(F32)


def _path(x, mod, mod_row, init, params, *, seg, tm, tm_mlp, emit_state):
    batch, seq_len, _ = x.shape
    wts, mlstm_nw, pool_w, pool_scale, w_out, norm_w, w1, w2 = params
    tokens = batch * seq_len
    xf = x.reshape(tokens, D_MODEL)
    q, kt, v, o, u, gates = _inproj_call(xf, mod, mod_row, norm_w, wts,
                                         batch=batch, seq_len=seq_len, tm=tm)
    res = _mlstm_call(q, kt, v, o, gates, mlstm_nw, init,
                      batch=batch, seq_len=seq_len, emit_state=emit_state)
    hm = res[0]

    def mod_map_for(t):
        if mod_row is None:
            per = seq_len // t
            return lambda i: (1 + i // per, 0, 0)
        return lambda i: (mod_row, 0, 0)

    x1, h2 = _outproj_call(hm, u, xf, mod, mod_map_for(tm), norm_w, pool_w, pool_scale, w_out,
                           seg=seg, tm=tm)
    y = _mlp_call(h2, x1, mod, mod_map_for(tm_mlp), norm_w, w1, w2, tm=tm_mlp)
    return y.reshape(batch, seq_len, D_MODEL), res[1:]


def kernel(x_prompt, x_sample, c, state_C, state_n, state_m, c_ctx, w_in, gate_bias, mlstm_norm_w,
           pool_w, pool_scale, w_out, ada_w, ada_b, norm_w, w1, w2):
    hb = HEADS_PER_STEP
    nhb = N_HEADS // hb
    m4 = 4 * MLSTM_WIDTH
    yp, ys = x_prompt, x_sample
    new_c, new_n, new_m = [], [], []
    for layer in range(w_in.shape[0]):
        wl = w_in[layer]
        wgt, gb = _gate_weights(wl[:, m4:m4 + N_GATES * N_HEADS], gate_bias[layer])
        wkg = jnp.concatenate([wl[:, MLSTM_WIDTH:2 * MLSTM_WIDTH].T.astype(BF16), wgt], axis=0)
        wts = (wl[:, 0:MLSTM_WIDTH].astype(BF16),
               wkg,
               wl[:, 2 * MLSTM_WIDTH:3 * MLSTM_WIDTH].astype(BF16),
               wl[:, 3 * MLSTM_WIDTH:m4].astype(BF16),
               wl[:, m4 + N_GATES * N_HEADS:].astype(BF16),
               gb)
        params = (wts, mlstm_norm_w[layer].reshape(1, MLSTM_WIDTH), pool_w[layer].astype(BF16),
                  pool_scale[layer], w_out[layer].astype(BF16), norm_w[layer],
                  w1[layer].reshape(D_MODEL, D_FF // MLP_TF, MLP_TF).transpose(1, 0, 2).astype(BF16),
                  w2[layer].astype(BF16))

        n_lat = c.shape[0]
        rows = 16
        cond = jnp.concatenate([c_ctx[None, :], c, jnp.zeros((rows - 1 - n_lat, D_MODEL), F32)], axis=0)
        mod = _mod_call(cond, ada_w[layer], ada_b[layer]).reshape(rows, N_MOD, D_MODEL)

        yp, st = _path(yp, mod, 0, None, params, seg=yp.shape[1], tm=512, tm_mlp=512, emit_state=True)
        cc, nn, mm = st
        new_c.append(cc)
        new_n.append(nn[..., 0])
        new_m.append(mm.reshape(mm.shape[0], nhb, 2, SUBLANES, LANES)[:, :, :, :hb, 0]
                     .transpose(0, 2, 1, 3).reshape(mm.shape[0], 2, N_HEADS))

        sc = state_C[:, layer].astype(F32)
        sn = jnp.broadcast_to(state_n[:, layer].astype(F32)[..., None],
                              (n_lat, 2, N_HEADS, HEAD_DIM, LANES))
        sm = state_m[:, layer].astype(F32).reshape(n_lat, 2, nhb, hb).transpose(0, 2, 1, 3)
        sm = jnp.pad(sm, ((0, 0), (0, 0), (0, 0), (0, SUBLANES - hb)))
        sm = jnp.broadcast_to(sm[..., None], (n_lat, nhb, 2, SUBLANES, LANES))
        ys, _ = _path(ys, mod, None, (sc, sn, sm), params, seg=GRID_W, tm=512, tm_mlp=512,
                      emit_state=False)
    return (yp, ys, jnp.stack(new_c, axis=1), jnp.stack(new_n, axis=1), jnp.stack(new_m, axis=1))
```

```python
import functools

import jax
import jax.numpy as jnp
from jax import lax
from jax.experimental import pallas as pl
from jax.experimental.pallas import tpu as pltpu

F32 = jnp.float32
BF16 = jnp.bfloat16

D_MODEL = 2048
MLSTM_WIDTH = 1024
N_HEADS = 8
HEAD_DIM = 128
POOL_WIDTH = 1024
POOL_WINDOWS = (2, 4, 8, 16)
POOL_GROUP_DIM = 256
N_GATES = 4
D_FF = 8192
CHUNK = 128
N_MOD = 6
GRID_W = 64
EPS = 1e-6

LANES = 128
SUBLANES = 8
VMEM_LIMIT = 56 * 1024 * 1024

HEADS_PER_STEP = 4
GATE_ROWS = N_GATES * SUBLANES
GATE_STATS = 5
STAT_ROWS = 2 * GATE_STATS * SUBLANES


def _cparams(sem):
    return pltpu.CompilerParams(dimension_semantics=sem, vmem_limit_bytes=VMEM_LIMIT)


def _resident(shape):
    nd = len(shape)
    return pl.BlockSpec(shape, lambda *_: (0,) * nd, pipeline_mode=pl.Buffered(1))


def _rms(x, w):
    ms = jnp.mean(x * x, axis=-1, keepdims=True)
    return x * lax.rsqrt(ms + EPS) * w


def _mod_kernel(cond_ref, w_ref, b_ref, out_ref):
    c = cond_ref[...]
    s = c * jax.nn.sigmoid(c)
    out_ref[...] = jnp.dot(s.astype(BF16), w_ref[...].astype(BF16),
                           preferred_element_type=F32) + b_ref[...]


def _mod_call(cond, ada_w, ada_b):
    rows = cond.shape[0]
    n = ada_w.shape[1]
    tn = 1536
    return pl.pallas_call(
        _mod_kernel,
        grid=(n // tn,),
        in_specs=[pl.BlockSpec((rows, D_MODEL), lambda j: (0, 0)),
                  pl.BlockSpec((D_MODEL, tn), lambda j: (0, j)),
                  pl.BlockSpec((1, tn), lambda j: (0, j))],
        out_specs=pl.BlockSpec((rows, tn), lambda j: (0, j)),
        out_shape=jax.ShapeDtypeStruct((rows, n), F32),
        compiler_params=_cparams(("arbitrary",)),
        name="mod",
    )(cond, ada_w, ada_b.reshape(1, n))


NORM_SLICE_ROWS = 64


def _lane_scan_pair(fwd, bwd, op, ident):
    lane = lax.broadcasted_iota(jnp.int32, fwd.shape, 1)
    k = 1
    while k < LANES:
        f_shift = pltpu.roll(fwd, k, axis=1)
        b_shift = pltpu.roll(bwd, LANES - k, axis=1)
        fwd = op(fwd, jnp.where(lane >= k, f_shift, ident))
        bwd = op(bwd, jnp.where(lane < LANES - k, b_shift, ident))
        k *= 2
    return fwd, bwd


def _inproj_kernel(x_ref, mod_ref, nw_ref, wq_ref, wkg_ref, wv_ref, wo_ref, wu_ref, gb_ref,
                   q_ref, kt_ref, v_ref, o_ref, u_ref, g_ref, *, seqs, chunks):
    x = x_ref[...]
    mod = mod_ref[0]
    h = (_rms(x, nw_ref[0:1, :]) * (1.0 + mod[1:2, :]) + mod[0:1, :]).astype(BF16)
    ktg = lax.dot_general(wkg_ref[...], h, (((1,), (1,)), ((), ())), preferred_element_type=F32)
    kt = ktg[0:MLSTM_WIDTH].astype(BF16)
    gt = ktg[MLSTM_WIDTH:] + gb_ref[...]
    nhb = gt.shape[0] // GATE_ROWS
    for s in range(seqs):
        for c in range(chunks):
            lo = (s * chunks + c) * CHUNK
            kt_ref[s, c] = kt[:, lo:lo + CHUNK]
    q = jnp.dot(h, wq_ref[...], preferred_element_type=F32) * (HEAD_DIM ** -0.5)
    q_ref[...] = q.astype(BF16)
    v_ref[...] = jnp.dot(h, wv_ref[...], preferred_element_type=F32).astype(BF16)
    o_ref[...] = jnp.dot(h, wo_ref[...], preferred_element_type=F32)
    u_ref[...] = jnp.dot(h, wu_ref[...], preferred_element_type=F32)

    groups = [(s, c, hbk) for s in range(seqs) for c in range(chunks) for hbk in range(nhb)]

    def gate(g):
        return jnp.concatenate(
            [gt[hbk * GATE_ROWS + g * SUBLANES:hbk * GATE_ROWS + (g + 1) * SUBLANES,
                (s * chunks + c) * CHUNK:(s * chunks + c + 1) * CHUNK] for s, c, hbk in groups], axis=0)

    li = (gate(0), gate(2))
    lf = (jax.nn.log_sigmoid(gate(1)), jax.nn.log_sigmoid(gate(3)))
    b = _lane_scan_pair(lf[0], lf[1], jnp.add, 0.0)
    a = (li[0] - b[0], li[1] - b[1])
    pm = _lane_scan_pair(a[0], a[1], jnp.maximum, jnp.float32(-jnp.inf))
    stats = []
    for d in range(2):
        amax = jnp.broadcast_to(jnp.max(a[d], axis=1, keepdims=True), a[d].shape)
        bsum = jnp.broadcast_to(jnp.sum(lf[d], axis=1, keepdims=True), a[d].shape)
        stats += [a[d], b[d], pm[d], amax, bsum]
    for gi, (s, c, hbk) in enumerate(groups):
        rs = slice(gi * SUBLANES, (gi + 1) * SUBLANES)
        g_ref[s, c, hbk * STAT_ROWS:(hbk + 1) * STAT_ROWS, :] = jnp.concatenate(
            [st[rs] for st in stats], axis=0)


def _inproj_call(x, mod, mod_row, norm_w, wts, *, batch, seq_len, tm):
    wq, wkg, wv, wo, wu, gb = wts
    tokens = batch * seq_len
    nc = seq_len // CHUNK
    grows = wkg.shape[0] - MLSTM_WIDTH
    srows = grows // GATE_ROWS * STAT_ROWS
    if tm >= seq_len:
        seqs, chunks = tm // seq_len, nc
        kt_map = lambda i: (i, 0, 0, 0)
    else:
        seqs, chunks = 1, tm // CHUNK
        per = seq_len // tm
        kt_map = lambda i: (i // per, i % per, 0, 0)
    tiles_per_seq = max(seq_len // tm, 1)
    if mod_row is None:
        mod_map = lambda i: (1 + i // tiles_per_seq, 0, 0)
    else:
        mod_map = lambda i: (mod_row, 0, 0)
    tok = lambda i: (i, 0)
    gbb = jnp.broadcast_to(gb[:, None], (grows, tm))
    kern = functools.partial(_inproj_kernel, seqs=seqs, chunks=chunks)
    return pl.pallas_call(
        kern,
        grid=(tokens // tm,),
        in_specs=[pl.BlockSpec((tm, D_MODEL), tok),
                  pl.BlockSpec((1, N_MOD, D_MODEL), mod_map),
                  _resident((4, D_MODEL)),
                  _resident(wq.shape), _resident(wkg.shape), _resident(wv.shape),
                  _resident(wo.shape), _resident(wu.shape),
                  _resident((grows, tm))],
        out_specs=[pl.BlockSpec((tm, MLSTM_WIDTH), tok),
                   pl.BlockSpec((seqs, chunks, MLSTM_WIDTH, CHUNK), kt_map),
                   pl.BlockSpec((tm, MLSTM_WIDTH), tok),
                   pl.BlockSpec((tm, MLSTM_WIDTH), tok),
                   pl.BlockSpec((tm, POOL_WIDTH), tok),
                   pl.BlockSpec((seqs, chunks, srows, CHUNK), kt_map)],
        out_shape=[jax.ShapeDtypeStruct((tokens, MLSTM_WIDTH), BF16),
                   jax.ShapeDtypeStruct((batch, nc, MLSTM_WIDTH, CHUNK), BF16),
                   jax.ShapeDtypeStruct((tokens, MLSTM_WIDTH), BF16),
                   jax.ShapeDtypeStruct((tokens, MLSTM_WIDTH), F32),
                   jax.ShapeDtypeStruct((tokens, POOL_WIDTH), F32),
                   jax.ShapeDtypeStruct((batch, nc, srows, CHUNK), F32)],
        compiler_params=_cparams(("arbitrary",)),
        name="in_proj",
    )(x, mod, norm_w, wq, wkg, wv, wo, wu, gbb)


def _chunk_start(c):
    return c * CHUNK if isinstance(c, int) else pl.multiple_of(c * CHUNK, CHUNK)


def _mlstm_kernel(*refs, nc, hb, has_init, emit_state):
    it = iter(refs)
    q_ref, kt_ref, v_ref, o_ref, g_ref, nw_ref = (next(it) for _ in range(6))
    if has_init:
        c0_ref, n0_ref, m0_ref = (next(it) for _ in range(3))
    out_ref = next(it)
    if emit_state:
        cout_ref, nout_ref, mout_ref = (next(it) for _ in range(3))
    hf_sc, hb_sc, cn_sc, m_sc, xt_sc, rows_sc = (next(it) for _ in range(6))

    if has_init:
        for d in range(2):
            for j in range(hb):
                cn_sc[d * hb + j, :, 0:HEAD_DIM] = c0_ref[0, d, j]
                n_row = n0_ref[0, 0, d][j:j + 1, :]
                cn_sc[d * hb + j, :, HEAD_DIM:] = jnp.broadcast_to(n_row, (HEAD_DIM, LANES)).T
    else:
        cn_sc[...] = jnp.zeros_like(cn_sc)

    neg_inf = jnp.float32(-jnp.inf)

    cols = []
    for d in range(2):
        def stat(k):
            lo = (d * GATE_STATS + k) * SUBLANES
            return g_ref[0, :, lo:lo + SUBLANES, :].reshape(nc * SUBLANES, LANES)
        a, b, pm, amax, bsum = (stat(k) for k in range(GATE_STATS))
        m = m0_ref[0, 0, d] if has_init else jnp.zeros((SUBLANES, LANES), F32)
        m_prev = [None] * nc
        for ci in (range(nc) if d == 0 else range(nc - 1, -1, -1)):
            rs = slice(ci * SUBLANES, (ci + 1) * SUBLANES)
            m_prev[ci] = m
            m = bsum[rs] + jnp.maximum(m, amax[rs])
        m_sc[d] = m
        mp = jnp.concatenate(m_prev, axis=0)
        mu = jnp.maximum(mp, pm)
        mu_last = jnp.maximum(mp, amax)
        rows_sc[0, d] = a.reshape(nc, SUBLANES, LANES)
        rows_sc[1, d] = jnp.exp(a - mu_last).reshape(nc, SUBLANES, LANES)
        rows_sc[2, d] = jnp.exp(mp - mu_last).reshape(nc, SUBLANES, LANES)
        cols.append((mu, jnp.exp(mp - mu), jnp.exp(-(b + mu))))
    zpad = jnp.zeros((LANES - 6 * SUBLANES, LANES), F32)
    for c in range(nc):
        pieces = []
        for d, ci in ((0, c), (1, nc - 1 - c)):
            pieces += [x[ci * SUBLANES:(ci + 1) * SUBLANES] for x in cols[d]]
        xt_sc[c] = jnp.concatenate(pieces + [zpad], axis=0).T

    row = lax.broadcasted_iota(jnp.int32, (CHUNK, CHUNK), 0)
    col = lax.broadcasted_iota(jnp.int32, (CHUNK, CHUNK), 1)
    causal = (col <= row, col >= row)
    ones_blk = jnp.ones((CHUNK, HEAD_DIM), BF16)

    def step(c, carry):
        chunk_of = (c, nc - 1 - c)
        xt = xt_sc[c]
        chains = []
        for d in range(2):
            ch = chunk_of[d]
            t0 = _chunk_start(ch)
            a_rows = rows_sc[0, d, ch]
            for j in range(hb):
                hs = slice(j * HEAD_DIM, (j + 1) * HEAD_DIM)
                qc = q_ref[0, pl.ds(t0, CHUNK), hs]
                base = 3 * d * SUBLANES + j
                mu_c = xt[:, base:base + 1]
                g_c = xt[:, base + SUBLANES:base + SUBLANES + 1]
                s = jnp.dot(qc, kt_ref[0, ch, hs, :], preferred_element_type=F32)
                p = s * jnp.exp(jnp.where(causal[d], a_rows[j:j + 1, :] - mu_c, neg_inf))
                gq = (qc.astype(F32) * g_c).astype(BF16)
                chains.append((d, j, ch, t0, jnp.concatenate([p.astype(BF16), gq], axis=1)))

        for d, j, ch, t0, lhs in chains:
            hs = slice(j * HEAD_DIM, (j + 1) * HEAD_DIM)
            base = 3 * d * SUBLANES + j
            en_c = xt[:, base + 2 * SUBLANES:base + 2 * SUBLANES + 1]
            v1 = jnp.concatenate([v_ref[0, pl.ds(t0, CHUNK), hs], ones_blk], axis=1)
            cn = cn_sc[d * hb + j]
            tot = jnp.dot(lhs, jnp.concatenate([v1, cn.astype(BF16)], axis=0),
                          preferred_element_type=F32)
            hval = tot[:, :HEAD_DIM] / jnp.maximum(jnp.abs(tot[:, HEAD_DIM:]), en_c)
            dst = hf_sc if d == 0 else hb_sc
            dst[pl.ds(t0, CHUNK), hs] = hval

            e_row = rows_sc[1, d, ch][j:j + 1, :]
            gs_row = rows_sc[2, d, ch][j:j + 1, :]
            wkt = (kt_ref[0, ch, hs, :].astype(F32) * e_row).astype(BF16)
            dcn = jnp.dot(wkt, v1, preferred_element_type=F32)
            gs_blk = jnp.broadcast_to(gs_row, (HEAD_DIM, LANES))
            cn_sc[d * hb + j] = jnp.concatenate([gs_blk, gs_blk], axis=1) * cn + dcn
        return carry

    if nc <= 2:
        for c in range(nc):
            step(c, 0)
    else:
        lax.fori_loop(0, nc, step, 0)

    def finalize(c, carry):
        t0 = _chunk_start(c)
        hsum = hf_sc[pl.ds(t0, CHUNK), :] + hb_sc[pl.ds(t0, CHUNK), :]
        og = jax.nn.sigmoid(o_ref[0, pl.ds(t0, CHUNK), :])
        for j in range(hb):
            hs = slice(j * HEAD_DIM, (j + 1) * HEAD_DIM)
            hn = _rms(hsum[:, hs], nw_ref[0:1, hs])
            out_ref[0, pl.ds(t0, CHUNK), hs] = (hn * og[:, hs]).astype(BF16)
        return carry

    if nc <= 2:
        for c in range(nc):
            finalize(c, 0)
    else:
        lax.fori_loop(0, nc, finalize, 0)

    if emit_state:
        pad_rows = jnp.zeros((SUBLANES - hb, LANES), F32)
        for d in range(2):
            n_rows = []
            for j in range(hb):
                cout_ref[0, d, j] = cn_sc[d * hb + j, :, 0:HEAD_DIM]
                n_rows.append(cn_sc[d * hb + j, :, HEAD_DIM:].T[0:1, :])
            nout_ref[0, 0, d] = jnp.concatenate(n_rows + ([pad_rows] if hb < SUBLANES else []), axis=0)
            mout_ref[0, 0, d] = m_sc[d]


def _mlstm_call(q, kt, v, o, stats, nw, init, *, batch, seq_len, emit_state):
    hb = HEADS_PER_STEP
    nhb = N_HEADS // hb
    nc = seq_len // CHUNK
    w = hb * HEAD_DIM
    seq_map = lambda b, k: (b, 0, k)
    in_specs = [pl.BlockSpec((1, seq_len, w), seq_map),
                pl.BlockSpec((1, nc, w, CHUNK), lambda b, k: (b, 0, k, 0)),
                pl.BlockSpec((1, seq_len, w), seq_map),
                pl.BlockSpec((1, seq_len, w), seq_map),
                pl.BlockSpec((1, nc, STAT_ROWS, CHUNK), lambda b, k: (b, 0, k, 0)),
                pl.BlockSpec((1, w), lambda b, k: (0, k))]
    args = [q, kt, v, o, stats, nw]
    state_map = lambda b, k: (b, 0, k, 0, 0)
    vec_map = lambda b, k: (b, k, 0, 0, 0)
    vec_spec = pl.BlockSpec((1, 1, 2, SUBLANES, LANES), vec_map)
    if init is not None:
        c0, n0, m0 = init
        in_specs += [pl.BlockSpec((1, 2, hb, HEAD_DIM, HEAD_DIM), state_map), vec_spec, vec_spec]
        args += [c0, n0, m0]
    out_specs = [pl.BlockSpec((1, seq_len, w), seq_map)]
    out_shape = [jax.ShapeDtypeStruct((batch, seq_len, MLSTM_WIDTH), BF16)]
    if emit_state:
        out_specs += [pl.BlockSpec((1, 2, hb, HEAD_DIM, HEAD_DIM), state_map), vec_spec, vec_spec]
        out_shape += [jax.ShapeDtypeStruct((batch, 2, N_HEADS, HEAD_DIM, HEAD_DIM), F32),
                      jax.ShapeDtypeStruct((batch, nhb, 2, SUBLANES, LANES), F32),
                      jax.ShapeDtypeStruct((batch, nhb, 2, SUBLANES, LANES), F32)]
    kern = functools.partial(_mlstm_kernel, nc=nc, hb=hb, has_init=init is not None,
                             emit_state=emit_state)
    return pl.pallas_call(
        kern,
        grid=(batch, nhb),
        in_specs=in_specs,
        out_specs=out_specs,
        out_shape=out_shape,
        scratch_shapes=[pltpu.VMEM((seq_len, w), F32),
                        pltpu.VMEM((seq_len, w), F32),
                        pltpu.VMEM((2 * hb, HEAD_DIM, 2 * HEAD_DIM), F32),
                        pltpu.VMEM((2, SUBLANES, LANES), F32),
                        pltpu.VMEM((nc, LANES, LANES), F32),
                        pltpu.VMEM((3, 2, nc, SUBLANES, LANES), F32)],
        compiler_params=_cparams(("arbitrary", "arbitrary")),
        name="mlstm",
    )(*args)


def _shift_rows(x, dlt):
    return pltpu.roll(x, (-dlt) % x.shape[0], axis=0)


def _pool_minus_self(x, half, pos, seg, inv_cnt):
    fwd = x
    k = 1
    while k < half:
        fwd = fwd + jnp.where(pos + k < seg, _shift_rows(fwd, k), 0.0)
        k *= 2
    bwd = jnp.where(pos >= 1, _shift_rows(x, -1), 0.0)
    k = 1
    while k < half:
        bwd = bwd + jnp.where(pos >= k, _shift_rows(bwd, -k), 0.0)
        k *= 2
    return (fwd + bwd) * inv_cnt - x


OUTPROJ_ROW_BLOCKS = 4
OUTPROJ_EPI_ROWS = 64


def _outproj_kernel(hm_ref, u_ref, x_ref, mod_ref, nw_ref, pw_ref, ps_ref, wout_ref,
                    x1_ref, h2_ref, mix_ref, *, seg):
    tm = x_ref.shape[0]
    nblk = min(OUTPROJ_ROW_BLOCKS, tm // seg)
    rb = tm // nblk
    nslice = 2 * len(POOL_WINDOWS)
    ncol = D_MODEL // nslice
    mod = mod_ref[0]
    gain1 = nw_ref[1:2, :] * mod[2:3, :]
    gain2 = nw_ref[2:3, :] * (1.0 + mod[4:5, :])
    shift2 = mod[3:4, :]
    pos = lax.broadcasted_iota(jnp.int32, (rb, 1), 0) % seg
    inv_cnt = [1.0 / (jnp.minimum(pos + w // 2, seg) - jnp.maximum(pos - w // 2, 0)).astype(F32)
               for w in POOL_WINDOWS]

    def epilogue_slices(r):
        def one(lo):
            rows = slice(lo, lo + OUTPROJ_EPI_ROWS)
            mix = mix_ref[rows, :]
            ms = jnp.mean(mix * mix, axis=-1, keepdims=True)
            x1 = x_ref[rows, :] + (mix * lax.rsqrt(ms + EPS)) * gain1
            x1_ref[rows, :] = x1
            ms2 = jnp.mean(x1 * x1, axis=-1, keepdims=True)
            h2_ref[rows, :] = ((x1 * lax.rsqrt(ms2 + EPS)) * gain2 + shift2).astype(BF16)
        return [functools.partial(one, r * rb + k * OUTPROJ_EPI_ROWS)
                for k in range(rb // OUTPROJ_EPI_ROWS)]

    def run_block(r, pending):
        rows = slice(r * rb, (r + 1) * rb)
        hm = hm_ref[rows, :]
        parts = []
        for gi, win in enumerate(POOL_WINDOWS):
            halves = []
            for hh in range(2):
                idx = 2 * gi + hh
                ns = slice(idx * ncol, (idx + 1) * ncol)
                mix_ref[rows, ns] = jnp.dot(hm, wout_ref[0:MLSTM_WIDTH, ns], preferred_element_type=F32)
                lo = gi * POOL_GROUP_DIM + hh * LANES
                halves.append(_pool_minus_self(u_ref[rows, lo:lo + LANES], win // 2, pos, seg,
                                               inv_cnt[gi]).astype(BF16))
                if pending and idx % 4 == 3:
                    pending.pop(0)()
            cs = slice(gi * POOL_GROUP_DIM, (gi + 1) * POOL_GROUP_DIM)
            y = jnp.dot(jnp.concatenate(halves, axis=1), pw_ref[gi], preferred_element_type=F32)
            parts.append((y * ps_ref[0:1, cs]).astype(BF16))
        hp = jnp.concatenate(parts, axis=1)
        for idx in range(nslice):
            ns = slice(idx * ncol, (idx + 1) * ncol)
            mix_ref[rows, ns] += jnp.dot(hp, wout_ref[MLSTM_WIDTH:, ns], preferred_element_type=F32)
            if pending and idx % 4 == 3:
                pending.pop(0)()
        while pending:
            pending.pop(0)()

    pending = []
    for r in range(nblk):
        run_block(r, pending)
        pending = epilogue_slices(r)
    for fn in pending:
        fn()


def _outproj_call(hm, u, x, mod, mod_map, norm_w, pool_w, pool_scale, w_out, *, seg, tm):
    tokens = x.shape[0]
    tok = lambda i: (i, 0)
    return pl.pallas_call(
        functools.partial(_outproj_kernel, seg=seg),
        grid=(tokens // tm,),
        in_specs=[pl.BlockSpec((tm, MLSTM_WIDTH), tok),
                  pl.BlockSpec((tm, POOL_WIDTH), tok),
                  pl.BlockSpec((tm, D_MODEL), tok),
                  pl.BlockSpec((1, N_MOD, D_MODEL), mod_map),
                  _resident((4, D_MODEL)),
                  _resident(pool_w.shape),
                  _resident((1, POOL_WIDTH)),
                  _resident(w_out.shape)],
        out_specs=[pl.BlockSpec((tm, D_MODEL), tok),
                   pl.BlockSpec((tm, D_MODEL), tok)],
        out_shape=[jax.ShapeDtypeStruct((tokens, D_MODEL), F32),
                   jax.ShapeDtypeStruct((tokens, D_MODEL), BF16)],
        scratch_shapes=[pltpu.VMEM((tm, D_MODEL), F32)],
        compiler_params=_cparams(("arbitrary",)),
        name="out_proj",
    )(hm, u, x, mod, norm_w, pool_w, pool_scale.reshape(1, POOL_WIDTH), w_out)


MLP_LAST_ROW_BLOCKS = 4
MLP_SLICE_COLS = 256


def _mlp_kernel(h_ref, w1_ref, w2_ref, x1_ref, mod_ref, nw_ref, out_ref, acc_ref):
    k = pl.program_id(1)
    last = pl.num_programs(1) - 1
    tm, tf = h_ref.shape[0], w1_ref.shape[1]

    def hidden(rows, cols=slice(None)):
        a = jnp.dot(h_ref[rows, :], w1_ref[:, cols], preferred_element_type=F32)
        return jnp.square(jnp.maximum(a, 0.0)).astype(BF16)

    @pl.when(k == 0)
    def _():
        acc_ref[...] = jnp.dot(hidden(slice(None)), w2_ref[...], preferred_element_type=F32)

    @pl.when((k > 0) & (k < last))
    def _():
        acc_ref[...] += jnp.dot(hidden(slice(None)), w2_ref[...], preferred_element_type=F32)

    @pl.when(k == last)
    def _():
        rb = tm // MLP_LAST_ROW_BLOCKS
        gain = nw_ref[3:4, :] * mod_ref[0, 5:6, :]

        def epilogue_slices(r):
            def one(lo):
                rows = slice(lo, lo + NORM_SLICE_ROWS)
                y = acc_ref[rows, :]
                ms = jnp.mean(y * y, axis=-1, keepdims=True)
                out_ref[rows, :] = x1_ref[rows, :] + (y * lax.rsqrt(ms + EPS)) * gain
            return [functools.partial(one, r * rb + j * NORM_SLICE_ROWS)
                    for j in range(rb // NORM_SLICE_ROWS)]

        pending = []
        n1, n2 = tf // MLP_SLICE_COLS, D_MODEL // MLP_SLICE_COLS
        for r in range(MLP_LAST_ROW_BLOCKS):
            rows = slice(r * rb, (r + 1) * rb)
            parts = []
            for idx in range(n1):
                parts.append(hidden(rows, slice(idx * MLP_SLICE_COLS, (idx + 1) * MLP_SLICE_COLS)))
                if pending and idx % 2 == 1:
                    pending.pop(0)()
            a = jnp.concatenate(parts, axis=1)
            for idx in range(n2):
                ns = slice(idx * MLP_SLICE_COLS, (idx + 1) * MLP_SLICE_COLS)
                acc_ref[rows, ns] += jnp.dot(a, w2_ref[:, ns], preferred_element_type=F32)
                if pending and idx % 2 == 1:
                    pending.pop(0)()
            while pending:
                pending.pop(0)()
            pending = epilogue_slices(r)
        for fn in pending:
            fn()


def _mlp_call(h2, x1, mod, mod_map, norm_w, w1, w2, *, tm, tf):
    tokens = x1.shape[0]
    tok = lambda i, k: (i, 0)
    return pl.pallas_call(
        _mlp_kernel,
        grid=(tokens // tm, D_FF // tf),
        in_specs=[pl.BlockSpec((tm, D_MODEL), tok),
                  pl.BlockSpec((D_MODEL, tf), lambda i, k: (0, k)),
                  pl.BlockSpec((tf, D_MODEL), lambda i, k: (k, 0)),
                  pl.BlockSpec((tm, D_MODEL), tok),
                  pl.BlockSpec((1, N_MOD, D_MODEL), lambda i, k: mod_map(i)),
                  pl.BlockSpec((4, D_MODEL), lambda i, k: (0, 0))],
        out_specs=pl.BlockSpec((tm, D_MODEL), tok),
        out_shape=jax.ShapeDtypeStruct((tokens, D_MODEL), F32),
        scratch_shapes=[pltpu.VMEM((tm, D_MODEL), F32)],
        compiler_params=_cparams(("arbitrary", "arbitrary")),
        name="mlp",
    )(h2, w1, w2, x1, mod, norm_w)


def _gate_weights(w_g, gate_bias):
    hb = HEADS_PER_STEP
    nhb = N_HEADS // hb
    wg = w_g.reshape(D_MODEL, N_GATES, nhb, hb)
    wg = jnp.pad(wg, ((0, 0), (0, 0), (0, 0), (0, SUBLANES - hb)))
    wgt = wg.transpose(2, 1, 3, 0).reshape(nhb * GATE_ROWS, D_MODEL)
    gb = jnp.pad(gate_bias.reshape(N_GATES, nhb, hb), ((0, 0), (0, 0), (0, SUBLANES - hb)))
    gb = gb.transpose(1, 0, 2).reshape(nhb * GATE_ROWS)
    return wgt.astype(BF16), gb.astype(F32)


def _pack_head_rows(x):
    hb = HEADS_PER_STEP
    nhb = N_HEADS // hb
    x = x.reshape(x.shape[:2] + (nhb, hb) + x.shape[3:])
    x = jnp.moveaxis(x, 2, 1)
    pad = [(0, 0)] * x.ndim
    pad[3] = (0, SUBLANES - hb)
    return jnp.pad(x, pad)


def _unpack_head_rows(x):
    x = jnp.moveaxis(x[:, :, :, :HEADS_PER_STEP], 1, 2)
    return x.reshape(x.shape[:2] + (N_HEADS,) + x.shape[4:])


def _path(x, mod, mod_row, init, params, *, seg, tm, tm_mlp, emit_state):
    batch, seq_len, _ = x.shape
    wts, mlstm_nw, pool_w, pool_scale, w_out, norm_w, w1, w2 = params
    tokens = batch * seq_len
    xf = x.reshape(tokens, D_MODEL)
    q, kt, v, o, u, stats = _inproj_call(xf, mod, mod_row, norm_w, wts,
                                         batch=batch, seq_len=seq_len, tm=tm)
    shp = (batch, seq_len, MLSTM_WIDTH)
    res = _mlstm_call(q.reshape(shp), kt, v.reshape(shp), o.reshape(shp), stats, mlstm_nw, init,
                      batch=batch, seq_len=seq_len, emit_state=emit_state)
    hm = res[0].reshape(tokens, MLSTM_WIDTH)

    def mod_map_for(t):
        if mod_row is None:
            per = seq_len // t
            return lambda i: (1 + i // per, 0, 0)
        return lambda i: (mod_row, 0, 0)

    x1, h2 = _outproj_call(hm, u, xf, mod, mod_map_for(tm), norm_w, pool_w, pool_scale, w_out,
                           seg=seg, tm=tm)
    y = _mlp_call(h2, x1, mod, mod_map_for(tm_mlp), norm_w, w1, w2, tm=tm_mlp, tf=1024)
    return y.reshape(batch, seq_len, D_MODEL), res[1:]


def kernel(x_prompt, x_sample, c, state_C, state_n, state_m, c_ctx, w_in, gate_bias, mlstm_norm_w,
           pool_w, pool_scale, w_out, ada_w, ada_b, norm_w, w1, w2):
    m4 = 4 * MLSTM_WIDTH
    yp, ys = x_prompt, x_sample
    new_c, new_n, new_m = [], [], []
    for layer in range(w_in.shape[0]):
        wl = w_in[layer]
        wgt, gb = _gate_weights(wl[:, m4:m4 + N_GATES * N_HEADS], gate_bias[layer])
        wkg = jnp.concatenate([wl[:, MLSTM_WIDTH:2 * MLSTM_WIDTH].T.astype(BF16), wgt], axis=0)
        wts = (wl[:, 0:MLSTM_WIDTH].astype(BF16),
               wkg,
               wl[:, 2 * MLSTM_WIDTH:3 * MLSTM_WIDTH].astype(BF16),
               wl[:, 3 * MLSTM_WIDTH:m4].astype(BF16),
               wl[:, m4 + N_GATES * N_HEADS:].astype(BF16),
               gb)
        params = (wts, mlstm_norm_w[layer].reshape(1, MLSTM_WIDTH), pool_w[layer].astype(BF16),
                  pool_scale[layer], w_out[layer].astype(BF16), norm_w[layer],
                  w1[layer].astype(BF16), w2[layer].astype(BF16))

        n_lat = c.shape[0]
        rows = 16
        cond = jnp.concatenate([c_ctx[None, :], c, jnp.zeros((rows - 1 - n_lat, D_MODEL), F32)], axis=0)
        mod = _mod_call(cond, ada_w[layer], ada_b[layer]).reshape(rows, N_MOD, D_MODEL)

        yp, st = _path(yp, mod, 0, None, params, seg=yp.shape[1], tm=512, tm_mlp=512, emit_state=True)
        cc, nn, mm = st
        new_c.append(cc)
        new_n.append(_unpack_head_rows(nn))
        new_m.append(_unpack_head_rows(mm)[..., 0])

        sc = state_C[:, layer].astype(F32)
        sn = _pack_head_rows(state_n[:, layer].astype(F32))
        sm = _pack_head_rows(jnp.broadcast_to(state_m[:, layer].astype(F32)[..., None],
                                              (n_lat, 2, N_HEADS, LANES)))
        ys, _ = _path(ys, mod, None, (sc, sn, sm), params, seg=GRID_W, tm=512, tm_mlp=512,
                      emit_state=False)
    return (yp, ys, jnp.stack(new_c, axis=1), jnp.stack(new_n, axis=1), jnp.stack(new_m, axis=1))
```

```python
import functools

import jax
import jax.numpy as jnp
from jax import lax
from jax.experimental import pallas as pl
from jax.experimental.pallas import tpu as pltpu

F32 = jnp.float32
BF16 = jnp.bfloat16

D_MODEL = 2048
MLSTM_WIDTH = 1024
N_HEADS = 8
HEAD_DIM = 128
POOL_WIDTH = 1024
POOL_WINDOWS = (2, 4, 8, 16)
POOL_GROUP_DIM = 256
N_GATES = 4
D_FF = 8192
CHUNK = 128
N_MOD = 6
GRID_W = 64
EPS = 1e-6

LANES = 128
SUBLANES = 8
VMEM_LIMIT = 56 * 1024 * 1024

HEADS_PER_STEP = 4
GATE_ROWS = N_GATES * SUBLANES
GATE_STATS = 5
STAT_ROWS = 2 * GATE_STATS * SUBLANES
FINALIZE_UNROLL = 4


def _cparams(sem):
    return pltpu.CompilerParams(dimension_semantics=sem, vmem_limit_bytes=VMEM_LIMIT)


def _resident(shape):
    nd = len(shape)
    return pl.BlockSpec(shape, lambda *_: (0,) * nd, pipeline_mode=pl.Buffered(1))


def _rms(x, w):
    ms = jnp.mean(x * x, axis=-1, keepdims=True)
    return x * lax.rsqrt(ms + EPS) * w


def _mod_kernel(cond_ref, w_ref, b_ref, out_ref):
    c = cond_ref[...]
    s = c * jax.nn.sigmoid(c)
    out_ref[...] = jnp.dot(s.astype(BF16), w_ref[...].astype(BF16),
                           preferred_element_type=F32) + b_ref[...]


def _mod_call(cond, ada_w, ada_b):
    rows = cond.shape[0]
    n = ada_w.shape[1]
    tn = 1536
    return pl.pallas_call(
        _mod_kernel,
        grid=(n // tn,),
        in_specs=[pl.BlockSpec((rows, D_MODEL), lambda j: (0, 0)),
                  pl.BlockSpec((D_MODEL, tn), lambda j: (0, j)),
                  pl.BlockSpec((1, tn), lambda j: (0, j))],
        out_specs=pl.BlockSpec((rows, tn), lambda j: (0, j)),
        out_shape=jax.ShapeDtypeStruct((rows, n), F32),
        compiler_params=_cparams(("arbitrary",)),
        name="mod",
    )(cond, ada_w, ada_b.reshape(1, n))


NORM_SLICE_ROWS = 64


def _lane_scan_pair(fwd, bwd, op, ident):
    lane = lax.broadcasted_iota(jnp.int32, fwd.shape, 1)
    k = 1
    while k < LANES:
        f_shift = pltpu.roll(fwd, k, axis=1)
        b_shift = pltpu.roll(bwd, LANES - k, axis=1)
        fwd = op(fwd, jnp.where(lane >= k, f_shift, ident))
        bwd = op(bwd, jnp.where(lane < LANES - k, b_shift, ident))
        k *= 2
    return fwd, bwd


def _inproj_kernel(x_ref, mod_ref, nw_ref, wq_ref, wkg_ref, wv_ref, wo_ref, wu_ref, gb_ref, pw_ref, ps_ref,
                   q_ref, kt_ref, v_ref, og_ref, hp_ref, g_ref, *, seqs, chunks, seg):
    x = x_ref[...]
    mod = mod_ref[0]
    h = (_rms(x, nw_ref[0:1, :]) * (1.0 + mod[1:2, :]) + mod[0:1, :]).astype(BF16)
    ktg = lax.dot_general(wkg_ref[...], h, (((1,), (1,)), ((), ())), preferred_element_type=F32)
    kt = ktg[0:MLSTM_WIDTH].astype(BF16)
    gt = ktg[MLSTM_WIDTH:] + gb_ref[...]
    nhb = gt.shape[0] // GATE_ROWS
    for s in range(seqs):
        for c in range(chunks):
            lo = (s * chunks + c) * CHUNK
            kt_ref[s, c] = kt[:, lo:lo + CHUNK]
    u = jnp.dot(h, wu_ref[...], preferred_element_type=F32)
    tm = x.shape[0]
    pos = lax.broadcasted_iota(jnp.int32, (tm, 1), 0) % seg

    def pool_group(gi):
        half = POOL_WINDOWS[gi] // 2
        inv_cnt = 1.0 / (jnp.minimum(pos + half, seg) - jnp.maximum(pos - half, 0)).astype(F32)
        cs = slice(gi * POOL_GROUP_DIM, (gi + 1) * POOL_GROUP_DIM)
        p = jnp.concatenate(
            [_pool_minus_self(u[:, lo:lo + LANES], half, pos, seg, inv_cnt).astype(BF16)
             for lo in range(cs.start, cs.stop, LANES)], axis=1)
        y = jnp.dot(p, pw_ref[gi], preferred_element_type=F32)
        hp_ref[:, cs] = (y * ps_ref[0:1, cs]).astype(BF16)

    q = jnp.dot(h, wq_ref[...], preferred_element_type=F32) * (HEAD_DIM ** -0.5)
    q_ref[...] = q.astype(BF16)
    pool_group(0)
    pool_group(1)
    v_ref[...] = jnp.dot(h, wv_ref[...], preferred_element_type=F32).astype(BF16)
    pool_group(2)
    og_ref[...] = jax.nn.sigmoid(jnp.dot(h, wo_ref[...], preferred_element_type=F32))
    pool_group(3)

    groups = [(s, c, hbk) for s in range(seqs) for c in range(chunks) for hbk in range(nhb)]

    def gate(g):
        return jnp.concatenate(
            [gt[hbk * GATE_ROWS + g * SUBLANES:hbk * GATE_ROWS + (g + 1) * SUBLANES,
                (s * chunks + c) * CHUNK:(s * chunks + c + 1) * CHUNK] for s, c, hbk in groups], axis=0)

    li = (gate(0), gate(2))
    lf = (jax.nn.log_sigmoid(gate(1)), jax.nn.log_sigmoid(gate(3)))
    b = _lane_scan_pair(lf[0], lf[1], jnp.add, 0.0)
    a = (li[0] - b[0], li[1] - b[1])
    pm = _lane_scan_pair(a[0], a[1], jnp.maximum, jnp.float32(-jnp.inf))
    stats = []
    for d in range(2):
        amax = jnp.broadcast_to(jnp.max(a[d], axis=1, keepdims=True), a[d].shape)
        bsum = jnp.broadcast_to(jnp.sum(lf[d], axis=1, keepdims=True), a[d].shape)
        stats += [a[d], b[d], pm[d], amax, bsum]
    for gi, (s, c, hbk) in enumerate(groups):
        rs = slice(gi * SUBLANES, (gi + 1) * SUBLANES)
        g_ref[s, c, hbk * STAT_ROWS:(hbk + 1) * STAT_ROWS, :] = jnp.concatenate(
            [st[rs] for st in stats], axis=0)


def _inproj_call(x, mod, mod_row, norm_w, wts, pool_w, pool_scale, *, batch, seq_len, tm, seg):
    wq, wkg, wv, wo, wu, gb = wts
    tokens = batch * seq_len
    nc = seq_len // CHUNK
    grows = wkg.shape[0] - MLSTM_WIDTH
    srows = grows // GATE_ROWS * STAT_ROWS
    if tm >= seq_len:
        seqs, chunks = tm // seq_len, nc
        kt_map = lambda i: (i, 0, 0, 0)
    else:
        seqs, chunks = 1, tm // CHUNK
        per = seq_len // tm
        kt_map = lambda i: (i // per, i % per, 0, 0)
    tiles_per_seq = max(seq_len // tm, 1)
    if mod_row is None:
        mod_map = lambda i: (1 + i // tiles_per_seq, 0, 0)
    else:
        mod_map = lambda i: (mod_row, 0, 0)
    tok = lambda i: (i, 0)
    gbb = jnp.broadcast_to(gb[:, None], (grows, tm))
    kern = functools.partial(_inproj_kernel, seqs=seqs, chunks=chunks, seg=seg)
    return pl.pallas_call(
        kern,
        grid=(tokens // tm,),
        in_specs=[pl.BlockSpec((tm, D_MODEL), tok),
                  pl.BlockSpec((1, N_MOD, D_MODEL), mod_map),
                  _resident((4, D_MODEL)),
                  _resident(wq.shape), _resident(wkg.shape), _resident(wv.shape),
                  _resident(wo.shape), _resident(wu.shape),
                  _resident((grows, tm)),
                  _resident(pool_w.shape),
                  _resident((1, POOL_WIDTH))],
        out_specs=[pl.BlockSpec((tm, MLSTM_WIDTH), tok),
                   pl.BlockSpec((seqs, chunks, MLSTM_WIDTH, CHUNK), kt_map),
                   pl.BlockSpec((tm, MLSTM_WIDTH), tok),
                   pl.BlockSpec((tm, MLSTM_WIDTH), tok),
                   pl.BlockSpec((tm, POOL_WIDTH), tok),
                   pl.BlockSpec((seqs, chunks, srows, CHUNK), kt_map)],
        out_shape=[jax.ShapeDtypeStruct((tokens, MLSTM_WIDTH), BF16),
                   jax.ShapeDtypeStruct((batch, nc, MLSTM_WIDTH, CHUNK), BF16),
                   jax.ShapeDtypeStruct((tokens, MLSTM_WIDTH), BF16),
                   jax.ShapeDtypeStruct((tokens, MLSTM_WIDTH), F32),
                   jax.ShapeDtypeStruct((tokens, POOL_WIDTH), BF16),
                   jax.ShapeDtypeStruct((batch, nc, srows, CHUNK), F32)],
        compiler_params=_cparams(("arbitrary",)),
        name="in_proj",
    )(x, mod, norm_w, wq, wkg, wv, wo, wu, gbb, pool_w, pool_scale.reshape(1, POOL_WIDTH))


def _chunk_start(c):
    return c * CHUNK if isinstance(c, int) else pl.multiple_of(c * CHUNK, CHUNK)


def _mlstm_kernel(*refs, nc, hb, has_init, emit_state):
    it = iter(refs)
    q_ref, kt_ref, v_ref, o_ref, g_ref, nw_ref = (next(it) for _ in range(6))
    if has_init:
        c0_ref, n0_ref, m0_ref = (next(it) for _ in range(3))
    out_ref = next(it)
    if emit_state:
        cout_ref, nout_ref, mout_ref = (next(it) for _ in range(3))
    hf_sc, hb_sc, cn_sc, m_sc, xt_sc, rows_sc = (next(it) for _ in range(6))

    if has_init:
        for d in range(2):
            for j in range(hb):
                cn_sc[d * hb + j, :, 0:HEAD_DIM] = c0_ref[0, d, j]
                n_row = n0_ref[0, 0, d][j:j + 1, :]
                cn_sc[d * hb + j, :, HEAD_DIM:] = jnp.broadcast_to(n_row, (HEAD_DIM, LANES)).T
    else:
        cn_sc[...] = jnp.zeros_like(cn_sc)

    neg_inf = jnp.float32(-jnp.inf)

    cols = []
    for d in range(2):
        def stat(k):
            lo = (d * GATE_STATS + k) * SUBLANES
            return g_ref[0, :, lo:lo + SUBLANES, :].reshape(nc * SUBLANES, LANES)
        a, b, pm, amax, bsum = (stat(k) for k in range(GATE_STATS))
        m = m0_ref[0, 0, d] if has_init else jnp.zeros((SUBLANES, LANES), F32)
        m_prev = [None] * nc
        for ci in (range(nc) if d == 0 else range(nc - 1, -1, -1)):
            rs = slice(ci * SUBLANES, (ci + 1) * SUBLANES)
            m_prev[ci] = m
            m = bsum[rs] + jnp.maximum(m, amax[rs])
        m_sc[d] = m
        mp = jnp.concatenate(m_prev, axis=0)
        mu = jnp.maximum(mp, pm)
        mu_last = jnp.maximum(mp, amax)
        rows_sc[0, d] = a.reshape(nc, SUBLANES, LANES)
        rows_sc[1, d] = jnp.exp(a - mu_last).reshape(nc, SUBLANES, LANES)
        rows_sc[2, d] = jnp.exp(mp - mu_last).reshape(nc, SUBLANES, LANES)
        cols.append((mu, jnp.exp(mp - mu), jnp.exp(-(b + mu))))
    zpad = jnp.zeros((LANES - 6 * SUBLANES, LANES), F32)
    for c in range(nc):
        pieces = []
        for d, ci in ((0, c), (1, nc - 1 - c)):
            pieces += [x[ci * SUBLANES:(ci + 1) * SUBLANES] for x in cols[d]]
        xt_sc[c] = jnp.concatenate(pieces + [zpad], axis=0).T

    row = lax.broadcasted_iota(jnp.int32, (CHUNK, CHUNK), 0)
    col = lax.broadcasted_iota(jnp.int32, (CHUNK, CHUNK), 1)
    causal = (col <= row, col >= row)
    ones_blk = jnp.ones((CHUNK, HEAD_DIM), BF16)

    def step(c, carry):
        chunk_of = (c, nc - 1 - c)
        xt = xt_sc[c]
        chains = []
        for d in range(2):
            ch = chunk_of[d]
            t0 = _chunk_start(ch)
            a_rows = rows_sc[0, d, ch]
            for j in range(hb):
                hs = slice(j * HEAD_DIM, (j + 1) * HEAD_DIM)
                qc = q_ref[0, pl.ds(t0, CHUNK), hs]
                base = 3 * d * SUBLANES + j
                mu_c = xt[:, base:base + 1]
                g_c = xt[:, base + SUBLANES:base + SUBLANES + 1]
                s = jnp.dot(qc, kt_ref[0, ch, hs, :], preferred_element_type=F32)
                p = s * jnp.exp(jnp.where(causal[d], a_rows[j:j + 1, :] - mu_c, neg_inf))
                gq = (qc.astype(F32) * g_c).astype(BF16)
                chains.append((d, j, ch, t0, jnp.concatenate([p.astype(BF16), gq], axis=1)))

        for d, j, ch, t0, lhs in chains:
            hs = slice(j * HEAD_DIM, (j + 1) * HEAD_DIM)
            base = 3 * d * SUBLANES + j
            en_c = xt[:, base + 2 * SUBLANES:base + 2 * SUBLANES + 1]
            v1 = jnp.concatenate([v_ref[0, pl.ds(t0, CHUNK), hs], ones_blk], axis=1)
            cn = cn_sc[d * hb + j]
            tot = jnp.dot(lhs, jnp.concatenate([v1, cn.astype(BF16)], axis=0),
                          preferred_element_type=F32)
            hval = tot[:, :HEAD_DIM] / jnp.maximum(jnp.abs(tot[:, HEAD_DIM:]), en_c)
            dst = hf_sc if d == 0 else hb_sc
            dst[pl.ds(t0, CHUNK), hs] = hval

            e_row = rows_sc[1, d, ch][j:j + 1, :]
            gs_row = rows_sc[2, d, ch][j:j + 1, :]
            wkt = (kt_ref[0, ch, hs, :].astype(F32) * e_row).astype(BF16)
            dcn = jnp.dot(wkt, v1, preferred_element_type=F32)
            gs_blk = jnp.broadcast_to(gs_row, (HEAD_DIM, LANES))
            cn_sc[d * hb + j] = jnp.concatenate([gs_blk, gs_blk], axis=1) * cn + dcn
        return carry

    if nc <= 2:
        for c in range(nc):
            step(c, 0)
    else:
        lax.fori_loop(0, nc, step, 0, unroll=2)

    def finalize(c, carry):
        t0 = _chunk_start(c)
        hsum = hf_sc[pl.ds(t0, CHUNK), :] + hb_sc[pl.ds(t0, CHUNK), :]
        og = o_ref[0, pl.ds(t0, CHUNK), :]
        for j in range(hb):
            hs = slice(j * HEAD_DIM, (j + 1) * HEAD_DIM)
            hn = _rms(hsum[:, hs], nw_ref[0:1, hs])
            out_ref[0, pl.ds(t0, CHUNK), hs] = (hn * og[:, hs]).astype(BF16)
        return carry

    if nc <= 2:
        for c in range(nc):
            finalize(c, 0)
    else:
        lax.fori_loop(0, nc, finalize, 0, unroll=FINALIZE_UNROLL)

    if emit_state:
        pad_rows = jnp.zeros((SUBLANES - hb, LANES), F32)
        for d in range(2):
            n_rows = []
            for j in range(hb):
                cout_ref[0, d, j] = cn_sc[d * hb + j, :, 0:HEAD_DIM]
                n_rows.append(cn_sc[d * hb + j, :, HEAD_DIM:].T[0:1, :])
            nout_ref[0, 0, d] = jnp.concatenate(n_rows + ([pad_rows] if hb < SUBLANES else []), axis=0)
            mout_ref[0, 0, d] = m_sc[d]


def _mlstm_call(q, kt, v, o, stats, nw, init, *, batch, seq_len, emit_state):
    hb = HEADS_PER_STEP
    nhb = N_HEADS // hb
    nc = seq_len // CHUNK
    w = hb * HEAD_DIM
    seq_map = lambda b, k: (b, 0, k)
    in_specs = [pl.BlockSpec((1, seq_len, w), seq_map),
                pl.BlockSpec((1, nc, w, CHUNK), lambda b, k: (b, 0, k, 0)),
                pl.BlockSpec((1, seq_len, w), seq_map),
                pl.BlockSpec((1, seq_len, w), seq_map),
                pl.BlockSpec((1, nc, STAT_ROWS, CHUNK), lambda b, k: (b, 0, k, 0)),
                pl.BlockSpec((1, w), lambda b, k: (0, k))]
    args = [q, kt, v, o, stats, nw]
    state_map = lambda b, k: (b, 0, k, 0, 0)
    vec_map = lambda b, k: (b, k, 0, 0, 0)
    vec_spec = pl.BlockSpec((1, 1, 2, SUBLANES, LANES), vec_map)
    if init is not None:
        c0, n0, m0 = init
        in_specs += [pl.BlockSpec((1, 2, hb, HEAD_DIM, HEAD_DIM), state_map), vec_spec, vec_spec]
        args += [c0, n0, m0]
    out_specs = [pl.BlockSpec((1, seq_len, w), seq_map)]
    out_shape = [jax.ShapeDtypeStruct((batch, seq_len, MLSTM_WIDTH), BF16)]
    if emit_state:
        out_specs += [pl.BlockSpec((1, 2, hb, HEAD_DIM, HEAD_DIM), state_map), vec_spec, vec_spec]
        out_shape += [jax.ShapeDtypeStruct((batch, 2, N_HEADS, HEAD_DIM, HEAD_DIM), F32),
                      jax.ShapeDtypeStruct((batch, nhb, 2, SUBLANES, LANES), F32),
                      jax.ShapeDtypeStruct((batch, nhb, 2, SUBLANES, LANES), F32)]
    kern = functools.partial(_mlstm_kernel, nc=nc, hb=hb, has_init=init is not None,
                             emit_state=emit_state)
    return pl.pallas_call(
        kern,
        grid=(batch, nhb),
        in_specs=in_specs,
        out_specs=out_specs,
        out_shape=out_shape,
        scratch_shapes=[pltpu.VMEM((seq_len, w), F32),
                        pltpu.VMEM((seq_len, w), F32),
                        pltpu.VMEM((2 * hb, HEAD_DIM, 2 * HEAD_DIM), F32),
                        pltpu.VMEM((2, SUBLANES, LANES), F32),
                        pltpu.VMEM((nc, LANES, LANES), F32),
                        pltpu.VMEM((3, 2, nc, SUBLANES, LANES), F32)],
        compiler_params=_cparams(("arbitrary", "arbitrary")),
        name="mlstm",
    )(*args)


def _shift_rows(x, dlt):
    return pltpu.roll(x, (-dlt) % x.shape[0], axis=0)


def _pool_minus_self(x, half, pos, seg, inv_cnt):
    fwd = x
    k = 1
    while k < half:
        fwd = fwd + jnp.where(pos + k < seg, _shift_rows(fwd, k), 0.0)
        k *= 2
    bwd = jnp.where(pos >= 1, _shift_rows(x, -1), 0.0)
    k = 1
    while k < half:
        bwd = bwd + jnp.where(pos >= k, _shift_rows(bwd, -k), 0.0)
        k *= 2
    return (fwd + bwd) * inv_cnt - x


OUTPROJ_ROW_BLOCKS = 4
OUTPROJ_EPI_ROWS = 64
OUTPROJ_SLICE_COLS = 256


def _outproj_kernel(hm_ref, hp_ref, x_ref, mod_ref, nw_ref, wout_ref, x1_ref, h2_ref, mix_ref):
    tm = x_ref.shape[0]
    nblk = OUTPROJ_ROW_BLOCKS
    rb = tm // nblk
    nslice = D_MODEL // OUTPROJ_SLICE_COLS
    mod = mod_ref[0]
    gain1 = nw_ref[1:2, :] * mod[2:3, :]
    gain2 = nw_ref[2:3, :] * (1.0 + mod[4:5, :])
    shift2 = mod[3:4, :]

    def epilogue_slices(r):
        def one(lo):
            rows = slice(lo, lo + OUTPROJ_EPI_ROWS)
            mix = mix_ref[rows, :]
            ms = jnp.mean(mix * mix, axis=-1, keepdims=True)
            x1 = x_ref[rows, :] + (mix * lax.rsqrt(ms + EPS)) * gain1
            x1_ref[rows, :] = x1
            ms2 = jnp.mean(x1 * x1, axis=-1, keepdims=True)
            h2_ref[rows, :] = ((x1 * lax.rsqrt(ms2 + EPS)) * gain2 + shift2).astype(BF16)
        return [functools.partial(one, r * rb + k * OUTPROJ_EPI_ROWS)
                for k in range(rb // OUTPROJ_EPI_ROWS)]

    def run_block(r, pending):
        rows = slice(r * rb, (r + 1) * rb)
        lhs = jnp.concatenate([hm_ref[rows, :], hp_ref[rows, :]], axis=1)
        for idx in range(nslice):
            ns = slice(idx * OUTPROJ_SLICE_COLS, (idx + 1) * OUTPROJ_SLICE_COLS)
            mix_ref[rows, ns] = jnp.dot(lhs, wout_ref[:, ns], preferred_element_type=F32)
            if pending and idx % 4 == 3:
                pending.pop(0)()
        while pending:
            pending.pop(0)()

    pending = []
    for r in range(nblk):
        run_block(r, pending)
        pending = epilogue_slices(r)
    for fn in pending:
        fn()


def _outproj_call(hm, hp, x, mod, mod_map, norm_w, w_out, *, tm):
    tokens = x.shape[0]
    tok = lambda i: (i, 0)
    return pl.pallas_call(
        _outproj_kernel,
        grid=(tokens // tm,),
        in_specs=[pl.BlockSpec((tm, MLSTM_WIDTH), tok),
                  pl.BlockSpec((tm, POOL_WIDTH), tok),
                  pl.BlockSpec((tm, D_MODEL), tok),
                  pl.BlockSpec((1, N_MOD, D_MODEL), mod_map),
                  _resident((4, D_MODEL)),
                  _resident(w_out.shape)],
        out_specs=[pl.BlockSpec((tm, D_MODEL), tok),
                   pl.BlockSpec((tm, D_MODEL), tok)],
        out_shape=[jax.ShapeDtypeStruct((tokens, D_MODEL), F32),
                   jax.ShapeDtypeStruct((tokens, D_MODEL), BF16)],
        scratch_shapes=[pltpu.VMEM((tm, D_MODEL), F32)],
        compiler_params=_cparams(("arbitrary",)),
        name="out_proj",
    )(hm, hp, x, mod, norm_w, w_out)


MLP_LAST_ROW_BLOCKS = 4
MLP_SLICE_COLS = 256


def _mlp_kernel(h_ref, w1_ref, w2_ref, x1_ref, mod_ref, nw_ref, out_ref, acc_ref):
    k = pl.program_id(1)
    last = pl.num_programs(1) - 1
    tm, tf = h_ref.shape[0], w1_ref.shape[1]

    def hidden(rows, cols=slice(None)):
        a = jnp.dot(h_ref[rows, :], w1_ref[:, cols], preferred_element_type=F32)
        return jnp.square(jnp.maximum(a, 0.0)).astype(BF16)

    @pl.when(k == 0)
    def _():
        acc_ref[...] = jnp.dot(hidden(slice(None)), w2_ref[...], preferred_element_type=F32)

    @pl.when((k > 0) & (k < last))
    def _():
        acc_ref[...] += jnp.dot(hidden(slice(None)), w2_ref[...], preferred_element_type=F32)

    @pl.when(k == last)
    def _():
        rb = tm // MLP_LAST_ROW_BLOCKS
        gain = nw_ref[3:4, :] * mod_ref[0, 5:6, :]

        def epilogue_slices(r):
            def one(lo):
                rows = slice(lo, lo + NORM_SLICE_ROWS)
                y = acc_ref[rows, :]
                ms = jnp.mean(y * y, axis=-1, keepdims=True)
                out_ref[rows, :] = x1_ref[rows, :] + (y * lax.rsqrt(ms + EPS)) * gain
            return [functools.partial(one, r * rb + j * NORM_SLICE_ROWS)
                    for j in range(rb // NORM_SLICE_ROWS)]

        pending = []
        n1, n2 = tf // MLP_SLICE_COLS, D_MODEL // MLP_SLICE_COLS
        for r in range(MLP_LAST_ROW_BLOCKS):
            rows = slice(r * rb, (r + 1) * rb)
            parts = []
            for idx in range(n1):
                parts.append(hidden(rows, slice(idx * MLP_SLICE_COLS, (idx + 1) * MLP_SLICE_COLS)))
                if pending and idx % 2 == 1:
                    pending.pop(0)()
            a = jnp.concatenate(parts, axis=1)
            for idx in range(n2):
                ns = slice(idx * MLP_SLICE_COLS, (idx + 1) * MLP_SLICE_COLS)
                acc_ref[rows, ns] += jnp.dot(a, w2_ref[:, ns], preferred_element_type=F32)
                if pending and idx % 2 == 1:
                    pending.pop(0)()
            while pending:
                pending.pop(0)()
            pending = epilogue_slices(r)
        for fn in pending:
            fn()


def _mlp_call(h2, x1, mod, mod_map, norm_w, w1, w2, *, tm, tf):
    tokens = x1.shape[0]
    tok = lambda i, k: (i, 0)
    return pl.pallas_call(
        _mlp_kernel,
        grid=(tokens // tm, D_FF // tf),
        in_specs=[pl.BlockSpec((tm, D_MODEL), tok),
                  pl.BlockSpec((D_MODEL, tf), lambda i, k: (0, k)),
                  pl.BlockSpec((tf, D_MODEL), lambda i, k: (k, 0)),
                  pl.BlockSpec((tm, D_MODEL), tok),
                  pl.BlockSpec((1, N_MOD, D_MODEL), lambda i, k: mod_map(i)),
                  pl.BlockSpec((4, D_MODEL), lambda i, k: (0, 0))],
        out_specs=pl.BlockSpec((tm, D_MODEL), tok),
        out_shape=jax.ShapeDtypeStruct((tokens, D_MODEL), F32),
        scratch_shapes=[pltpu.VMEM((tm, D_MODEL), F32)],
        compiler_params=_cparams(("arbitrary", "arbitrary")),
        name="mlp",
    )(h2, w1, w2, x1, mod, norm_w)


def _gate_weights(w_g, gate_bias):
    hb = HEADS_PER_STEP
    nhb = N_HEADS // hb
    wg = w_g.reshape(D_MODEL, N_GATES, nhb, hb)
    wg = jnp.pad(wg, ((0, 0), (0, 0), (0, 0), (0, SUBLANES - hb)))
    wgt = wg.transpose(2, 1, 3, 0).reshape(nhb * GATE_ROWS, D_MODEL)
    gb = jnp.pad(gate_bias.reshape(N_GATES, nhb, hb), ((0, 0), (0, 0), (0, SUBLANES - hb)))
    gb = gb.transpose(1, 0, 2).reshape(nhb * GATE_ROWS)
    return wgt.astype(BF16), gb.astype(F32)


def _pack_head_rows(x):
    hb = HEADS_PER_STEP
    nhb = N_HEADS // hb
    x = x.reshape(x.shape[:2] + (nhb, hb) + x.shape[3:])
    x = jnp.moveaxis(x, 2, 1)
    pad = [(0, 0)] * x.ndim
    pad[3] = (0, SUBLANES - hb)
    return jnp.pad(x, pad)


def _unpack_head_rows(x):
    x = jnp.moveaxis(x[:, :, :, :HEADS_PER_STEP], 1, 2)
    return x.reshape(x.shape[:2] + (N_HEADS,) + x.shape[4:])


def _path(x, mod, mod_row, init, params, *, seg, tm, tm_mlp, emit_state):
    batch, seq_len, _ = x.shape
    wts, mlstm_nw, pool_w, pool_scale, w_out, norm_w, w1, w2 = params
    tokens = batch * seq_len
    xf = x.reshape(tokens, D_MODEL)
    q, kt, v, og, hp, stats = _inproj_call(xf, mod, mod_row, norm_w, wts, pool_w, pool_scale,
                                           batch=batch, seq_len=seq_len, tm=tm, seg=seg)
    shp = (batch, seq_len, MLSTM_WIDTH)
    res = _mlstm_call(q.reshape(shp), kt, v.reshape(shp), og.reshape(shp), stats, mlstm_nw, init,
                      batch=batch, seq_len=seq_len, emit_state=emit_state)
    hm = res[0].reshape(tokens, MLSTM_WIDTH)

    def mod_map_for(t):
        if mod_row is None:
            per = seq_len // t
            return lambda i: (1 + i // per, 0, 0)
        return lambda i: (mod_row, 0, 0)

    x1, h2 = _outproj_call(hm, hp, xf, mod, mod_map_for(tm), norm_w, w_out, tm=tm)
    y = _mlp_call(h2, x1, mod, mod_map_for(tm_mlp), norm_w, w1, w2, tm=tm_mlp, tf=1024)
    return y.reshape(batch, seq_len, D_MODEL), res[1:]


def kernel(x_prompt, x_sample, c, state_C, state_n, state_m, c_ctx, w_in, gate_bias, mlstm_norm_w,
           pool_w, pool_scale, w_out, ada_w, ada_b, norm_w, w1, w2):
    m4 = 4 * MLSTM_WIDTH
    yp, ys = x_prompt, x_sample
    new_c, new_n, new_m = [], [], []
    for layer in range(w_in.shape[0]):
        wl = w_in[layer]
        wgt, gb = _gate_weights(wl[:, m4:m4 + N_GATES * N_HEADS], gate_bias[layer])
        wkg = jnp.concatenate([wl[:, MLSTM_WIDTH:2 * MLSTM_WIDTH].T.astype(BF16), wgt], axis=0)
        wts = (wl[:, 0:MLSTM_WIDTH].astype(BF16),
               wkg,
               wl[:, 2 * MLSTM_WIDTH:3 * MLSTM_WIDTH].astype(BF16),
               wl[:, 3 * MLSTM_WIDTH:m4].astype(BF16),
               wl[:, m4 + N_GATES * N_HEADS:].astype(BF16),
               gb)
        params = (wts, mlstm_norm_w[layer].reshape(1, MLSTM_WIDTH), pool_w[layer].astype(BF16),
                  pool_scale[layer], w_out[layer].astype(BF16), norm_w[layer],
                  w1[layer].astype(BF16), w2[layer].astype(BF16))

        n_lat = c.shape[0]
        rows = 16
        cond = jnp.concatenate([c_ctx[None, :], c, jnp.zeros((rows - 1 - n_lat, D_MODEL), F32)], axis=0)
        mod = _mod_call(cond, ada_w[layer], ada_b[layer]).reshape(rows, N_MOD, D_MODEL)

        yp, st = _path(yp, mod, 0, None, params, seg=yp.shape[1], tm=512, tm_mlp=512, emit_state=True)
        cc, nn, mm = st
        new_c.append(cc)
        new_n.append(_unpack_head_rows(nn))
        new_m.append(_unpack_head_rows(mm)[..., 0])

        sc = state_C[:, layer].astype(F32)
        sn = _pack_head_rows(state_n[:, layer].astype(F32))
        sm = _pack_head_rows(jnp.broadcast_to(state_m[:, layer].astype(F32)[..., None],
                                              (n_lat, 2, N_HEADS, LANES)))
        ys, _ = _path(ys, mod, None, (sc, sn, sm), params, seg=GRID_W, tm=512, tm_mlp=512,
                      emit_state=False)
    return (yp, ys, jnp.stack(new_c, axis=1), jnp.stack(new_n, axis=1), jnp.stack(new_m, axis=1))
```

```python
import functools

import jax
import jax.numpy as jnp
from jax import lax
from jax.experimental import pallas as pl
from jax.experimental.pallas import tpu as pltpu

F32 = jnp.float32
BF16 = jnp.bfloat16

D_MODEL = 2048
MLSTM_WIDTH = 1024
N_HEADS = 8
HEAD_DIM = 128
POOL_WIDTH = 1024
POOL_WINDOWS = (2, 4, 8, 16)
POOL_GROUP_DIM = 256
N_GATES = 4
D_FF = 8192
CHUNK = 128
N_MOD = 6
GRID_W = 64
EPS = 1e-6

LANES = 128
SUBLANES = 8
VMEM_LIMIT = 56 * 1024 * 1024

HEADS_PER_STEP = 4
GATE_ROWS = N_GATES * SUBLANES
GATE_STATS = 5
STAT_ROWS = 2 * GATE_STATS * SUBLANES
FINALIZE_UNROLL = 4


def _cparams(sem):
    return pltpu.CompilerParams(dimension_semantics=sem, vmem_limit_bytes=VMEM_LIMIT)


def _resident(shape):
    nd = len(shape)
    return pl.BlockSpec(shape, lambda *_: (0,) * nd, pipeline_mode=pl.Buffered(1))


def _rms(x, w):
    ms = jnp.mean(x * x, axis=-1, keepdims=True)
    return x * lax.rsqrt(ms + EPS) * w


def _mod_kernel(cond_ref, w_ref, b_ref, out_ref):
    c = cond_ref[...]
    s = c * jax.nn.sigmoid(c)
    out_ref[...] = jnp.dot(s.astype(BF16), w_ref[...].astype(BF16),
                           preferred_element_type=F32) + b_ref[...]


def _mod_call(cond, ada_w, ada_b):
    rows = cond.shape[0]
    n = ada_w.shape[1]
    tn = 1536
    return pl.pallas_call(
        _mod_kernel,
        grid=(n // tn,),
        in_specs=[pl.BlockSpec((rows, D_MODEL), lambda j: (0, 0)),
                  pl.BlockSpec((D_MODEL, tn), lambda j: (0, j)),
                  pl.BlockSpec((1, tn), lambda j: (0, j))],
        out_specs=pl.BlockSpec((rows, tn), lambda j: (0, j)),
        out_shape=jax.ShapeDtypeStruct((rows, n), F32),
        compiler_params=_cparams(("arbitrary",)),
        name="mod",
    )(cond, ada_w, ada_b.reshape(1, n))


def _wprep_kernel(w_ref, wgt_ref, qkvo_ref, wkg_ref):
    w = w_ref[...]
    qkvo_ref[0] = w.astype(BF16)

    @pl.when(pl.program_id(0) == 1)
    def _():
        wkg_ref[0:MLSTM_WIDTH, :] = w.T.astype(BF16)
        wkg_ref[MLSTM_WIDTH:, :] = wgt_ref[...]


def _wprep_call(w_in_layer, wgt):
    grows = wgt.shape[0]
    return pl.pallas_call(
        _wprep_kernel,
        grid=(4,),
        in_specs=[pl.BlockSpec((D_MODEL, MLSTM_WIDTH), lambda j: (0, j)),
                  pl.BlockSpec((grows, D_MODEL), lambda j: (0, 0))],
        out_specs=[pl.BlockSpec((1, D_MODEL, MLSTM_WIDTH), lambda j: (j, 0, 0)),
                   pl.BlockSpec((MLSTM_WIDTH + grows, D_MODEL), lambda j: (0, 0))],
        out_shape=[jax.ShapeDtypeStruct((4, D_MODEL, MLSTM_WIDTH), BF16),
                   jax.ShapeDtypeStruct((MLSTM_WIDTH + grows, D_MODEL), BF16)],
        compiler_params=_cparams(("arbitrary",)),
        name="wprep",
    )(w_in_layer, wgt)


NORM_SLICE_ROWS = 64


def _lane_scan_pair(fwd, bwd, op, ident):
    lane = lax.broadcasted_iota(jnp.int32, fwd.shape, 1)
    k = 1
    while k < LANES:
        f_shift = pltpu.roll(fwd, k, axis=1)
        b_shift = pltpu.roll(bwd, LANES - k, axis=1)
        fwd = op(fwd, jnp.where(lane >= k, f_shift, ident))
        bwd = op(bwd, jnp.where(lane < LANES - k, b_shift, ident))
        k *= 2
    return fwd, bwd


def _inproj_kernel(x_ref, mod_ref, nw_ref, wq_ref, wkg_ref, wv_ref, wo_ref, wu_ref, gb_ref, pw_ref, ps_ref,
                   q_ref, kt_ref, v_ref, og_ref, hp_ref, g_ref, *, seqs, chunks, seg):
    x = x_ref[...]
    mod = mod_ref[0]
    tm = x.shape[0]
    h = (_rms(x, nw_ref[0:1, :]) * (1.0 + mod[1:2, :]) + mod[0:1, :]).astype(BF16)
    ktg = lax.dot_general(wkg_ref[...], h, (((1,), (1,)), ((), ())), preferred_element_type=F32)
    kt = ktg[0:MLSTM_WIDTH].astype(BF16)
    gt = ktg[MLSTM_WIDTH:] + gb_ref[...]
    nhb = gt.shape[0] // GATE_ROWS
    for s in range(seqs):
        for c in range(chunks):
            lo = (s * chunks + c) * CHUNK
            kt_ref[s, c] = kt[:, lo:lo + CHUNK]
    u = jnp.dot(h, wu_ref[...], preferred_element_type=F32)
    pos = lax.broadcasted_iota(jnp.int32, (tm, 1), 0) % seg

    def pool_group(gi):
        half = POOL_WINDOWS[gi] // 2
        inv_cnt = 1.0 / (jnp.minimum(pos + half, seg) - jnp.maximum(pos - half, 0)).astype(F32)
        cs = slice(gi * POOL_GROUP_DIM, (gi + 1) * POOL_GROUP_DIM)
        p = jnp.concatenate(
            [_pool_minus_self(u[:, lo:lo + LANES], half, pos, seg, inv_cnt).astype(BF16)
             for lo in range(cs.start, cs.stop, LANES)], axis=1)
        y = jnp.dot(p, pw_ref[gi], preferred_element_type=F32)
        hp_ref[:, cs] = (y * ps_ref[0:1, cs]).astype(BF16)

    q = jnp.dot(h, wq_ref[0], preferred_element_type=F32) * (HEAD_DIM ** -0.5)
    q_ref[...] = q.astype(BF16)
    pool_group(0)
    pool_group(1)
    v_ref[...] = jnp.dot(h, wv_ref[0], preferred_element_type=F32).astype(BF16)
    pool_group(2)
    og_ref[...] = jax.nn.sigmoid(jnp.dot(h, wo_ref[0], preferred_element_type=F32))
    pool_group(3)

    groups = [(s, c, hbk) for s in range(seqs) for c in range(chunks) for hbk in range(nhb)]

    def gate(g):
        return jnp.concatenate(
            [gt[hbk * GATE_ROWS + g * SUBLANES:hbk * GATE_ROWS + (g + 1) * SUBLANES,
                (s * chunks + c) * CHUNK:(s * chunks + c + 1) * CHUNK] for s, c, hbk in groups], axis=0)

    li = (gate(0), gate(2))
    lf = (jax.nn.log_sigmoid(gate(1)), jax.nn.log_sigmoid(gate(3)))
    b = _lane_scan_pair(lf[0], lf[1], jnp.add, 0.0)
    a = (li[0] - b[0], li[1] - b[1])
    pm = _lane_scan_pair(a[0], a[1], jnp.maximum, jnp.float32(-jnp.inf))
    stats = []
    for d in range(2):
        amax = jnp.broadcast_to(jnp.max(a[d], axis=1, keepdims=True), a[d].shape)
        bsum = jnp.broadcast_to(jnp.sum(lf[d], axis=1, keepdims=True), a[d].shape)
        stats += [a[d], b[d], pm[d], amax, bsum]
    for gi, (s, c, hbk) in enumerate(groups):
        rs = slice(gi * SUBLANES, (gi + 1) * SUBLANES)
        g_ref[s, c, hbk * STAT_ROWS:(hbk + 1) * STAT_ROWS, :] = jnp.concatenate(
            [st[rs] for st in stats], axis=0)


def _inproj_call(x, mod, mod_row, norm_w, wts, pool_w, pool_scale, *, batch, seq_len, tm, seg):
    qkvo, wkg, wu, gb = wts
    col_block = lambda j: pl.BlockSpec((1, D_MODEL, MLSTM_WIDTH), lambda *_: (j, 0, 0),
                                       pipeline_mode=pl.Buffered(1))
    tokens = batch * seq_len
    nc = seq_len // CHUNK
    grows = wkg.shape[0] - MLSTM_WIDTH
    srows = grows // GATE_ROWS * STAT_ROWS
    if tm >= seq_len:
        seqs, chunks = tm // seq_len, nc
        kt_map = lambda i: (i, 0, 0, 0)
    else:
        seqs, chunks = 1, tm // CHUNK
        per = seq_len // tm
        kt_map = lambda i: (i // per, i % per, 0, 0)
    tiles_per_seq = max(seq_len // tm, 1)
    if mod_row is None:
        mod_map = lambda i: (1 + i // tiles_per_seq, 0, 0)
    else:
        mod_map = lambda i: (mod_row, 0, 0)
    tok = lambda i: (i, 0)
    gbb = jnp.broadcast_to(gb[:, None], (grows, tm))
    kern = functools.partial(_inproj_kernel, seqs=seqs, chunks=chunks, seg=seg)
    return pl.pallas_call(
        kern,
        grid=(tokens // tm,),
        in_specs=[pl.BlockSpec((tm, D_MODEL), tok),
                  pl.BlockSpec((1, N_MOD, D_MODEL), mod_map),
                  _resident((4, D_MODEL)),
                  col_block(0), _resident(wkg.shape), col_block(2),
                  col_block(3), _resident(wu.shape),
                  _resident((grows, tm)),
                  _resident(pool_w.shape),
                  _resident((1, POOL_WIDTH))],
        out_specs=[pl.BlockSpec((tm, MLSTM_WIDTH), tok),
                   pl.BlockSpec((seqs, chunks, MLSTM_WIDTH, CHUNK), kt_map),
                   pl.BlockSpec((tm, MLSTM_WIDTH), tok),
                   pl.BlockSpec((tm, MLSTM_WIDTH), tok),
                   pl.BlockSpec((tm, POOL_WIDTH), tok),
                   pl.BlockSpec((seqs, chunks, srows, CHUNK), kt_map)],
        out_shape=[jax.ShapeDtypeStruct((tokens, MLSTM_WIDTH), BF16),
                   jax.ShapeDtypeStruct((batch, nc, MLSTM_WIDTH, CHUNK), BF16),
                   jax.ShapeDtypeStruct((tokens, MLSTM_WIDTH), BF16),
                   jax.ShapeDtypeStruct((tokens, MLSTM_WIDTH), F32),
                   jax.ShapeDtypeStruct((tokens, POOL_WIDTH), BF16),
                   jax.ShapeDtypeStruct((batch, nc, srows, CHUNK), F32)],
        compiler_params=_cparams(("arbitrary",)),
        name="in_proj",
    )(x, mod, norm_w, qkvo, wkg, qkvo, qkvo, wu, gbb, pool_w, pool_scale.reshape(1, POOL_WIDTH))


def _chunk_start(c):
    return c * CHUNK if isinstance(c, int) else pl.multiple_of(c * CHUNK, CHUNK)


def _mlstm_kernel(*refs, nc, hb, has_init, emit_state):
    it = iter(refs)
    q_ref, kt_ref, v_ref, o_ref, g_ref, nw_ref = (next(it) for _ in range(6))
    if has_init:
        c0_ref, n0_ref, m0_ref = (next(it) for _ in range(3))
    out_ref = next(it)
    if emit_state:
        cout_ref, nout_ref, mout_ref = (next(it) for _ in range(3))
    hf_sc, hb_sc, cn_sc, m_sc, xt_sc, rows_sc = (next(it) for _ in range(6))

    if has_init:
        for d in range(2):
            for j in range(hb):
                cn_sc[d * hb + j, :, 0:HEAD_DIM] = c0_ref[0, d, j]
                n_row = n0_ref[0, 0, d][j:j + 1, :]
                cn_sc[d * hb + j, :, HEAD_DIM:] = jnp.broadcast_to(n_row, (HEAD_DIM, LANES)).T
    else:
        cn_sc[...] = jnp.zeros_like(cn_sc)

    neg_inf = jnp.float32(-jnp.inf)

    cols = []
    for d in range(2):
        def stat(k):
            lo = (d * GATE_STATS + k) * SUBLANES
            return g_ref[0, :, lo:lo + SUBLANES, :].reshape(nc * SUBLANES, LANES)
        a, b, pm, amax, bsum = (stat(k) for k in range(GATE_STATS))
        m = m0_ref[0, 0, d] if has_init else jnp.zeros((SUBLANES, LANES), F32)
        m_prev = [None] * nc
        for ci in (range(nc) if d == 0 else range(nc - 1, -1, -1)):
            rs = slice(ci * SUBLANES, (ci + 1) * SUBLANES)
            m_prev[ci] = m
            m = bsum[rs] + jnp.maximum(m, amax[rs])
        m_sc[d] = m
        mp = jnp.concatenate(m_prev, axis=0)
        mu = jnp.maximum(mp, pm)
        mu_last = jnp.maximum(mp, amax)
        rows_sc[0, d] = a.reshape(nc, SUBLANES, LANES)
        rows_sc[1, d] = jnp.exp(a - mu_last).reshape(nc, SUBLANES, LANES)
        rows_sc[2, d] = jnp.exp(mp - mu_last).reshape(nc, SUBLANES, LANES)
        cols.append((mu, jnp.exp(mp - mu), jnp.exp(-(b + mu))))
    zpad = jnp.zeros((LANES - 6 * SUBLANES, LANES), F32)
    for c in range(nc):
        pieces = []
        for d, ci in ((0, c), (1, nc - 1 - c)):
            pieces += [x[ci * SUBLANES:(ci + 1) * SUBLANES] for x in cols[d]]
        xt_sc[c] = jnp.concatenate(pieces + [zpad], axis=0).T

    row = lax.broadcasted_iota(jnp.int32, (CHUNK, CHUNK), 0)
    col = lax.broadcasted_iota(jnp.int32, (CHUNK, CHUNK), 1)
    causal = (col <= row, col >= row)
    ones_blk = jnp.ones((CHUNK, HEAD_DIM), BF16)

    def step(c, carry):
        chunk_of = (c, nc - 1 - c)
        xt = xt_sc[c]
        chains = []
        for d in range(2):
            ch = chunk_of[d]
            t0 = _chunk_start(ch)
            a_rows = rows_sc[0, d, ch]
            for j in range(hb):
                hs = slice(j * HEAD_DIM, (j + 1) * HEAD_DIM)
                qc = q_ref[0, pl.ds(t0, CHUNK), hs]
                base = 3 * d * SUBLANES + j
                mu_c = xt[:, base:base + 1]
                g_c = xt[:, base + SUBLANES:base + SUBLANES + 1]
                s = jnp.dot(qc, kt_ref[0, ch, hs, :], preferred_element_type=F32)
                p = s * jnp.exp(jnp.where(causal[d], a_rows[j:j + 1, :] - mu_c, neg_inf))
                gq = (qc.astype(F32) * g_c).astype(BF16)
                chains.append((d, j, ch, t0, jnp.concatenate([p.astype(BF16), gq], axis=1)))

        for d, j, ch, t0, lhs in chains:
            hs = slice(j * HEAD_DIM, (j + 1) * HEAD_DIM)
            base = 3 * d * SUBLANES + j
            en_c = xt[:, base + 2 * SUBLANES:base + 2 * SUBLANES + 1]
            v1 = jnp.concatenate([v_ref[0, pl.ds(t0, CHUNK), hs], ones_blk], axis=1)
            cn = cn_sc[d * hb + j]
            tot = jnp.dot(lhs, jnp.concatenate([v1, cn.astype(BF16)], axis=0),
                          preferred_element_type=F32)
            hval = tot[:, :HEAD_DIM] / jnp.maximum(jnp.abs(tot[:, HEAD_DIM:]), en_c)
            dst = hf_sc if d == 0 else hb_sc
            dst[pl.ds(t0, CHUNK), hs] = hval

            e_row = rows_sc[1, d, ch][j:j + 1, :]
            gs_row = rows_sc[2, d, ch][j:j + 1, :]
            wkt = (kt_ref[0, ch, hs, :].astype(F32) * e_row).astype(BF16)
            dcn = jnp.dot(wkt, v1, preferred_element_type=F32)
            gs_blk = jnp.broadcast_to(gs_row, (HEAD_DIM, LANES))
            cn_sc[d * hb + j] = jnp.concatenate([gs_blk, gs_blk], axis=1) * cn + dcn
        return carry

    if nc <= 2:
        for c in range(nc):
            step(c, 0)
    else:
        lax.fori_loop(0, nc, step, 0, unroll=2)

    def finalize(c, carry):
        t0 = _chunk_start(c)
        hsum = hf_sc[pl.ds(t0, CHUNK), :] + hb_sc[pl.ds(t0, CHUNK), :]
        og = o_ref[0, pl.ds(t0, CHUNK), :]
        for j in range(hb):
            hs = slice(j * HEAD_DIM, (j + 1) * HEAD_DIM)
            hn = _rms(hsum[:, hs], nw_ref[0:1, hs])
            out_ref[0, pl.ds(t0, CHUNK), hs] = (hn * og[:, hs]).astype(BF16)
        return carry

    if nc <= 2:
        for c in range(nc):
            finalize(c, 0)
    else:
        lax.fori_loop(0, nc, finalize, 0, unroll=FINALIZE_UNROLL)

    if emit_state:
        pad_rows = jnp.zeros((SUBLANES - hb, LANES), F32)
        for d in range(2):
            n_rows = []
            for j in range(hb):
                cout_ref[0, d, j] = cn_sc[d * hb + j, :, 0:HEAD_DIM]
                n_rows.append(cn_sc[d * hb + j, :, HEAD_DIM:].T[0:1, :])
            nout_ref[0, 0, d] = jnp.concatenate(n_rows + ([pad_rows] if hb < SUBLANES else []), axis=0)
            mout_ref[0, 0, d] = m_sc[d]


def _mlstm_call(q, kt, v, o, stats, nw, init, *, batch, seq_len, emit_state):
    hb = HEADS_PER_STEP
    nhb = N_HEADS // hb
    nc = seq_len // CHUNK
    w = hb * HEAD_DIM
    seq_map = lambda b, k: (b, 0, k)
    in_specs = [pl.BlockSpec((1, seq_len, w), seq_map),
                pl.BlockSpec((1, nc, w, CHUNK), lambda b, k: (b, 0, k, 0)),
                pl.BlockSpec((1, seq_len, w), seq_map),
                pl.BlockSpec((1, seq_len, w), seq_map),
                pl.BlockSpec((1, nc, STAT_ROWS, CHUNK), lambda b, k: (b, 0, k, 0)),
                pl.BlockSpec((1, w), lambda b, k: (0, k))]
    args = [q, kt, v, o, stats, nw]
    state_map = lambda b, k: (b, 0, k, 0, 0)
    vec_map = lambda b, k: (b, k, 0, 0, 0)
    vec_spec = pl.BlockSpec((1, 1, 2, SUBLANES, LANES), vec_map)
    if init is not None:
        c0, n0, m0 = init
        in_specs += [pl.BlockSpec((1, 2, hb, HEAD_DIM, HEAD_DIM), state_map), vec_spec, vec_spec]
        args += [c0, n0, m0]
    out_specs = [pl.BlockSpec((1, seq_len, w), seq_map)]
    out_shape = [jax.ShapeDtypeStruct((batch, seq_len, MLSTM_WIDTH), BF16)]
    if emit_state:
        out_specs += [pl.BlockSpec((1, 2, hb, HEAD_DIM, HEAD_DIM), state_map), vec_spec, vec_spec]
        out_shape += [jax.ShapeDtypeStruct((batch, 2, N_HEADS, HEAD_DIM, HEAD_DIM), F32),
                      jax.ShapeDtypeStruct((batch, nhb, 2, SUBLANES, LANES), F32),
                      jax.ShapeDtypeStruct((batch, nhb, 2, SUBLANES, LANES), F32)]
    kern = functools.partial(_mlstm_kernel, nc=nc, hb=hb, has_init=init is not None,
                             emit_state=emit_state)
    return pl.pallas_call(
        kern,
        grid=(batch, nhb),
        in_specs=in_specs,
        out_specs=out_specs,
        out_shape=out_shape,
        scratch_shapes=[pltpu.VMEM((seq_len, w), F32),
                        pltpu.VMEM((seq_len, w), F32),
                        pltpu.VMEM((2 * hb, HEAD_DIM, 2 * HEAD_DIM), F32),
                        pltpu.VMEM((2, SUBLANES, LANES), F32),
                        pltpu.VMEM((nc, LANES, LANES), F32),
                        pltpu.VMEM((3, 2, nc, SUBLANES, LANES), F32)],
        compiler_params=_cparams(("arbitrary", "arbitrary")),
        name="mlstm",
    )(*args)


def _shift_rows(x, dlt):
    return pltpu.roll(x, (-dlt) % x.shape[0], axis=0)


def _pool_minus_self(x, half, pos, seg, inv_cnt):
    fwd = x
    k = 1
    while k < half:
        fwd = fwd + jnp.where(pos + k < seg, _shift_rows(fwd, k), 0.0)
        k *= 2
    bwd = jnp.where(pos >= 1, _shift_rows(x, -1), 0.0)
    k = 1
    while k < half:
        bwd = bwd + jnp.where(pos >= k, _shift_rows(bwd, -k), 0.0)
        k *= 2
    return (fwd + bwd) * inv_cnt - x


OUTPROJ_ROW_BLOCKS = 4
OUTPROJ_EPI_ROWS = 64
OUTPROJ_SLICE_COLS = 256


def _outproj_kernel(hm_ref, hp_ref, x_ref, mod_ref, nw_ref, wout_ref, x1_ref, h2_ref, mix_ref):
    tm = x_ref.shape[0]
    nblk = OUTPROJ_ROW_BLOCKS
    rb = tm // nblk
    nslice = D_MODEL // OUTPROJ_SLICE_COLS
    mod = mod_ref[0]
    gain1 = nw_ref[1:2, :] * mod[2:3, :]
    gain2 = nw_ref[2:3, :] * (1.0 + mod[4:5, :])
    shift2 = mod[3:4, :]

    def epilogue_slices(r):
        def one(lo):
            rows = slice(lo, lo + OUTPROJ_EPI_ROWS)
            mix = mix_ref[rows, :]
            ms = jnp.mean(mix * mix, axis=-1, keepdims=True)
            x1 = x_ref[rows, :] + (mix * lax.rsqrt(ms + EPS)) * gain1
            x1_ref[rows, :] = x1
            ms2 = jnp.mean(x1 * x1, axis=-1, keepdims=True)
            h2_ref[rows, :] = ((x1 * lax.rsqrt(ms2 + EPS)) * gain2 + shift2).astype(BF16)
        return [functools.partial(one, r * rb + k * OUTPROJ_EPI_ROWS)
                for k in range(rb // OUTPROJ_EPI_ROWS)]

    def run_block(r, pending):
        rows = slice(r * rb, (r + 1) * rb)
        lhs = jnp.concatenate([hm_ref[rows, :], hp_ref[rows, :]], axis=1)
        for idx in range(nslice):
            ns = slice(idx * OUTPROJ_SLICE_COLS, (idx + 1) * OUTPROJ_SLICE_COLS)
            mix_ref[rows, ns] = jnp.dot(lhs, wout_ref[:, ns], preferred_element_type=F32)
            if pending and idx % 4 == 3:
                pending.pop(0)()
        while pending:
            pending.pop(0)()

    pending = []
    for r in range(nblk):
        run_block(r, pending)
        pending = epilogue_slices(r)
    for fn in pending:
        fn()


def _outproj_call(hm, hp, x, mod, mod_map, norm_w, w_out, *, tm):
    tokens = x.shape[0]
    tok = lambda i: (i, 0)
    return pl.pallas_call(
        _outproj_kernel,
        grid=(tokens // tm,),
        in_specs=[pl.BlockSpec((tm, MLSTM_WIDTH), tok),
                  pl.BlockSpec((tm, POOL_WIDTH), tok),
                  pl.BlockSpec((tm, D_MODEL), tok),
                  pl.BlockSpec((1, N_MOD, D_MODEL), mod_map),
                  _resident((4, D_MODEL)),
                  _resident(w_out.shape)],
        out_specs=[pl.BlockSpec((tm, D_MODEL), tok),
                   pl.BlockSpec((tm, D_MODEL), tok)],
        out_shape=[jax.ShapeDtypeStruct((tokens, D_MODEL), F32),
                   jax.ShapeDtypeStruct((tokens, D_MODEL), BF16)],
        scratch_shapes=[pltpu.VMEM((tm, D_MODEL), F32)],
        compiler_params=_cparams(("arbitrary",)),
        name="out_proj",
    )(hm, hp, x, mod, norm_w, w_out)


MLP_LAST_ROW_BLOCKS = 4
MLP_SLICE_COLS = 256


def _mlp_kernel(h_ref, w1_ref, w2_ref, x1_ref, mod_ref, nw_ref, out_ref, acc_ref):
    k = pl.program_id(1)
    last = pl.num_programs(1) - 1
    tm, tf = h_ref.shape[0], w1_ref.shape[1]

    def hidden(rows, cols=slice(None)):
        a = jnp.dot(h_ref[rows, :], w1_ref[:, cols], preferred_element_type=F32)
        return jnp.square(jnp.maximum(a, 0.0)).astype(BF16)

    @pl.when(k == 0)
    def _():
        acc_ref[...] = jnp.dot(hidden(slice(None)), w2_ref[...], preferred_element_type=F32)

    @pl.when((k > 0) & (k < last))
    def _():
        acc_ref[...] += jnp.dot(hidden(slice(None)), w2_ref[...], preferred_element_type=F32)

    @pl.when(k == last)
    def _():
        rb = tm // MLP_LAST_ROW_BLOCKS
        gain = nw_ref[3:4, :] * mod_ref[0, 5:6, :]

        def epilogue_slices(r):
            def one(lo):
                rows = slice(lo, lo + NORM_SLICE_ROWS)
                y = acc_ref[rows, :]
                ms = jnp.mean(y * y, axis=-1, keepdims=True)
                out_ref[rows, :] = x1_ref[rows, :] + (y * lax.rsqrt(ms + EPS)) * gain
            return [functools.partial(one, r * rb + j * NORM_SLICE_ROWS)
                    for j in range(rb // NORM_SLICE_ROWS)]

        pending = []
        n1, n2 = tf // MLP_SLICE_COLS, D_MODEL // MLP_SLICE_COLS
        for r in range(MLP_LAST_ROW_BLOCKS):
            rows = slice(r * rb, (r + 1) * rb)
            parts = []
            for idx in range(n1):
                parts.append(hidden(rows, slice(idx * MLP_SLICE_COLS, (idx + 1) * MLP_SLICE_COLS)))
                if pending and idx % 2 == 1:
                    pending.pop(0)()
            a = jnp.concatenate(parts, axis=1)
            for idx in range(n2):
                ns = slice(idx * MLP_SLICE_COLS, (idx + 1) * MLP_SLICE_COLS)
                acc_ref[rows, ns] += jnp.dot(a, w2_ref[:, ns], preferred_element_type=F32)
                if pending and idx % 2 == 1:
                    pending.pop(0)()
            while pending:
                pending.pop(0)()
            pending = epilogue_slices(r)
        for fn in pending:
            fn()


def _mlp_call(h2, x1, mod, mod_map, norm_w, w1, w2, *, tm, tf):
    tokens = x1.shape[0]
    tok = lambda i, k: (i, 0)
    return pl.pallas_call(
        _mlp_kernel,
        grid=(tokens // tm, D_FF // tf),
        in_specs=[pl.BlockSpec((tm, D_MODEL), tok),
                  pl.BlockSpec((D_MODEL, tf), lambda i, k: (0, k)),
                  pl.BlockSpec((tf, D_MODEL), lambda i, k: (k, 0)),
                  pl.BlockSpec((tm, D_MODEL), tok),
                  pl.BlockSpec((1, N_MOD, D_MODEL), lambda i, k: mod_map(i)),
                  pl.BlockSpec((4, D_MODEL), lambda i, k: (0, 0))],
        out_specs=pl.BlockSpec((tm, D_MODEL), tok),
        out_shape=jax.ShapeDtypeStruct((tokens, D_MODEL), F32),
        scratch_shapes=[pltpu.VMEM((tm, D_MODEL), F32)],
        compiler_params=_cparams(("arbitrary", "arbitrary")),
        name="mlp",
    )(h2, w1, w2, x1, mod, norm_w)


def _gate_weights(w_g, gate_bias):
    hb = HEADS_PER_STEP
    nhb = N_HEADS // hb
    wg = w_g.reshape(D_MODEL, N_GATES, nhb, hb)
    wg = jnp.pad(wg, ((0, 0), (0, 0), (0, 0), (0, SUBLANES - hb)))
    wgt = wg.transpose(2, 1, 3, 0).reshape(nhb * GATE_ROWS, D_MODEL)
    gb = jnp.pad(gate_bias.reshape(N_GATES, nhb, hb), ((0, 0), (0, 0), (0, SUBLANES - hb)))
    gb = gb.transpose(1, 0, 2).reshape(nhb * GATE_ROWS)
    return wgt.astype(BF16), gb.astype(F32)


def _pack_head_rows(x):
    hb = HEADS_PER_STEP
    nhb = N_HEADS // hb
    x = x.reshape(x.shape[:2] + (nhb, hb) + x.shape[3:])
    x = jnp.moveaxis(x, 2, 1)
    pad = [(0, 0)] * x.ndim
    pad[3] = (0, SUBLANES - hb)
    return jnp.pad(x, pad)


def _unpack_head_rows(x):
    x = jnp.moveaxis(x[:, :, :, :HEADS_PER_STEP], 1, 2)
    return x.reshape(x.shape[:2] + (N_HEADS,) + x.shape[4:])


def _path(x, mod, mod_row, init, params, *, seg, tm, tm_mlp, emit_state):
    batch, seq_len, _ = x.shape
    wts, mlstm_nw, pool_w, pool_scale, w_out, norm_w, w1, w2 = params
    tokens = batch * seq_len
    xf = x.reshape(tokens, D_MODEL)
    q, kt, v, og, hp, stats = _inproj_call(xf, mod, mod_row, norm_w, wts, pool_w, pool_scale,
                                           batch=batch, seq_len=seq_len, tm=tm, seg=seg)
    shp = (batch, seq_len, MLSTM_WIDTH)
    res = _mlstm_call(q.reshape(shp), kt, v.reshape(shp), og.reshape(shp), stats, mlstm_nw, init,
                      batch=batch, seq_len=seq_len, emit_state=emit_state)
    hm = res[0].reshape(tokens, MLSTM_WIDTH)

    def mod_map_for(t):
        if mod_row is None:
            per = seq_len // t
            return lambda i: (1 + i // per, 0, 0)
        return lambda i: (mod_row, 0, 0)

    x1, h2 = _outproj_call(hm, hp, xf, mod, mod_map_for(tm), norm_w, w_out, tm=tm)
    y = _mlp_call(h2, x1, mod, mod_map_for(tm_mlp), norm_w, w1, w2, tm=tm_mlp, tf=1024)
    return y.reshape(batch, seq_len, D_MODEL), res[1:]


def kernel(x_prompt, x_sample, c, state_C, state_n, state_m, c_ctx, w_in, gate_bias, mlstm_norm_w,
           pool_w, pool_scale, w_out, ada_w, ada_b, norm_w, w1, w2):
    m4 = 4 * MLSTM_WIDTH
    yp, ys = x_prompt, x_sample
    new_c, new_n, new_m = [], [], []
    for layer in range(w_in.shape[0]):
        wl = w_in[layer]
        wgt, gb = _gate_weights(wl[:, m4:m4 + N_GATES * N_HEADS], gate_bias[layer])
        qkvo, wkg = _wprep_call(wl, wgt)
        wts = (qkvo, wkg, wl[:, m4 + N_GATES * N_HEADS:].astype(BF16), gb)
        params = (wts, mlstm_norm_w[layer].reshape(1, MLSTM_WIDTH), pool_w[layer].astype(BF16),
                  pool_scale[layer], w_out[layer].astype(BF16), norm_w[layer],
                  w1[layer].astype(BF16), w2[layer].astype(BF16))

        n_lat = c.shape[0]
        rows = 16
        cond = jnp.concatenate([c_ctx[None, :], c, jnp.zeros((rows - 1 - n_lat, D_MODEL), F32)], axis=0)
        mod = _mod_call(cond, ada_w[layer], ada_b[layer]).reshape(rows, N_MOD, D_MODEL)

        yp, st = _path(yp, mod, 0, None, params, seg=yp.shape[1], tm=512, tm_mlp=512, emit_state=True)
        cc, nn, mm = st
        new_c.append(cc)
        new_n.append(_unpack_head_rows(nn))
        new_m.append(_unpack_head_rows(mm)[..., 0])

        sc = state_C[:, layer].astype(F32)
        sn = _pack_head_rows(state_n[:, layer].astype(F32))
        sm = _pack_head_rows(jnp.broadcast_to(state_m[:, layer].astype(F32)[..., None],
                                              (n_lat, 2, N_HEADS, LANES)))
        ys, _ = _path(ys, mod, None, (sc, sn, sm), params, seg=GRID_W, tm=512, tm_mlp=512,
                      emit_state=False)
    return (yp, ys, jnp.stack(new_c, axis=1), jnp.stack(new_n, axis=1), jnp.stack(new_m, axis=1))
```

```python
import functools

import jax
import jax.numpy as jnp
from jax import lax
from jax.experimental import pallas as pl
from jax.experimental.pallas import tpu as pltpu

F32 = jnp.float32
BF16 = jnp.bfloat16

D_MODEL = 2048
MLSTM_WIDTH = 1024
N_HEADS = 8
HEAD_DIM = 128
POOL_WIDTH = 1024
POOL_WINDOWS = (2, 4, 8, 16)
POOL_GROUP_DIM = 256
N_GATES = 4
D_FF = 8192
CHUNK = 128
N_MOD = 6
GRID_W = 64
EPS = 1e-6

LANES = 128
SUBLANES = 8
VMEM_LIMIT = 56 * 1024 * 1024

HEADS_PER_STEP = 4
GATE_ROWS = N_GATES * SUBLANES
GATE_STATS = 5
STAT_ROWS = 2 * GATE_STATS * SUBLANES
FINALIZE_UNROLL = 4


def _cparams(sem):
    return pltpu.CompilerParams(dimension_semantics=sem, vmem_limit_bytes=VMEM_LIMIT)


def _resident(shape):
    nd = len(shape)
    return pl.BlockSpec(shape, lambda *_: (0,) * nd, pipeline_mode=pl.Buffered(1))


def _rms(x, w):
    ms = jnp.mean(x * x, axis=-1, keepdims=True)
    return x * lax.rsqrt(ms + EPS) * w


def _mod_kernel(cond_ref, w_ref, b_ref, out_ref):
    c = cond_ref[...]
    s = c * jax.nn.sigmoid(c)
    out_ref[...] = jnp.dot(s.astype(BF16), w_ref[...].astype(BF16),
                           preferred_element_type=F32) + b_ref[...]


def _mod_call(cond, ada_w, ada_b):
    rows = cond.shape[0]
    n = ada_w.shape[1]
    tn = 1536
    return pl.pallas_call(
        _mod_kernel,
        grid=(n // tn,),
        in_specs=[pl.BlockSpec((rows, D_MODEL), lambda j: (0, 0)),
                  pl.BlockSpec((D_MODEL, tn), lambda j: (0, j)),
                  pl.BlockSpec((1, tn), lambda j: (0, j))],
        out_specs=pl.BlockSpec((rows, tn), lambda j: (0, j)),
        out_shape=jax.ShapeDtypeStruct((rows, n), F32),
        compiler_params=_cparams(("arbitrary",)),
        name="mod",
    )(cond, ada_w, ada_b.reshape(1, n))


NORM_SLICE_ROWS = 64


def _lane_scan_pair(fwd, bwd, op, ident):
    lane = lax.broadcasted_iota(jnp.int32, fwd.shape, 1)
    k = 1
    while k < LANES:
        f_shift = pltpu.roll(fwd, k, axis=1)
        b_shift = pltpu.roll(bwd, LANES - k, axis=1)
        fwd = op(fwd, jnp.where(lane >= k, f_shift, ident))
        bwd = op(bwd, jnp.where(lane < LANES - k, b_shift, ident))
        k *= 2
    return fwd, bwd


def _inproj_kernel(x_ref, mod_ref, nw_ref, wq_ref, wkg_ref, wv_ref, wo_ref, wu_ref, gb_ref, pw_ref, ps_ref,
                   q_ref, kt_ref, v_ref, og_ref, hp_ref, g_ref, *, seqs, chunks, seg):
    x = x_ref[...]
    mod = mod_ref[0]
    tm = x.shape[0]
    h = (_rms(x, nw_ref[0:1, :]) * (1.0 + mod[1:2, :]) + mod[0:1, :]).astype(BF16)
    ktg = lax.dot_general(wkg_ref[...], h, (((1,), (1,)), ((), ())), preferred_element_type=F32)
    kt = ktg[0:MLSTM_WIDTH].astype(BF16)
    gt = ktg[MLSTM_WIDTH:] + gb_ref[...]
    nhb = gt.shape[0] // GATE_ROWS
    for s in range(seqs):
        for c in range(chunks):
            lo = (s * chunks + c) * CHUNK
            kt_ref[s, c] = kt[:, lo:lo + CHUNK]
    u = jnp.dot(h, wu_ref[...], preferred_element_type=F32)
    pos = lax.broadcasted_iota(jnp.int32, (tm, 1), 0) % seg

    def pool_group(gi):
        half = POOL_WINDOWS[gi] // 2
        inv_cnt = 1.0 / (jnp.minimum(pos + half, seg) - jnp.maximum(pos - half, 0)).astype(F32)
        cs = slice(gi * POOL_GROUP_DIM, (gi + 1) * POOL_GROUP_DIM)
        p = jnp.concatenate(
            [_pool_minus_self(u[:, lo:lo + LANES], half, pos, seg, inv_cnt).astype(BF16)
             for lo in range(cs.start, cs.stop, LANES)], axis=1)
        y = jnp.dot(p, pw_ref[gi], preferred_element_type=F32)
        hp_ref[:, cs] = (y * ps_ref[0:1, cs]).astype(BF16)

    q = jnp.dot(h, wq_ref[...], preferred_element_type=F32) * (HEAD_DIM ** -0.5)
    q_ref[...] = q.astype(BF16)
    pool_group(0)
    pool_group(1)
    v_ref[...] = jnp.dot(h, wv_ref[...], preferred_element_type=F32).astype(BF16)
    pool_group(2)
    og_ref[...] = jax.nn.sigmoid(jnp.dot(h, wo_ref[...], preferred_element_type=F32))
    pool_group(3)

    groups = [(s, c, hbk) for s in range(seqs) for c in range(chunks) for hbk in range(nhb)]

    def gate(g):
        return jnp.concatenate(
            [gt[hbk * GATE_ROWS + g * SUBLANES:hbk * GATE_ROWS + (g + 1) * SUBLANES,
                (s * chunks + c) * CHUNK:(s * chunks + c + 1) * CHUNK] for s, c, hbk in groups], axis=0)

    li = (gate(0), gate(2))
    lf = (jax.nn.log_sigmoid(gate(1)), jax.nn.log_sigmoid(gate(3)))
    b = _lane_scan_pair(lf[0], lf[1], jnp.add, 0.0)
    a = (li[0] - b[0], li[1] - b[1])
    pm = _lane_scan_pair(a[0], a[1], jnp.maximum, jnp.float32(-jnp.inf))
    stats = []
    for d in range(2):
        amax = jnp.broadcast_to(jnp.max(a[d], axis=1, keepdims=True), a[d].shape)
        bsum = jnp.broadcast_to(jnp.sum(lf[d], axis=1, keepdims=True), a[d].shape)
        stats += [a[d], b[d], pm[d], amax, bsum]
    for gi, (s, c, hbk) in enumerate(groups):
        rs = slice(gi * SUBLANES, (gi + 1) * SUBLANES)
        g_ref[s, c, hbk * STAT_ROWS:(hbk + 1) * STAT_ROWS, :] = jnp.concatenate(
            [st[rs] for st in stats], axis=0)


def _inproj_call(x, mod, mod_row, norm_w, wts, pool_w, pool_scale, *, batch, seq_len, tm, seg):
    wq, wkg, wv, wo, wu, gb = wts
    tokens = batch * seq_len
    nc = seq_len // CHUNK
    grows = wkg.shape[0] - MLSTM_WIDTH
    srows = grows // GATE_ROWS * STAT_ROWS
    if tm >= seq_len:
        seqs, chunks = tm // seq_len, nc
        kt_map = lambda i: (i, 0, 0, 0)
    else:
        seqs, chunks = 1, tm // CHUNK
        per = seq_len // tm
        kt_map = lambda i: (i // per, i % per, 0, 0)
    tiles_per_seq = max(seq_len // tm, 1)
    if mod_row is None:
        mod_map = lambda i: (1 + i // tiles_per_seq, 0, 0)
    else:
        mod_map = lambda i: (mod_row, 0, 0)
    tok = lambda i: (i, 0)
    gbb = jnp.broadcast_to(gb[:, None], (grows, tm))
    kern = functools.partial(_inproj_kernel, seqs=seqs, chunks=chunks, seg=seg)
    return pl.pallas_call(
        kern,
        grid=(tokens // tm,),
        in_specs=[pl.BlockSpec((tm, D_MODEL), tok),
                  pl.BlockSpec((1, N_MOD, D_MODEL), mod_map),
                  _resident((4, D_MODEL)),
                  _resident(wq.shape), _resident(wkg.shape), _resident(wv.shape),
                  _resident(wo.shape), _resident(wu.shape),
                  _resident((grows, tm)),
                  _resident(pool_w.shape),
                  _resident((1, POOL_WIDTH))],
        out_specs=[pl.BlockSpec((tm, MLSTM_WIDTH), tok),
                   pl.BlockSpec((seqs, chunks, MLSTM_WIDTH, CHUNK), kt_map),
                   pl.BlockSpec((tm, MLSTM_WIDTH), tok),
                   pl.BlockSpec((tm, MLSTM_WIDTH), tok),
                   pl.BlockSpec((tm, POOL_WIDTH), tok),
                   pl.BlockSpec((seqs, chunks, srows, CHUNK), kt_map)],
        out_shape=[jax.ShapeDtypeStruct((tokens, MLSTM_WIDTH), BF16),
                   jax.ShapeDtypeStruct((batch, nc, MLSTM_WIDTH, CHUNK), BF16),
                   jax.ShapeDtypeStruct((tokens, MLSTM_WIDTH), BF16),
                   jax.ShapeDtypeStruct((tokens, MLSTM_WIDTH), F32),
                   jax.ShapeDtypeStruct((tokens, POOL_WIDTH), BF16),
                   jax.ShapeDtypeStruct((batch, nc, srows, CHUNK), F32)],
        compiler_params=_cparams(("arbitrary",)),
        name="in_proj",
    )(x, mod, norm_w, wq, wkg, wv, wo, wu, gbb, pool_w, pool_scale.reshape(1, POOL_WIDTH))


def _chunk_start(c):
    return c * CHUNK if isinstance(c, int) else pl.multiple_of(c * CHUNK, CHUNK)


def _mlstm_kernel(*refs, nc, hb, has_init, emit_state):
    it = iter(refs)
    q_ref, kt_ref, v_ref, o_ref, g_ref, nw_ref = (next(it) for _ in range(6))
    if has_init:
        c0_ref, n0_ref, m0_ref = (next(it) for _ in range(3))
    out_ref = next(it)
    if emit_state:
        cout_ref, nout_ref, mout_ref = (next(it) for _ in range(3))
    hf_sc, hb_sc, cn_sc, m_sc, xt_sc, rows_sc = (next(it) for _ in range(6))

    if has_init:
        for d in range(2):
            for j in range(hb):
                cn_sc[d * hb + j, :, 0:HEAD_DIM] = c0_ref[0, d, j]
                n_row = n0_ref[0, 0, d][j:j + 1, :]
                cn_sc[d * hb + j, :, HEAD_DIM:] = jnp.broadcast_to(n_row, (HEAD_DIM, LANES)).T
    else:
        cn_sc[...] = jnp.zeros_like(cn_sc)

    neg_inf = jnp.float32(-jnp.inf)

    cols = []
    for d in range(2):
        def stat(k):
            lo = (d * GATE_STATS + k) * SUBLANES
            return g_ref[0, :, lo:lo + SUBLANES, :].reshape(nc * SUBLANES, LANES)
        a, b, pm, amax, bsum = (stat(k) for k in range(GATE_STATS))
        m = m0_ref[0, 0, d] if has_init else jnp.zeros((SUBLANES, LANES), F32)
        m_prev = [None] * nc
        for ci in (range(nc) if d == 0 else range(nc - 1, -1, -1)):
            rs = slice(ci * SUBLANES, (ci + 1) * SUBLANES)
            m_prev[ci] = m
            m = bsum[rs] + jnp.maximum(m, amax[rs])
        m_sc[d] = m
        mp = jnp.concatenate(m_prev, axis=0)
        mu = jnp.maximum(mp, pm)
        mu_last = jnp.maximum(mp, amax)
        rows_sc[0, d] = a.reshape(nc, SUBLANES, LANES)
        rows_sc[1, d] = jnp.exp(a - mu_last).reshape(nc, SUBLANES, LANES)
        rows_sc[2, d] = jnp.exp(mp - mu_last).reshape(nc, SUBLANES, LANES)
        cols.append((mu, jnp.exp(mp - mu), jnp.exp(-(b + mu))))
    zpad = jnp.zeros((LANES - 6 * SUBLANES, LANES), F32)
    for c in range(nc):
        pieces = []
        for d, ci in ((0, c), (1, nc - 1 - c)):
            pieces += [x[ci * SUBLANES:(ci + 1) * SUBLANES] for x in cols[d]]
        xt_sc[c] = jnp.concatenate(pieces + [zpad], axis=0).T

    row = lax.broadcasted_iota(jnp.int32, (CHUNK, CHUNK), 0)
    col = lax.broadcasted_iota(jnp.int32, (CHUNK, CHUNK), 1)
    causal = (col <= row, col >= row)
    ones_blk = jnp.ones((CHUNK, HEAD_DIM), BF16)

    def step(c, carry):
        chunk_of = (c, nc - 1 - c)
        xt = xt_sc[c]
        chains = []
        for d in range(2):
            ch = chunk_of[d]
            t0 = _chunk_start(ch)
            a_rows = rows_sc[0, d, ch]
            for j in range(hb):
                hs = slice(j * HEAD_DIM, (j + 1) * HEAD_DIM)
                qc = q_ref[0, pl.ds(t0, CHUNK), hs]
                base = 3 * d * SUBLANES + j
                mu_c = xt[:, base:base + 1]
                g_c = xt[:, base + SUBLANES:base + SUBLANES + 1]
                s = jnp.dot(qc, kt_ref[0, ch, hs, :], preferred_element_type=F32)
                p = s * jnp.exp(jnp.where(causal[d], a_rows[j:j + 1, :] - mu_c, neg_inf))
                gq = (qc.astype(F32) * g_c).astype(BF16)
                chains.append((d, j, ch, t0, jnp.concatenate([p.astype(BF16), gq], axis=1)))

        for d, j, ch, t0, lhs in chains:
            hs = slice(j * HEAD_DIM, (j + 1) * HEAD_DIM)
            base = 3 * d * SUBLANES + j
            en_c = xt[:, base + 2 * SUBLANES:base + 2 * SUBLANES + 1]
            v1 = jnp.concatenate([v_ref[0, pl.ds(t0, CHUNK), hs], ones_blk], axis=1)
            cn = cn_sc[d * hb + j]
            tot = jnp.dot(lhs, jnp.concatenate([v1, cn.astype(BF16)], axis=0),
                          preferred_element_type=F32)
            hval = tot[:, :HEAD_DIM] / jnp.maximum(jnp.abs(tot[:, HEAD_DIM:]), en_c)
            dst = hf_sc if d == 0 else hb_sc
            dst[pl.ds(t0, CHUNK), hs] = hval

            e_row = rows_sc[1, d, ch][j:j + 1, :]
            gs_row = rows_sc[2, d, ch][j:j + 1, :]
            wkt = (kt_ref[0, ch, hs, :].astype(F32) * e_row).astype(BF16)
            dcn = jnp.dot(wkt, v1, preferred_element_type=F32)
            gs_blk = jnp.broadcast_to(gs_row, (HEAD_DIM, LANES))
            cn_sc[d * hb + j] = jnp.concatenate([gs_blk, gs_blk], axis=1) * cn + dcn
        return carry

    if nc <= 2:
        for c in range(nc):
            step(c, 0)
    else:
        lax.fori_loop(0, nc, step, 0, unroll=2)

    def finalize(c, carry):
        t0 = _chunk_start(c)
        hsum = hf_sc[pl.ds(t0, CHUNK), :] + hb_sc[pl.ds(t0, CHUNK), :]
        og = o_ref[0, pl.ds(t0, CHUNK), :]
        for j in range(hb):
            hs = slice(j * HEAD_DIM, (j + 1) * HEAD_DIM)
            hn = _rms(hsum[:, hs], nw_ref[0:1, hs])
            out_ref[0, pl.ds(t0, CHUNK), hs] = (hn * og[:, hs]).astype(BF16)
        return carry

    if nc <= 2:
        for c in range(nc):
            finalize(c, 0)
    else:
        lax.fori_loop(0, nc, finalize, 0, unroll=FINALIZE_UNROLL)

    if emit_state:
        pad_rows = jnp.zeros((SUBLANES - hb, LANES), F32)
        for d in range(2):
            n_rows = []
            for j in range(hb):
                cout_ref[0, d, j] = cn_sc[d * hb + j, :, 0:HEAD_DIM]
                n_rows.append(cn_sc[d * hb + j, :, HEAD_DIM:].T[0:1, :])
            nout_ref[0, 0, d] = jnp.concatenate(n_rows + ([pad_rows] if hb < SUBLANES else []), axis=0)
            mout_ref[0, 0, d] = m_sc[d]


def _mlstm_call(q, kt, v, o, stats, nw, init, *, batch, seq_len, emit_state):
    hb = HEADS_PER_STEP
    nhb = N_HEADS // hb
    nc = seq_len // CHUNK
    w = hb * HEAD_DIM
    seq_map = lambda b, k: (b, 0, k)
    in_specs = [pl.BlockSpec((1, seq_len, w), seq_map),
                pl.BlockSpec((1, nc, w, CHUNK), lambda b, k: (b, 0, k, 0)),
                pl.BlockSpec((1, seq_len, w), seq_map),
                pl.BlockSpec((1, seq_len, w), seq_map),
                pl.BlockSpec((1, nc, STAT_ROWS, CHUNK), lambda b, k: (b, 0, k, 0)),
                pl.BlockSpec((1, w), lambda b, k: (0, k))]
    args = [q, kt, v, o, stats, nw]
    state_map = lambda b, k: (b, 0, k, 0, 0)
    vec_map = lambda b, k: (b, k, 0, 0, 0)
    vec_spec = pl.BlockSpec((1, 1, 2, SUBLANES, LANES), vec_map)
    if init is not None:
        c0, n0, m0 = init
        in_specs += [pl.BlockSpec((1, 2, hb, HEAD_DIM, HEAD_DIM), state_map), vec_spec, vec_spec]
        args += [c0, n0, m0]
    out_specs = [pl.BlockSpec((1, seq_len, w), seq_map)]
    out_shape = [jax.ShapeDtypeStruct((batch, seq_len, MLSTM_WIDTH), BF16)]
    if emit_state:
        out_specs += [pl.BlockSpec((1, 2, hb, HEAD_DIM, HEAD_DIM), state_map), vec_spec, vec_spec]
        out_shape += [jax.ShapeDtypeStruct((batch, 2, N_HEADS, HEAD_DIM, HEAD_DIM), F32),
                      jax.ShapeDtypeStruct((batch, nhb, 2, SUBLANES, LANES), F32),
                      jax.ShapeDtypeStruct((batch, nhb, 2, SUBLANES, LANES), F32)]
    kern = functools.partial(_mlstm_kernel, nc=nc, hb=hb, has_init=init is not None,
                             emit_state=emit_state)
    return pl.pallas_call(
        kern,
        grid=(batch, nhb),
        in_specs=in_specs,
        out_specs=out_specs,
        out_shape=out_shape,
        scratch_shapes=[pltpu.VMEM((seq_len, w), F32),
                        pltpu.VMEM((seq_len, w), F32),
                        pltpu.VMEM((2 * hb, HEAD_DIM, 2 * HEAD_DIM), F32),
                        pltpu.VMEM((2, SUBLANES, LANES), F32),
                        pltpu.VMEM((nc, LANES, LANES), F32),
                        pltpu.VMEM((3, 2, nc, SUBLANES, LANES), F32)],
        compiler_params=_cparams(("arbitrary", "arbitrary")),
        name="mlstm",
    )(*args)


def _shift_rows(x, dlt):
    return pltpu.roll(x, (-dlt) % x.shape[0], axis=0)


def _pool_minus_self(x, half, pos, seg, inv_cnt):
    fwd = x
    k = 1
    while k < half:
        fwd = fwd + jnp.where(pos + k < seg, _shift_rows(fwd, k), 0.0)
        k *= 2
    bwd = jnp.where(pos >= 1, _shift_rows(x, -1), 0.0)
    k = 1
    while k < half:
        bwd = bwd + jnp.where(pos >= k, _shift_rows(bwd, -k), 0.0)
        k *= 2
    return (fwd + bwd) * inv_cnt - x


OUTPROJ_ROW_BLOCKS = 4
OUTPROJ_EPI_ROWS = 64
OUTPROJ_SLICE_COLS = 256


def _outproj_kernel(*refs, cast_mlp_weights):
    hm_ref, hp_ref, x_ref, mod_ref, nw_ref, wout_ref = refs[:6]
    if cast_mlp_weights:
        w1_ref, w2_ref, x1_ref, h2_ref, w1b_ref, w2b_ref, mix_ref = refs[6:]
        w1b_ref[...] = w1_ref[...].astype(BF16)
        w2b_ref[...] = w2_ref[...].astype(BF16)
    else:
        x1_ref, h2_ref, mix_ref = refs[6:]
    tm = x_ref.shape[0]
    nblk = OUTPROJ_ROW_BLOCKS
    rb = tm // nblk
    nslice = D_MODEL // OUTPROJ_SLICE_COLS
    mod = mod_ref[0]
    gain1 = nw_ref[1:2, :] * mod[2:3, :]
    gain2 = nw_ref[2:3, :] * (1.0 + mod[4:5, :])
    shift2 = mod[3:4, :]

    def epilogue_slices(r):
        def one(lo):
            rows = slice(lo, lo + OUTPROJ_EPI_ROWS)
            mix = mix_ref[rows, :]
            ms = jnp.mean(mix * mix, axis=-1, keepdims=True)
            x1 = x_ref[rows, :] + (mix * lax.rsqrt(ms + EPS)) * gain1
            x1_ref[rows, :] = x1
            ms2 = jnp.mean(x1 * x1, axis=-1, keepdims=True)
            h2_ref[rows, :] = ((x1 * lax.rsqrt(ms2 + EPS)) * gain2 + shift2).astype(BF16)
        return [functools.partial(one, r * rb + k * OUTPROJ_EPI_ROWS)
                for k in range(rb // OUTPROJ_EPI_ROWS)]

    def run_block(r, pending):
        rows = slice(r * rb, (r + 1) * rb)
        lhs = jnp.concatenate([hm_ref[rows, :], hp_ref[rows, :]], axis=1)
        for idx in range(nslice):
            ns = slice(idx * OUTPROJ_SLICE_COLS, (idx + 1) * OUTPROJ_SLICE_COLS)
            mix_ref[rows, ns] = jnp.dot(lhs, wout_ref[:, ns], preferred_element_type=F32)
            if pending and idx % 4 == 3:
                pending.pop(0)()
        while pending:
            pending.pop(0)()

    pending = []
    for r in range(nblk):
        run_block(r, pending)
        pending = epilogue_slices(r)
    for fn in pending:
        fn()


def _outproj_call(hm, hp, x, mod, mod_map, norm_w, w_out, mlp_w=None, *, tm):
    tokens = x.shape[0]
    steps = tokens // tm
    tok = lambda i: (i, 0)
    in_specs = [pl.BlockSpec((tm, MLSTM_WIDTH), tok),
                pl.BlockSpec((tm, POOL_WIDTH), tok),
                pl.BlockSpec((tm, D_MODEL), tok),
                pl.BlockSpec((1, N_MOD, D_MODEL), mod_map),
                _resident((4, D_MODEL)),
                _resident(w_out.shape)]
    out_specs = [pl.BlockSpec((tm, D_MODEL), tok),
                 pl.BlockSpec((tm, D_MODEL), tok)]
    out_shape = [jax.ShapeDtypeStruct((tokens, D_MODEL), F32),
                 jax.ShapeDtypeStruct((tokens, D_MODEL), BF16)]
    args = [hm, hp, x, mod, norm_w, w_out]
    if mlp_w is not None:
        slab = D_FF // steps
        assert slab * steps == D_FF and slab % LANES == 0
        slab_specs = [pl.BlockSpec((D_MODEL, slab), lambda i: (0, i)),
                      pl.BlockSpec((slab, D_MODEL), lambda i: (i, 0))]
        in_specs += slab_specs
        out_specs += slab_specs
        out_shape += [jax.ShapeDtypeStruct((D_MODEL, D_FF), BF16),
                      jax.ShapeDtypeStruct((D_FF, D_MODEL), BF16)]
        args += list(mlp_w)
    return pl.pallas_call(
        functools.partial(_outproj_kernel, cast_mlp_weights=mlp_w is not None),
        grid=(steps,),
        in_specs=in_specs,
        out_specs=out_specs,
        out_shape=out_shape,
        scratch_shapes=[pltpu.VMEM((tm, D_MODEL), F32)],
        compiler_params=_cparams(("arbitrary",)),
        name="out_proj",
    )(*args)


MLP_LAST_ROW_BLOCKS = 4
MLP_SLICE_COLS = 256


def _mlp_kernel(h_ref, w1_ref, w2_ref, x1_ref, mod_ref, nw_ref, out_ref, acc_ref):
    k = pl.program_id(1)
    last = pl.num_programs(1) - 1
    tm, tf = h_ref.shape[0], w1_ref.shape[1]

    def hidden(rows, cols=slice(None)):
        a = jnp.dot(h_ref[rows, :], w1_ref[:, cols], preferred_element_type=F32)
        return jnp.square(jnp.maximum(a, 0.0)).astype(BF16)

    @pl.when(k == 0)
    def _():
        acc_ref[...] = jnp.dot(hidden(slice(None)), w2_ref[...], preferred_element_type=F32)

    @pl.when((k > 0) & (k < last))
    def _():
        acc_ref[...] += jnp.dot(hidden(slice(None)), w2_ref[...], preferred_element_type=F32)

    @pl.when(k == last)
    def _():
        rb = tm // MLP_LAST_ROW_BLOCKS
        gain = nw_ref[3:4, :] * mod_ref[0, 5:6, :]

        def epilogue_slices(r):
            def one(lo):
                rows = slice(lo, lo + NORM_SLICE_ROWS)
                y = acc_ref[rows, :]
                ms = jnp.mean(y * y, axis=-1, keepdims=True)
                out_ref[rows, :] = x1_ref[rows, :] + (y * lax.rsqrt(ms + EPS)) * gain
            return [functools.partial(one, r * rb + j * NORM_SLICE_ROWS)
                    for j in range(rb // NORM_SLICE_ROWS)]

        pending = []
        n1, n2 = tf // MLP_SLICE_COLS, D_MODEL // MLP_SLICE_COLS
        for r in range(MLP_LAST_ROW_BLOCKS):
            rows = slice(r * rb, (r + 1) * rb)
            parts = []
            for idx in range(n1):
                parts.append(hidden(rows, slice(idx * MLP_SLICE_COLS, (idx + 1) * MLP_SLICE_COLS)))
                if pending and idx % 2 == 1:
                    pending.pop(0)()
            a = jnp.concatenate(parts, axis=1)
            for idx in range(n2):
                ns = slice(idx * MLP_SLICE_COLS, (idx + 1) * MLP_SLICE_COLS)
                acc_ref[rows, ns] += jnp.dot(a, w2_ref[:, ns], preferred_element_type=F32)
                if pending and idx % 2 == 1:
                    pending.pop(0)()
            while pending:
                pending.pop(0)()
            pending = epilogue_slices(r)
        for fn in pending:
            fn()


def _mlp_call(h2, x1, mod, mod_map, norm_w, w1, w2, *, tm, tf):
    tokens = x1.shape[0]
    tok = lambda i, k: (i, 0)
    return pl.pallas_call(
        _mlp_kernel,
        grid=(tokens // tm, D_FF // tf),
        in_specs=[pl.BlockSpec((tm, D_MODEL), tok),
                  pl.BlockSpec((D_MODEL, tf), lambda i, k: (0, k)),
                  pl.BlockSpec((tf, D_MODEL), lambda i, k: (k, 0)),
                  pl.BlockSpec((tm, D_MODEL), tok),
                  pl.BlockSpec((1, N_MOD, D_MODEL), lambda i, k: mod_map(i)),
                  pl.BlockSpec((4, D_MODEL), lambda i, k: (0, 0))],
        out_specs=pl.BlockSpec((tm, D_MODEL), tok),
        out_shape=jax.ShapeDtypeStruct((tokens, D_MODEL), F32),
        scratch_shapes=[pltpu.VMEM((tm, D_MODEL), F32)],
        compiler_params=_cparams(("arbitrary", "arbitrary")),
        name="mlp",
    )(h2, w1, w2, x1, mod, norm_w)


def _gate_weights(w_g, gate_bias):
    hb = HEADS_PER_STEP
    nhb = N_HEADS // hb
    wg = w_g.reshape(D_MODEL, N_GATES, nhb, hb)
    wg = jnp.pad(wg, ((0, 0), (0, 0), (0, 0), (0, SUBLANES - hb)))
    wgt = wg.transpose(2, 1, 3, 0).reshape(nhb * GATE_ROWS, D_MODEL)
    gb = jnp.pad(gate_bias.reshape(N_GATES, nhb, hb), ((0, 0), (0, 0), (0, SUBLANES - hb)))
    gb = gb.transpose(1, 0, 2).reshape(nhb * GATE_ROWS)
    return wgt.astype(BF16), gb.astype(F32)


def _pack_head_rows(x):
    hb = HEADS_PER_STEP
    nhb = N_HEADS // hb
    x = x.reshape(x.shape[:2] + (nhb, hb) + x.shape[3:])
    x = jnp.moveaxis(x, 2, 1)
    pad = [(0, 0)] * x.ndim
    pad[3] = (0, SUBLANES - hb)
    return jnp.pad(x, pad)


def _unpack_head_rows(x):
    x = jnp.moveaxis(x[:, :, :, :HEADS_PER_STEP], 1, 2)
    return x.reshape(x.shape[:2] + (N_HEADS,) + x.shape[4:])


def _mod_map(mod_row, seq_len, tm):
    if mod_row is None:
        per = seq_len // tm
        return lambda i: (1 + i // per, 0, 0)
    return lambda i: (mod_row, 0, 0)


def _mixer_stage(x, mod, mod_row, init, params, mlp_w=None, *, seg, tm, emit_state):
    batch, seq_len, _ = x.shape
    wts, mlstm_nw, pool_w, pool_scale, w_out, norm_w = params
    tokens = batch * seq_len
    xf = x.reshape(tokens, D_MODEL)
    q, kt, v, og, hp, stats = _inproj_call(xf, mod, mod_row, norm_w, wts, pool_w, pool_scale,
                                           batch=batch, seq_len=seq_len, tm=tm, seg=seg)
    shp = (batch, seq_len, MLSTM_WIDTH)
    res = _mlstm_call(q.reshape(shp), kt, v.reshape(shp), og.reshape(shp), stats, mlstm_nw, init,
                      batch=batch, seq_len=seq_len, emit_state=emit_state)
    hm = res[0].reshape(tokens, MLSTM_WIDTH)
    outs = _outproj_call(hm, hp, xf, mod, _mod_map(mod_row, seq_len, tm), norm_w, w_out, mlp_w, tm=tm)
    return outs, res[1:]


def kernel(x_prompt, x_sample, c, state_C, state_n, state_m, c_ctx, w_in, gate_bias, mlstm_norm_w,
           pool_w, pool_scale, w_out, ada_w, ada_b, norm_w, w1, w2):
    m4 = 4 * MLSTM_WIDTH
    yp, ys = x_prompt, x_sample
    new_c, new_n, new_m = [], [], []
    for layer in range(w_in.shape[0]):
        wl = w_in[layer]
        wgt, gb = _gate_weights(wl[:, m4:m4 + N_GATES * N_HEADS], gate_bias[layer])
        wkg = jnp.concatenate([wl[:, MLSTM_WIDTH:2 * MLSTM_WIDTH].T.astype(BF16), wgt], axis=0)
        wts = (wl[:, 0:MLSTM_WIDTH].astype(BF16),
               wkg,
               wl[:, 2 * MLSTM_WIDTH:3 * MLSTM_WIDTH].astype(BF16),
               wl[:, 3 * MLSTM_WIDTH:m4].astype(BF16),
               wl[:, m4 + N_GATES * N_HEADS:].astype(BF16),
               gb)
        params = (wts, mlstm_norm_w[layer].reshape(1, MLSTM_WIDTH), pool_w[layer].astype(BF16),
                  pool_scale[layer], w_out[layer].astype(BF16), norm_w[layer])

        n_lat = c.shape[0]
        rows = 16
        cond = jnp.concatenate([c_ctx[None, :], c, jnp.zeros((rows - 1 - n_lat, D_MODEL), F32)], axis=0)
        mod = _mod_call(cond, ada_w[layer], ada_b[layer]).reshape(rows, N_MOD, D_MODEL)

        tm = 512
        (x1p, h2p), (cc, nn, mm) = _mixer_stage(yp, mod, 0, None, params, seg=yp.shape[1], tm=tm,
                                                emit_state=True)
        new_c.append(cc)
        new_n.append(_unpack_head_rows(nn))
        new_m.append(_unpack_head_rows(mm)[..., 0])

        sc = state_C[:, layer].astype(F32)
        sn = _pack_head_rows(state_n[:, layer].astype(F32))
        sm = _pack_head_rows(jnp.broadcast_to(state_m[:, layer].astype(F32)[..., None],
                                              (n_lat, 2, N_HEADS, LANES)))
        (x1s, h2s, w1b, w2b), _ = _mixer_stage(ys, mod, None, (sc, sn, sm), params,
                                               (w1[layer], w2[layer]), seg=GRID_W, tm=tm,
                                               emit_state=False)
        yp = _mlp_call(h2p, x1p, mod, _mod_map(0, yp.shape[1], tm), norm_w[layer], w1b, w2b,
                       tm=tm, tf=1024).reshape(yp.shape)
        ys = _mlp_call(h2s, x1s, mod, _mod_map(None, ys.shape[1], tm), norm_w[layer], w1b, w2b,
                       tm=tm, tf=1024).reshape(ys.shape)
    return (yp, ys, jnp.stack(new_c, axis=1), jnp.stack(new_n, axis=1), jnp.stack(new_m, axis=1))
```

```python
import functools

import jax
import jax.numpy as jnp
from jax import lax
from jax.experimental import pallas as pl
from jax.experimental.pallas import tpu as pltpu

F32 = jnp.float32
BF16 = jnp.bfloat16

D_MODEL = 2048
MLSTM_WIDTH = 1024
N_HEADS = 8
HEAD_DIM = 128
POOL_WIDTH = 1024
POOL_WINDOWS = (2, 4, 8, 16)
POOL_GROUP_DIM = 256
N_GATES = 4
D_FF = 8192
CHUNK = 128
N_MOD = 6
GRID_W = 64
EPS = 1e-6

LANES = 128
SUBLANES = 8
VMEM_LIMIT = 56 * 1024 * 1024

HEADS_PER_STEP = 4
GATE_ROWS = N_GATES * SUBLANES
GATE_STATS = 5
STAT_ROWS = 2 * GATE_STATS * SUBLANES
FINALIZE_UNROLL = 4


def _cparams(sem):
    return pltpu.CompilerParams(dimension_semantics=sem, vmem_limit_bytes=VMEM_LIMIT)


def _resident(shape):
    nd = len(shape)
    return pl.BlockSpec(shape, lambda *_: (0,) * nd, pipeline_mode=pl.Buffered(1))


def _rms(x, w):
    ms = jnp.mean(x * x, axis=-1, keepdims=True)
    return x * lax.rsqrt(ms + EPS) * w


def _mod_kernel(cond_ref, w_ref, b_ref, out_ref):
    c = cond_ref[...]
    s = c * jax.nn.sigmoid(c)
    out_ref[...] = jnp.dot(s.astype(BF16), w_ref[...].astype(BF16),
                           preferred_element_type=F32) + b_ref[...]


def _mod_call(cond, ada_w, ada_b):
    rows = cond.shape[0]
    n = ada_w.shape[1]
    tn = 1536
    return pl.pallas_call(
        _mod_kernel,
        grid=(n // tn,),
        in_specs=[pl.BlockSpec((rows, D_MODEL), lambda j: (0, 0)),
                  pl.BlockSpec((D_MODEL, tn), lambda j: (0, j)),
                  pl.BlockSpec((1, tn), lambda j: (0, j))],
        out_specs=pl.BlockSpec((rows, tn), lambda j: (0, j)),
        out_shape=jax.ShapeDtypeStruct((rows, n), F32),
        compiler_params=_cparams(("arbitrary",)),
        name="mod",
    )(cond, ada_w, ada_b.reshape(1, n))


NORM_SLICE_ROWS = 64


def _lane_scan_pair(fwd, bwd, op, ident):
    lane = lax.broadcasted_iota(jnp.int32, fwd.shape, 1)
    k = 1
    while k < LANES:
        f_shift = pltpu.roll(fwd, k, axis=1)
        b_shift = pltpu.roll(bwd, LANES - k, axis=1)
        fwd = op(fwd, jnp.where(lane >= k, f_shift, ident))
        bwd = op(bwd, jnp.where(lane < LANES - k, b_shift, ident))
        k *= 2
    return fwd, bwd


def _inproj_kernel(x_ref, mod_ref, nw_ref, wq_ref, wkg_ref, wv_ref, wo_ref, wu_ref, gb_ref, pw_ref, ps_ref,
                   q_ref, kt_ref, v_ref, og_ref, hp_ref, g_ref, *, seqs, chunks, seg):
    x = x_ref[...]
    mod = mod_ref[0]
    tm = x.shape[0]
    h = (_rms(x, nw_ref[0:1, :]) * (1.0 + mod[1:2, :]) + mod[0:1, :]).astype(BF16)
    ktg = lax.dot_general(wkg_ref[...], h, (((1,), (1,)), ((), ())), preferred_element_type=F32)
    kt = ktg[0:MLSTM_WIDTH].astype(BF16)
    gt = ktg[MLSTM_WIDTH:] + gb_ref[...]
    nhb = gt.shape[0] // GATE_ROWS
    for s in range(seqs):
        for c in range(chunks):
            lo = (s * chunks + c) * CHUNK
            kt_ref[s, c] = kt[:, lo:lo + CHUNK]
    u = jnp.dot(h, wu_ref[...], preferred_element_type=F32)
    pos = lax.broadcasted_iota(jnp.int32, (tm, 1), 0) % seg

    def pool_group(gi):
        half = POOL_WINDOWS[gi] // 2
        inv_cnt = 1.0 / (jnp.minimum(pos + half, seg) - jnp.maximum(pos - half, 0)).astype(F32)
        cs = slice(gi * POOL_GROUP_DIM, (gi + 1) * POOL_GROUP_DIM)
        p = jnp.concatenate(
            [_pool_minus_self(u[:, lo:lo + LANES], half, pos, seg, inv_cnt).astype(BF16)
             for lo in range(cs.start, cs.stop, LANES)], axis=1)
        y = jnp.dot(p, pw_ref[gi], preferred_element_type=F32)
        hp_ref[:, cs] = (y * ps_ref[0:1, cs]).astype(BF16)

    q = jnp.dot(h, wq_ref[...], preferred_element_type=F32) * (HEAD_DIM ** -0.5)
    q_ref[...] = q.astype(BF16)
    pool_group(0)
    pool_group(1)
    v_ref[...] = jnp.dot(h, wv_ref[...], preferred_element_type=F32).astype(BF16)
    pool_group(2)
    og_ref[...] = jax.nn.sigmoid(jnp.dot(h, wo_ref[...], preferred_element_type=F32))
    pool_group(3)

    groups = [(s, c, hbk) for s in range(seqs) for c in range(chunks) for hbk in range(nhb)]

    def gate(g):
        return jnp.concatenate(
            [gt[hbk * GATE_ROWS + g * SUBLANES:hbk * GATE_ROWS + (g + 1) * SUBLANES,
                (s * chunks + c) * CHUNK:(s * chunks + c + 1) * CHUNK] for s, c, hbk in groups], axis=0)

    li = (gate(0), gate(2))
    lf = (jax.nn.log_sigmoid(gate(1)), jax.nn.log_sigmoid(gate(3)))
    b = _lane_scan_pair(lf[0], lf[1], jnp.add, 0.0)
    a = (li[0] - b[0], li[1] - b[1])
    pm = _lane_scan_pair(a[0], a[1], jnp.maximum, jnp.float32(-jnp.inf))
    stats = []
    for d in range(2):
        amax = jnp.broadcast_to(jnp.max(a[d], axis=1, keepdims=True), a[d].shape)
        bsum = jnp.broadcast_to(jnp.sum(lf[d], axis=1, keepdims=True), a[d].shape)
        stats += [a[d], b[d], pm[d], amax, bsum]
    for gi, (s, c, hbk) in enumerate(groups):
        rs = slice(gi * SUBLANES, (gi + 1) * SUBLANES)
        g_ref[s, c, hbk * STAT_ROWS:(hbk + 1) * STAT_ROWS, :] = jnp.concatenate(
            [st[rs] for st in stats], axis=0)


def _inproj_call(x, mod, mod_row, norm_w, wts, pool_w, pool_scale, *, batch, seq_len, tm, seg):
    wq, wkg, wv, wo, wu, gb = wts
    tokens = batch * seq_len
    nc = seq_len // CHUNK
    grows = wkg.shape[0] - MLSTM_WIDTH
    srows = grows // GATE_ROWS * STAT_ROWS
    if tm >= seq_len:
        seqs, chunks = tm // seq_len, nc
        kt_map = lambda i: (i, 0, 0, 0)
    else:
        seqs, chunks = 1, tm // CHUNK
        per = seq_len // tm
        kt_map = lambda i: (i // per, i % per, 0, 0)
    tiles_per_seq = max(seq_len // tm, 1)
    if mod_row is None:
        mod_map = lambda i: (1 + i // tiles_per_seq, 0, 0)
    else:
        mod_map = lambda i: (mod_row, 0, 0)
    tok = lambda i: (i, 0)
    gbb = jnp.broadcast_to(gb[:, None], (grows, tm))
    kern = functools.partial(_inproj_kernel, seqs=seqs, chunks=chunks, seg=seg)
    return pl.pallas_call(
        kern,
        grid=(tokens // tm,),
        in_specs=[pl.BlockSpec((tm, D_MODEL), tok),
                  pl.BlockSpec((1, N_MOD, D_MODEL), mod_map),
                  _resident((4, D_MODEL)),
                  _resident(wq.shape), _resident(wkg.shape), _resident(wv.shape),
                  _resident(wo.shape), _resident(wu.shape),
                  _resident((grows, tm)),
                  _resident(pool_w.shape),
                  _resident((1, POOL_WIDTH))],
        out_specs=[pl.BlockSpec((tm, MLSTM_WIDTH), tok),
                   pl.BlockSpec((seqs, chunks, MLSTM_WIDTH, CHUNK), kt_map),
                   pl.BlockSpec((tm, MLSTM_WIDTH), tok),
                   pl.BlockSpec((tm, MLSTM_WIDTH), tok),
                   pl.BlockSpec((tm, POOL_WIDTH), tok),
                   pl.BlockSpec((seqs, chunks, srows, CHUNK), kt_map)],
        out_shape=[jax.ShapeDtypeStruct((tokens, MLSTM_WIDTH), BF16),
                   jax.ShapeDtypeStruct((batch, nc, MLSTM_WIDTH, CHUNK), BF16),
                   jax.ShapeDtypeStruct((tokens, MLSTM_WIDTH), BF16),
                   jax.ShapeDtypeStruct((tokens, MLSTM_WIDTH), F32),
                   jax.ShapeDtypeStruct((tokens, POOL_WIDTH), BF16),
                   jax.ShapeDtypeStruct((batch, nc, srows, CHUNK), F32)],
        compiler_params=_cparams(("arbitrary",)),
        name="in_proj",
    )(x, mod, norm_w, wq, wkg, wv, wo, wu, gbb, pool_w, pool_scale.reshape(1, POOL_WIDTH))


def _chunk_start(c):
    return c * CHUNK if isinstance(c, int) else pl.multiple_of(c * CHUNK, CHUNK)


def _mlstm_kernel(*refs, nc, hb, has_init, emit_state, cast_weight):
    it = iter(refs)
    q_ref, kt_ref, v_ref, o_ref, g_ref, nw_ref = (next(it) for _ in range(6))
    if has_init:
        c0_ref, n0_ref, m0_ref = (next(it) for _ in range(3))
    if cast_weight:
        wf_ref = next(it)
    out_ref = next(it)
    if emit_state:
        cout_ref, nout_ref, mout_ref = (next(it) for _ in range(3))
    if cast_weight:
        next(it)[...] = wf_ref[...].astype(BF16)
    hf_sc, hb_sc, cn_sc, m_sc, xt_sc, rows_sc = (next(it) for _ in range(6))

    if has_init:
        for d in range(2):
            for j in range(hb):
                cn_sc[d * hb + j, :, 0:HEAD_DIM] = c0_ref[0, d, j]
                n_row = n0_ref[0, 0, d][j:j + 1, :]
                cn_sc[d * hb + j, :, HEAD_DIM:] = jnp.broadcast_to(n_row, (HEAD_DIM, LANES)).T
    else:
        cn_sc[...] = jnp.zeros_like(cn_sc)

    neg_inf = jnp.float32(-jnp.inf)

    cols = []
    for d in range(2):
        def stat(k):
            lo = (d * GATE_STATS + k) * SUBLANES
            return g_ref[0, :, lo:lo + SUBLANES, :].reshape(nc * SUBLANES, LANES)
        a, b, pm, amax, bsum = (stat(k) for k in range(GATE_STATS))
        m = m0_ref[0, 0, d] if has_init else jnp.zeros((SUBLANES, LANES), F32)
        m_prev = [None] * nc
        for ci in (range(nc) if d == 0 else range(nc - 1, -1, -1)):
            rs = slice(ci * SUBLANES, (ci + 1) * SUBLANES)
            m_prev[ci] = m
            m = bsum[rs] + jnp.maximum(m, amax[rs])
        m_sc[d] = m
        mp = jnp.concatenate(m_prev, axis=0)
        mu = jnp.maximum(mp, pm)
        mu_last = jnp.maximum(mp, amax)
        rows_sc[0, d] = a.reshape(nc, SUBLANES, LANES)
        rows_sc[1, d] = jnp.exp(a - mu_last).reshape(nc, SUBLANES, LANES)
        rows_sc[2, d] = jnp.exp(mp - mu_last).reshape(nc, SUBLANES, LANES)
        cols.append((mu, jnp.exp(mp - mu), jnp.exp(-(b + mu))))
    zpad = jnp.zeros((LANES - 6 * SUBLANES, LANES), F32)
    for c in range(nc):
        pieces = []
        for d, ci in ((0, c), (1, nc - 1 - c)):
            pieces += [x[ci * SUBLANES:(ci + 1) * SUBLANES] for x in cols[d]]
        xt_sc[c] = jnp.concatenate(pieces + [zpad], axis=0).T

    row = lax.broadcasted_iota(jnp.int32, (CHUNK, CHUNK), 0)
    col = lax.broadcasted_iota(jnp.int32, (CHUNK, CHUNK), 1)
    causal = (col <= row, col >= row)
    ones_blk = jnp.ones((CHUNK, HEAD_DIM), BF16)

    def step(c, carry):
        chunk_of = (c, nc - 1 - c)
        xt = xt_sc[c]
        chains = []
        for d in range(2):
            ch = chunk_of[d]
            t0 = _chunk_start(ch)
            a_rows = rows_sc[0, d, ch]
            for j in range(hb):
                hs = slice(j * HEAD_DIM, (j + 1) * HEAD_DIM)
                qc = q_ref[0, pl.ds(t0, CHUNK), hs]
                base = 3 * d * SUBLANES + j
                mu_c = xt[:, base:base + 1]
                g_c = xt[:, base + SUBLANES:base + SUBLANES + 1]
                s = jnp.dot(qc, kt_ref[0, ch, hs, :], preferred_element_type=F32)
                p = s * jnp.exp(jnp.where(causal[d], a_rows[j:j + 1, :] - mu_c, neg_inf))
                gq = (qc.astype(F32) * g_c).astype(BF16)
                chains.append((d, j, ch, t0, jnp.concatenate([p.astype(BF16), gq], axis=1)))

        for d, j, ch, t0, lhs in chains:
            hs = slice(j * HEAD_DIM, (j + 1) * HEAD_DIM)
            base = 3 * d * SUBLANES + j
            en_c = xt[:, base + 2 * SUBLANES:base + 2 * SUBLANES + 1]
            v1 = jnp.concatenate([v_ref[0, pl.ds(t0, CHUNK), hs], ones_blk], axis=1)
            cn = cn_sc[d * hb + j]
            tot = jnp.dot(lhs, jnp.concatenate([v1, cn.astype(BF16)], axis=0),
                          preferred_element_type=F32)
            hval = tot[:, :HEAD_DIM] / jnp.maximum(jnp.abs(tot[:, HEAD_DIM:]), en_c)
            dst = hf_sc if d == 0 else hb_sc
            dst[pl.ds(t0, CHUNK), hs] = hval

            e_row = rows_sc[1, d, ch][j:j + 1, :]
            gs_row = rows_sc[2, d, ch][j:j + 1, :]
            wkt = (kt_ref[0, ch, hs, :].astype(F32) * e_row).astype(BF16)
            dcn = jnp.dot(wkt, v1, preferred_element_type=F32)
            gs_blk = jnp.broadcast_to(gs_row, (HEAD_DIM, LANES))
            cn_sc[d * hb + j] = jnp.concatenate([gs_blk, gs_blk], axis=1) * cn + dcn
        return carry

    if nc <= 2:
        for c in range(nc):
            step(c, 0)
    else:
        lax.fori_loop(0, nc, step, 0, unroll=2)

    def finalize(c, carry):
        t0 = _chunk_start(c)
        hsum = hf_sc[pl.ds(t0, CHUNK), :] + hb_sc[pl.ds(t0, CHUNK), :]
        og = o_ref[0, pl.ds(t0, CHUNK), :]
        for j in range(hb):
            hs = slice(j * HEAD_DIM, (j + 1) * HEAD_DIM)
            hn = _rms(hsum[:, hs], nw_ref[0:1, hs])
            out_ref[0, pl.ds(t0, CHUNK), hs] = (hn * og[:, hs]).astype(BF16)
        return carry

    if nc <= 2:
        for c in range(nc):
            finalize(c, 0)
    else:
        lax.fori_loop(0, nc, finalize, 0, unroll=FINALIZE_UNROLL)

    if emit_state:
        pad_rows = jnp.zeros((SUBLANES - hb, LANES), F32)
        for d in range(2):
            n_rows = []
            for j in range(hb):
                cout_ref[0, d, j] = cn_sc[d * hb + j, :, 0:HEAD_DIM]
                n_rows.append(cn_sc[d * hb + j, :, HEAD_DIM:].T[0:1, :])
            nout_ref[0, 0, d] = jnp.concatenate(n_rows + ([pad_rows] if hb < SUBLANES else []), axis=0)
            mout_ref[0, 0, d] = m_sc[d]


def _mlstm_call(q, kt, v, o, stats, nw, init, cast_weight=None, *, batch, seq_len, emit_state):
    hb = HEADS_PER_STEP
    nhb = N_HEADS // hb
    nc = seq_len // CHUNK
    w = hb * HEAD_DIM
    seq_map = lambda b, k: (b, 0, k)
    in_specs = [pl.BlockSpec((1, seq_len, w), seq_map),
                pl.BlockSpec((1, nc, w, CHUNK), lambda b, k: (b, 0, k, 0)),
                pl.BlockSpec((1, seq_len, w), seq_map),
                pl.BlockSpec((1, seq_len, w), seq_map),
                pl.BlockSpec((1, nc, STAT_ROWS, CHUNK), lambda b, k: (b, 0, k, 0)),
                pl.BlockSpec((1, w), lambda b, k: (0, k))]
    args = [q, kt, v, o, stats, nw]
    state_map = lambda b, k: (b, 0, k, 0, 0)
    vec_map = lambda b, k: (b, k, 0, 0, 0)
    vec_spec = pl.BlockSpec((1, 1, 2, SUBLANES, LANES), vec_map)
    if init is not None:
        c0, n0, m0 = init
        in_specs += [pl.BlockSpec((1, 2, hb, HEAD_DIM, HEAD_DIM), state_map), vec_spec, vec_spec]
        args += [c0, n0, m0]
    out_specs = [pl.BlockSpec((1, seq_len, w), seq_map)]
    out_shape = [jax.ShapeDtypeStruct((batch, seq_len, MLSTM_WIDTH), BF16)]
    if emit_state:
        out_specs += [pl.BlockSpec((1, 2, hb, HEAD_DIM, HEAD_DIM), state_map), vec_spec, vec_spec]
        out_shape += [jax.ShapeDtypeStruct((batch, 2, N_HEADS, HEAD_DIM, HEAD_DIM), F32),
                      jax.ShapeDtypeStruct((batch, nhb, 2, SUBLANES, LANES), F32),
                      jax.ShapeDtypeStruct((batch, nhb, 2, SUBLANES, LANES), F32)]
    if cast_weight is not None:
        slab = cast_weight.shape[0] // (batch * nhb)
        assert slab * batch * nhb == cast_weight.shape[0] and slab % (2 * SUBLANES) == 0
        slab_spec = pl.BlockSpec((slab, D_MODEL), lambda b, k: (b * nhb + k, 0))
        in_specs.append(slab_spec)
        args.append(cast_weight)
        out_specs.append(slab_spec)
        out_shape.append(jax.ShapeDtypeStruct(cast_weight.shape, BF16))
    kern = functools.partial(_mlstm_kernel, nc=nc, hb=hb, has_init=init is not None,
                             emit_state=emit_state, cast_weight=cast_weight is not None)
    return pl.pallas_call(
        kern,
        grid=(batch, nhb),
        in_specs=in_specs,
        out_specs=out_specs,
        out_shape=out_shape,
        scratch_shapes=[pltpu.VMEM((seq_len, w), F32),
                        pltpu.VMEM((seq_len, w), F32),
                        pltpu.VMEM((2 * hb, HEAD_DIM, 2 * HEAD_DIM), F32),
                        pltpu.VMEM((2, SUBLANES, LANES), F32),
                        pltpu.VMEM((nc, LANES, LANES), F32),
                        pltpu.VMEM((3, 2, nc, SUBLANES, LANES), F32)],
        compiler_params=_cparams(("arbitrary", "arbitrary")),
        name="mlstm",
    )(*args)


def _shift_rows(x, dlt):
    return pltpu.roll(x, (-dlt) % x.shape[0], axis=0)


def _pool_minus_self(x, half, pos, seg, inv_cnt):
    fwd = x
    k = 1
    while k < half:
        fwd = fwd + jnp.where(pos + k < seg, _shift_rows(fwd, k), 0.0)
        k *= 2
    bwd = jnp.where(pos >= 1, _shift_rows(x, -1), 0.0)
    k = 1
    while k < half:
        bwd = bwd + jnp.where(pos >= k, _shift_rows(bwd, -k), 0.0)
        k *= 2
    return (fwd + bwd) * inv_cnt - x


OUTPROJ_ROW_BLOCKS = 4
OUTPROJ_EPI_ROWS = 64
OUTPROJ_SLICE_COLS = 256


def _outproj_kernel(*refs, cast_mlp_weights):
    hm_ref, hp_ref, x_ref, mod_ref, nw_ref, wout_ref = refs[:6]
    if cast_mlp_weights:
        w1_ref, x1_ref, h2_ref, w1b_ref, mix_ref = refs[6:]
        w1b_ref[...] = w1_ref[...].astype(BF16)
    else:
        x1_ref, h2_ref, mix_ref = refs[6:]
    tm = x_ref.shape[0]
    nblk = OUTPROJ_ROW_BLOCKS
    rb = tm // nblk
    nslice = D_MODEL // OUTPROJ_SLICE_COLS
    mod = mod_ref[0]
    gain1 = nw_ref[1:2, :] * mod[2:3, :]
    gain2 = nw_ref[2:3, :] * (1.0 + mod[4:5, :])
    shift2 = mod[3:4, :]

    def epilogue_slices(r):
        def one(lo):
            rows = slice(lo, lo + OUTPROJ_EPI_ROWS)
            mix = mix_ref[rows, :]
            ms = jnp.mean(mix * mix, axis=-1, keepdims=True)
            x1 = x_ref[rows, :] + (mix * lax.rsqrt(ms + EPS)) * gain1
            x1_ref[rows, :] = x1
            ms2 = jnp.mean(x1 * x1, axis=-1, keepdims=True)
            h2_ref[rows, :] = ((x1 * lax.rsqrt(ms2 + EPS)) * gain2 + shift2).astype(BF16)
        return [functools.partial(one, r * rb + k * OUTPROJ_EPI_ROWS)
                for k in range(rb // OUTPROJ_EPI_ROWS)]

    def run_block(r, pending):
        rows = slice(r * rb, (r + 1) * rb)
        lhs = jnp.concatenate([hm_ref[rows, :], hp_ref[rows, :]], axis=1)
        for idx in range(nslice):
            ns = slice(idx * OUTPROJ_SLICE_COLS, (idx + 1) * OUTPROJ_SLICE_COLS)
            mix_ref[rows, ns] = jnp.dot(lhs, wout_ref[:, ns], preferred_element_type=F32)
            if pending and idx % 4 == 3:
                pending.pop(0)()
        while pending:
            pending.pop(0)()

    pending = []
    for r in range(nblk):
        run_block(r, pending)
        pending = epilogue_slices(r)
    for fn in pending:
        fn()


def _outproj_call(hm, hp, x, mod, mod_map, norm_w, w_out, cast_w1=None, *, tm):
    tokens = x.shape[0]
    steps = tokens // tm
    tok = lambda i: (i, 0)
    in_specs = [pl.BlockSpec((tm, MLSTM_WIDTH), tok),
                pl.BlockSpec((tm, POOL_WIDTH), tok),
                pl.BlockSpec((tm, D_MODEL), tok),
                pl.BlockSpec((1, N_MOD, D_MODEL), mod_map),
                _resident((4, D_MODEL)),
                _resident(w_out.shape)]
    out_specs = [pl.BlockSpec((tm, D_MODEL), tok),
                 pl.BlockSpec((tm, D_MODEL), tok)]
    out_shape = [jax.ShapeDtypeStruct((tokens, D_MODEL), F32),
                 jax.ShapeDtypeStruct((tokens, D_MODEL), BF16)]
    args = [hm, hp, x, mod, norm_w, w_out]
    if cast_w1 is not None:
        slab = D_FF // steps
        assert slab * steps == D_FF and slab % LANES == 0
        slab_spec = pl.BlockSpec((D_MODEL, slab), lambda i: (0, i))
        in_specs.append(slab_spec)
        out_specs.append(slab_spec)
        out_shape.append(jax.ShapeDtypeStruct((D_MODEL, D_FF), BF16))
        args.append(cast_w1)
    return pl.pallas_call(
        functools.partial(_outproj_kernel, cast_mlp_weights=cast_w1 is not None),
        grid=(steps,),
        in_specs=in_specs,
        out_specs=out_specs,
        out_shape=out_shape,
        scratch_shapes=[pltpu.VMEM((tm, D_MODEL), F32)],
        compiler_params=_cparams(("arbitrary",)),
        name="out_proj",
    )(*args)


MLP_LAST_ROW_BLOCKS = 4
MLP_SLICE_COLS = 256


def _mlp_kernel(h_ref, w1_ref, w2_ref, x1_ref, mod_ref, nw_ref, out_ref, acc_ref):
    k = pl.program_id(1)
    last = pl.num_programs(1) - 1
    tm, tf = h_ref.shape[0], w1_ref.shape[1]

    def hidden(rows, cols=slice(None)):
        a = jnp.dot(h_ref[rows, :], w1_ref[:, cols], preferred_element_type=F32)
        return jnp.square(jnp.maximum(a, 0.0)).astype(BF16)

    @pl.when(k == 0)
    def _():
        acc_ref[...] = jnp.dot(hidden(slice(None)), w2_ref[...], preferred_element_type=F32)

    @pl.when((k > 0) & (k < last))
    def _():
        acc_ref[...] += jnp.dot(hidden(slice(None)), w2_ref[...], preferred_element_type=F32)

    @pl.when(k == last)
    def _():
        rb = tm // MLP_LAST_ROW_BLOCKS
        gain = nw_ref[3:4, :] * mod_ref[0, 5:6, :]

        def epilogue_slices(r):
            def one(lo):
                rows = slice(lo, lo + NORM_SLICE_ROWS)
                y = acc_ref[rows, :]
                ms = jnp.mean(y * y, axis=-1, keepdims=True)
                out_ref[rows, :] = x1_ref[rows, :] + (y * lax.rsqrt(ms + EPS)) * gain
            return [functools.partial(one, r * rb + j * NORM_SLICE_ROWS)
                    for j in range(rb // NORM_SLICE_ROWS)]

        pending = []
        n1, n2 = tf // MLP_SLICE_COLS, D_MODEL // MLP_SLICE_COLS
        for r in range(MLP_LAST_ROW_BLOCKS):
            rows = slice(r * rb, (r + 1) * rb)
            parts = []
            for idx in range(n1):
                parts.append(hidden(rows, slice(idx * MLP_SLICE_COLS, (idx + 1) * MLP_SLICE_COLS)))
                if pending and idx % 2 == 1:
                    pending.pop(0)()
            a = jnp.concatenate(parts, axis=1)
            for idx in range(n2):
                ns = slice(idx * MLP_SLICE_COLS, (idx + 1) * MLP_SLICE_COLS)
                acc_ref[rows, ns] += jnp.dot(a, w2_ref[:, ns], preferred_element_type=F32)
                if pending and idx % 2 == 1:
                    pending.pop(0)()
            while pending:
                pending.pop(0)()
            pending = epilogue_slices(r)
        for fn in pending:
            fn()


def _mlp_call(h2, x1, mod, mod_map, norm_w, w1, w2, *, tm, tf):
    tokens = x1.shape[0]
    tok = lambda i, k: (i, 0)
    return pl.pallas_call(
        _mlp_kernel,
        grid=(tokens // tm, D_FF // tf),
        in_specs=[pl.BlockSpec((tm, D_MODEL), tok),
                  pl.BlockSpec((D_MODEL, tf), lambda i, k: (0, k)),
                  pl.BlockSpec((tf, D_MODEL), lambda i, k: (k, 0)),
                  pl.BlockSpec((tm, D_MODEL), tok),
                  pl.BlockSpec((1, N_MOD, D_MODEL), lambda i, k: mod_map(i)),
                  pl.BlockSpec((4, D_MODEL), lambda i, k: (0, 0))],
        out_specs=pl.BlockSpec((tm, D_MODEL), tok),
        out_shape=jax.ShapeDtypeStruct((tokens, D_MODEL), F32),
        scratch_shapes=[pltpu.VMEM((tm, D_MODEL), F32)],
        compiler_params=_cparams(("arbitrary", "arbitrary")),
        name="mlp",
    )(h2, w1, w2, x1, mod, norm_w)


def _gate_weights(w_g, gate_bias):
    hb = HEADS_PER_STEP
    nhb = N_HEADS // hb
    wg = w_g.reshape(D_MODEL, N_GATES, nhb, hb)
    wg = jnp.pad(wg, ((0, 0), (0, 0), (0, 0), (0, SUBLANES - hb)))
    wgt = wg.transpose(2, 1, 3, 0).reshape(nhb * GATE_ROWS, D_MODEL)
    gb = jnp.pad(gate_bias.reshape(N_GATES, nhb, hb), ((0, 0), (0, 0), (0, SUBLANES - hb)))
    gb = gb.transpose(1, 0, 2).reshape(nhb * GATE_ROWS)
    return wgt.astype(BF16), gb.astype(F32)


def _pack_head_rows(x):
    hb = HEADS_PER_STEP
    nhb = N_HEADS // hb
    x = x.reshape(x.shape[:2] + (nhb, hb) + x.shape[3:])
    x = jnp.moveaxis(x, 2, 1)
    pad = [(0, 0)] * x.ndim
    pad[3] = (0, SUBLANES - hb)
    return jnp.pad(x, pad)


def _unpack_head_rows(x):
    x = jnp.moveaxis(x[:, :, :, :HEADS_PER_STEP], 1, 2)
    return x.reshape(x.shape[:2] + (N_HEADS,) + x.shape[4:])


def _mod_map(mod_row, seq_len, tm):
    if mod_row is None:
        per = seq_len // tm
        return lambda i: (1 + i // per, 0, 0)
    return lambda i: (mod_row, 0, 0)


def _mixer_stage(x, mod, mod_row, init, params, w_out, cast_rows, cast_w1=None, *, seg, tm, emit_state):
    batch, seq_len, _ = x.shape
    wts, mlstm_nw, pool_w, pool_scale, norm_w = params
    tokens = batch * seq_len
    xf = x.reshape(tokens, D_MODEL)
    q, kt, v, og, hp, stats = _inproj_call(xf, mod, mod_row, norm_w, wts, pool_w, pool_scale,
                                           batch=batch, seq_len=seq_len, tm=tm, seg=seg)
    shp = (batch, seq_len, MLSTM_WIDTH)
    res = _mlstm_call(q.reshape(shp), kt, v.reshape(shp), og.reshape(shp), stats, mlstm_nw, init,
                      cast_rows, batch=batch, seq_len=seq_len, emit_state=emit_state)
    hm = res[0].reshape(tokens, MLSTM_WIDTH)
    outs = _outproj_call(hm, hp, xf, mod, _mod_map(mod_row, seq_len, tm), norm_w,
                         res[-1] if w_out is None else w_out, cast_w1, tm=tm)
    return outs, res[1:]


def kernel(x_prompt, x_sample, c, state_C, state_n, state_m, c_ctx, w_in, gate_bias, mlstm_norm_w,
           pool_w, pool_scale, w_out, ada_w, ada_b, norm_w, w1, w2):
    m4 = 4 * MLSTM_WIDTH
    yp, ys = x_prompt, x_sample
    new_c, new_n, new_m = [], [], []
    for layer in range(w_in.shape[0]):
        wl = w_in[layer]
        wgt, gb = _gate_weights(wl[:, m4:m4 + N_GATES * N_HEADS], gate_bias[layer])
        wkg = jnp.concatenate([wl[:, MLSTM_WIDTH:2 * MLSTM_WIDTH].T.astype(BF16), wgt], axis=0)
        wts = (wl[:, 0:MLSTM_WIDTH].astype(BF16),
               wkg,
               wl[:, 2 * MLSTM_WIDTH:3 * MLSTM_WIDTH].astype(BF16),
               wl[:, 3 * MLSTM_WIDTH:m4].astype(BF16),
               wl[:, m4 + N_GATES * N_HEADS:].astype(BF16),
               gb)
        params = (wts, mlstm_norm_w[layer].reshape(1, MLSTM_WIDTH), pool_w[layer].astype(BF16),
                  pool_scale[layer], norm_w[layer])

        n_lat = c.shape[0]
        rows = 16
        cond = jnp.concatenate([c_ctx[None, :], c, jnp.zeros((rows - 1 - n_lat, D_MODEL), F32)], axis=0)
        mod = _mod_call(cond, ada_w[layer], ada_b[layer]).reshape(rows, N_MOD, D_MODEL)

        tm = 512
        (x1p, h2p), (cc, nn, mm, w_out_b) = _mixer_stage(yp, mod, 0, None, params, None, w_out[layer],
                                                         seg=yp.shape[1], tm=tm, emit_state=True)
        new_c.append(cc)
        new_n.append(_unpack_head_rows(nn))
        new_m.append(_unpack_head_rows(mm)[..., 0])

        sc = state_C[:, layer].astype(F32)
        sn = _pack_head_rows(state_n[:, layer].astype(F32))
        sm = _pack_head_rows(jnp.broadcast_to(state_m[:, layer].astype(F32)[..., None],
                                              (n_lat, 2, N_HEADS, LANES)))
        (x1s, h2s, w1b), (w2b,) = _mixer_stage(ys, mod, None, (sc, sn, sm), params, w_out_b, w2[layer],
                                               w1[layer], seg=GRID_W, tm=tm, emit_state=False)
        yp = _mlp_call(h2p, x1p, mod, _mod_map(0, yp.shape[1], tm), norm_w[layer], w1b, w2b,
                       tm=tm, tf=1024).reshape(yp.shape)
        ys = _mlp_call(h2s, x1s, mod, _mod_map(None, ys.shape[1], tm), norm_w[layer], w1b, w2b,
                       tm=tm, tf=1024).reshape(ys.shape)
    return (yp, ys, jnp.stack(new_c, axis=1), jnp.stack(new_n, axis=1), jnp.stack(new_m, axis=1))
```

```python
import functools

import jax
import jax.numpy as jnp
from jax import lax
from jax.experimental import pallas as pl
from jax.experimental.pallas import tpu as pltpu

F32 = jnp.float32
BF16 = jnp.bfloat16

D_MODEL = 2048
MLSTM_WIDTH = 1024
N_HEADS = 8
HEAD_DIM = 128
POOL_WIDTH = 1024
POOL_WINDOWS = (2, 4, 8, 16)
POOL_GROUP_DIM = 256
N_GATES = 4
D_FF = 8192
CHUNK = 128
N_MOD = 6
GRID_W = 64
EPS = 1e-6

LANES = 128
SUBLANES = 8
VMEM_LIMIT = 56 * 1024 * 1024

HEADS_PER_STEP = 4
GATE_ROWS = N_GATES * SUBLANES
GATE_STATS = 5
STAT_ROWS = 2 * GATE_STATS * SUBLANES
FINALIZE_UNROLL = 4


def _cparams(sem):
    return pltpu.CompilerParams(dimension_semantics=sem, vmem_limit_bytes=VMEM_LIMIT)


def _resident(shape):
    nd = len(shape)
    return pl.BlockSpec(shape, lambda *_: (0,) * nd, pipeline_mode=pl.Buffered(1))


def _rms(x, w):
    ms = jnp.mean(x * x, axis=-1, keepdims=True)
    return x * lax.rsqrt(ms + EPS) * w


def _mod_kernel(cond_ref, w_ref, b_ref, out_ref):
    c = cond_ref[...]
    s = c * jax.nn.sigmoid(c)
    out_ref[...] = jnp.dot(s.astype(BF16), w_ref[...].astype(BF16),
                           preferred_element_type=F32) + b_ref[...]


def _mod_call(cond, ada_w, ada_b):
    rows = cond.shape[0]
    n = ada_w.shape[1]
    tn = 1536
    return pl.pallas_call(
        _mod_kernel,
        grid=(n // tn,),
        in_specs=[pl.BlockSpec((rows, D_MODEL), lambda j: (0, 0)),
                  pl.BlockSpec((D_MODEL, tn), lambda j: (0, j)),
                  pl.BlockSpec((1, tn), lambda j: (0, j))],
        out_specs=pl.BlockSpec((rows, tn), lambda j: (0, j)),
        out_shape=jax.ShapeDtypeStruct((rows, n), F32),
        compiler_params=_cparams(("arbitrary",)),
        name="mod",
    )(cond, ada_w, ada_b.reshape(1, n))


NORM_SLICE_ROWS = 64


def _lane_scan_pair(fwd, bwd, op, ident):
    lane = lax.broadcasted_iota(jnp.int32, fwd.shape, 1)
    k = 1
    while k < LANES:
        f_shift = pltpu.roll(fwd, k, axis=1)
        b_shift = pltpu.roll(bwd, LANES - k, axis=1)
        fwd = op(fwd, jnp.where(lane >= k, f_shift, ident))
        bwd = op(bwd, jnp.where(lane < LANES - k, b_shift, ident))
        k *= 2
    return fwd, bwd


def _inproj_kernel(x_ref, mod_ref, nw_ref, wq_ref, wkg_ref, wv_ref, wo_ref, wu_ref, gb_ref, pw_ref, ps_ref,
                   q_ref, kt_ref, v_ref, og_ref, hp_ref, g_ref, *, seqs, chunks, seg):
    x = x_ref[...]
    mod = mod_ref[0]
    tm = x.shape[0]
    h = (_rms(x, nw_ref[0:1, :]) * (1.0 + mod[1:2, :]) + mod[0:1, :]).astype(BF16)
    ktg = lax.dot_general(wkg_ref[...], h, (((1,), (1,)), ((), ())), preferred_element_type=F32)
    kt = ktg[0:MLSTM_WIDTH].astype(BF16)
    gt = ktg[MLSTM_WIDTH:] + gb_ref[...]
    nhb = gt.shape[0] // GATE_ROWS
    for s in range(seqs):
        for c in range(chunks):
            lo = (s * chunks + c) * CHUNK
            kt_ref[s, c] = kt[:, lo:lo + CHUNK]
    def proj(wt_ref):
        return lax.dot_general(h, wt_ref[...], (((1,), (1,)), ((), ())), preferred_element_type=F32)

    u = proj(wu_ref)
    pos = lax.broadcasted_iota(jnp.int32, (tm, 1), 0) % seg

    def pool_group(gi):
        half = POOL_WINDOWS[gi] // 2
        inv_cnt = 1.0 / (jnp.minimum(pos + half, seg) - jnp.maximum(pos - half, 0)).astype(F32)
        cs = slice(gi * POOL_GROUP_DIM, (gi + 1) * POOL_GROUP_DIM)
        p = jnp.concatenate(
            [_pool_minus_self(u[:, lo:lo + LANES], half, pos, seg, inv_cnt).astype(BF16)
             for lo in range(cs.start, cs.stop, LANES)], axis=1)
        y = jnp.dot(p, pw_ref[gi], preferred_element_type=F32)
        hp_ref[:, cs] = (y * ps_ref[0:1, cs]).astype(BF16)

    q = proj(wq_ref) * (HEAD_DIM ** -0.5)
    q_ref[...] = q.astype(BF16)
    pool_group(0)
    pool_group(1)
    v_ref[...] = proj(wv_ref).astype(BF16)
    pool_group(2)
    og_ref[...] = jax.nn.sigmoid(proj(wo_ref))
    pool_group(3)

    groups = [(s, c, hbk) for s in range(seqs) for c in range(chunks) for hbk in range(nhb)]

    def gate(g):
        return jnp.concatenate(
            [gt[hbk * GATE_ROWS + g * SUBLANES:hbk * GATE_ROWS + (g + 1) * SUBLANES,
                (s * chunks + c) * CHUNK:(s * chunks + c + 1) * CHUNK] for s, c, hbk in groups], axis=0)

    li = (gate(0), gate(2))
    lf = (jax.nn.log_sigmoid(gate(1)), jax.nn.log_sigmoid(gate(3)))
    b = _lane_scan_pair(lf[0], lf[1], jnp.add, 0.0)
    a = (li[0] - b[0], li[1] - b[1])
    pm = _lane_scan_pair(a[0], a[1], jnp.maximum, jnp.float32(-jnp.inf))
    stats = []
    for d in range(2):
        amax = jnp.broadcast_to(jnp.max(a[d], axis=1, keepdims=True), a[d].shape)
        bsum = jnp.broadcast_to(jnp.sum(lf[d], axis=1, keepdims=True), a[d].shape)
        stats += [a[d], b[d], pm[d], amax, bsum]
    for gi, (s, c, hbk) in enumerate(groups):
        rs = slice(gi * SUBLANES, (gi + 1) * SUBLANES)
        g_ref[s, c, hbk * STAT_ROWS:(hbk + 1) * STAT_ROWS, :] = jnp.concatenate(
            [st[rs] for st in stats], axis=0)


def _inproj_call(x, mod, mod_row, norm_w, wts, pool_w, pool_scale, *, batch, seq_len, tm, seg):
    wq, wkg, wv, wo, wu, gb = wts
    tokens = batch * seq_len
    nc = seq_len // CHUNK
    grows = wkg.shape[0] - MLSTM_WIDTH
    srows = grows // GATE_ROWS * STAT_ROWS
    if tm >= seq_len:
        seqs, chunks = tm // seq_len, nc
        kt_map = lambda i: (i, 0, 0, 0)
    else:
        seqs, chunks = 1, tm // CHUNK
        per = seq_len // tm
        kt_map = lambda i: (i // per, i % per, 0, 0)
    tiles_per_seq = max(seq_len // tm, 1)
    if mod_row is None:
        mod_map = lambda i: (1 + i // tiles_per_seq, 0, 0)
    else:
        mod_map = lambda i: (mod_row, 0, 0)
    tok = lambda i: (i, 0)
    gbb = jnp.broadcast_to(gb[:, None], (grows, tm))
    kern = functools.partial(_inproj_kernel, seqs=seqs, chunks=chunks, seg=seg)
    return pl.pallas_call(
        kern,
        grid=(tokens // tm,),
        in_specs=[pl.BlockSpec((tm, D_MODEL), tok),
                  pl.BlockSpec((1, N_MOD, D_MODEL), mod_map),
                  _resident((4, D_MODEL)),
                  _resident(wq.shape), _resident(wkg.shape), _resident(wv.shape),
                  _resident(wo.shape), _resident(wu.shape),
                  _resident((grows, tm)),
                  _resident(pool_w.shape),
                  _resident((1, POOL_WIDTH))],
        out_specs=[pl.BlockSpec((tm, MLSTM_WIDTH), tok),
                   pl.BlockSpec((seqs, chunks, MLSTM_WIDTH, CHUNK), kt_map),
                   pl.BlockSpec((tm, MLSTM_WIDTH), tok),
                   pl.BlockSpec((tm, MLSTM_WIDTH), tok),
                   pl.BlockSpec((tm, POOL_WIDTH), tok),
                   pl.BlockSpec((seqs, chunks, srows, CHUNK), kt_map)],
        out_shape=[jax.ShapeDtypeStruct((tokens, MLSTM_WIDTH), BF16),
                   jax.ShapeDtypeStruct((batch, nc, MLSTM_WIDTH, CHUNK), BF16),
                   jax.ShapeDtypeStruct((tokens, MLSTM_WIDTH), BF16),
                   jax.ShapeDtypeStruct((tokens, MLSTM_WIDTH), F32),
                   jax.ShapeDtypeStruct((tokens, POOL_WIDTH), BF16),
                   jax.ShapeDtypeStruct((batch, nc, srows, CHUNK), F32)],
        compiler_params=_cparams(("arbitrary",)),
        name="in_proj",
    )(x, mod, norm_w, wq, wkg, wv, wo, wu, gbb, pool_w, pool_scale.reshape(1, POOL_WIDTH))


def _chunk_start(c):
    return c * CHUNK if isinstance(c, int) else pl.multiple_of(c * CHUNK, CHUNK)


def _mlstm_kernel(*refs, nc, hb, has_init, emit_state, cast_weight):
    it = iter(refs)
    q_ref, kt_ref, v_ref, o_ref, g_ref, nw_ref = (next(it) for _ in range(6))
    if has_init:
        c0_ref, n0_ref, m0_ref = (next(it) for _ in range(3))
    if cast_weight:
        wf_ref = next(it)
    out_ref = next(it)
    if emit_state:
        cout_ref, nout_ref, mout_ref = (next(it) for _ in range(3))
    if cast_weight:
        next(it)[...] = wf_ref[...].astype(BF16)
    hf_sc, hb_sc, cn_sc, m_sc, xt_sc, rows_sc = (next(it) for _ in range(6))

    if has_init:
        for d in range(2):
            for j in range(hb):
                cn_sc[d * hb + j, :, 0:HEAD_DIM] = c0_ref[0, d, j]
                n_row = n0_ref[0, 0, d][j:j + 1, :]
                cn_sc[d * hb + j, :, HEAD_DIM:] = jnp.broadcast_to(n_row, (HEAD_DIM, LANES)).T
    else:
        cn_sc[...] = jnp.zeros_like(cn_sc)

    neg_inf = jnp.float32(-jnp.inf)

    cols = []
    for d in range(2):
        def stat(k):
            lo = (d * GATE_STATS + k) * SUBLANES
            return g_ref[0, :, lo:lo + SUBLANES, :].reshape(nc * SUBLANES, LANES)
        a, b, pm, amax, bsum = (stat(k) for k in range(GATE_STATS))
        m = m0_ref[0, 0, d] if has_init else jnp.zeros((SUBLANES, LANES), F32)
        m_prev = [None] * nc
        for ci in (range(nc) if d == 0 else range(nc - 1, -1, -1)):
            rs = slice(ci * SUBLANES, (ci + 1) * SUBLANES)
            m_prev[ci] = m
            m = bsum[rs] + jnp.maximum(m, amax[rs])
        m_sc[d] = m
        mp = jnp.concatenate(m_prev, axis=0)
        mu = jnp.maximum(mp, pm)
        mu_last = jnp.maximum(mp, amax)
        rows_sc[0, d] = a.reshape(nc, SUBLANES, LANES)
        rows_sc[1, d] = jnp.exp(a - mu_last).reshape(nc, SUBLANES, LANES)
        rows_sc[2, d] = jnp.exp(mp - mu_last).reshape(nc, SUBLANES, LANES)
        cols.append((mu, jnp.exp(mp - mu), jnp.exp(-(b + mu))))
    zpad = jnp.zeros((LANES - 6 * SUBLANES, LANES), F32)
    for c in range(nc):
        pieces = []
        for d, ci in ((0, c), (1, nc - 1 - c)):
            pieces += [x[ci * SUBLANES:(ci + 1) * SUBLANES] for x in cols[d]]
        xt_sc[c] = jnp.concatenate(pieces + [zpad], axis=0).T

    row = lax.broadcasted_iota(jnp.int32, (CHUNK, CHUNK), 0)
    col = lax.broadcasted_iota(jnp.int32, (CHUNK, CHUNK), 1)
    causal = (col <= row, col >= row)
    ones_blk = jnp.ones((CHUNK, HEAD_DIM), BF16)

    def step(c, carry):
        chunk_of = (c, nc - 1 - c)
        xt = xt_sc[c]
        chains = []
        for d in range(2):
            ch = chunk_of[d]
            t0 = _chunk_start(ch)
            a_rows = rows_sc[0, d, ch]
            for j in range(hb):
                hs = slice(j * HEAD_DIM, (j + 1) * HEAD_DIM)
                qc = q_ref[0, pl.ds(t0, CHUNK), hs]
                base = 3 * d * SUBLANES + j
                mu_c = xt[:, base:base + 1]
                g_c = xt[:, base + SUBLANES:base + SUBLANES + 1]
                s = jnp.dot(qc, kt_ref[0, ch, hs, :], preferred_element_type=F32)
                p = s * jnp.exp(jnp.where(causal[d], a_rows[j:j + 1, :] - mu_c, neg_inf))
                gq = (qc.astype(F32) * g_c).astype(BF16)
                chains.append((d, j, ch, t0, jnp.concatenate([p.astype(BF16), gq], axis=1)))

        for d, j, ch, t0, lhs in chains:
            hs = slice(j * HEAD_DIM, (j + 1) * HEAD_DIM)
            base = 3 * d * SUBLANES + j
            en_c = xt[:, base + 2 * SUBLANES:base + 2 * SUBLANES + 1]
            v1 = jnp.concatenate([v_ref[0, pl.ds(t0, CHUNK), hs], ones_blk], axis=1)
            cn = cn_sc[d * hb + j]
            tot = jnp.dot(lhs, jnp.concatenate([v1, cn.astype(BF16)], axis=0),
                          preferred_element_type=F32)
            hval = tot[:, :HEAD_DIM] / jnp.maximum(jnp.abs(tot[:, HEAD_DIM:]), en_c)
            dst = hf_sc if d == 0 else hb_sc
            dst[pl.ds(t0, CHUNK), hs] = hval

            e_row = rows_sc[1, d, ch][j:j + 1, :]
            gs_row = rows_sc[2, d, ch][j:j + 1, :]
            wkt = (kt_ref[0, ch, hs, :].astype(F32) * e_row).astype(BF16)
            dcn = jnp.dot(wkt, v1, preferred_element_type=F32)
            gs_blk = jnp.broadcast_to(gs_row, (HEAD_DIM, LANES))
            cn_sc[d * hb + j] = jnp.concatenate([gs_blk, gs_blk], axis=1) * cn + dcn
        return carry

    if nc <= 2:
        for c in range(nc):
            step(c, 0)
    else:
        lax.fori_loop(0, nc, step, 0, unroll=2)

    def finalize(c, carry):
        t0 = _chunk_start(c)
        hsum = hf_sc[pl.ds(t0, CHUNK), :] + hb_sc[pl.ds(t0, CHUNK), :]
        og = o_ref[0, pl.ds(t0, CHUNK), :]
        for j in range(hb):
            hs = slice(j * HEAD_DIM, (j + 1) * HEAD_DIM)
            hn = _rms(hsum[:, hs], nw_ref[0:1, hs])
            out_ref[0, pl.ds(t0, CHUNK), hs] = (hn * og[:, hs]).astype(BF16)
        return carry

    if nc <= 2:
        for c in range(nc):
            finalize(c, 0)
    else:
        lax.fori_loop(0, nc, finalize, 0, unroll=FINALIZE_UNROLL)

    if emit_state:
        pad_rows = jnp.zeros((SUBLANES - hb, LANES), F32)
        for d in range(2):
            n_rows = []
            for j in range(hb):
                cout_ref[0, d, j] = cn_sc[d * hb + j, :, 0:HEAD_DIM]
                n_rows.append(cn_sc[d * hb + j, :, HEAD_DIM:].T[0:1, :])
            nout_ref[0, 0, d] = jnp.concatenate(n_rows + ([pad_rows] if hb < SUBLANES else []), axis=0)
            mout_ref[0, 0, d] = m_sc[d]


def _mlstm_call(q, kt, v, o, stats, nw, init, cast_weight=None, *, batch, seq_len, emit_state):
    hb = HEADS_PER_STEP
    nhb = N_HEADS // hb
    nc = seq_len // CHUNK
    w = hb * HEAD_DIM
    seq_map = lambda b, k: (b, 0, k)
    in_specs = [pl.BlockSpec((1, seq_len, w), seq_map),
                pl.BlockSpec((1, nc, w, CHUNK), lambda b, k: (b, 0, k, 0)),
                pl.BlockSpec((1, seq_len, w), seq_map),
                pl.BlockSpec((1, seq_len, w), seq_map),
                pl.BlockSpec((1, nc, STAT_ROWS, CHUNK), lambda b, k: (b, 0, k, 0)),
                pl.BlockSpec((1, w), lambda b, k: (0, k))]
    args = [q, kt, v, o, stats, nw]
    state_map = lambda b, k: (b, 0, k, 0, 0)
    vec_map = lambda b, k: (b, k, 0, 0, 0)
    vec_spec = pl.BlockSpec((1, 1, 2, SUBLANES, LANES), vec_map)
    if init is not None:
        c0, n0, m0 = init
        in_specs += [pl.BlockSpec((1, 2, hb, HEAD_DIM, HEAD_DIM), state_map), vec_spec, vec_spec]
        args += [c0, n0, m0]
    out_specs = [pl.BlockSpec((1, seq_len, w), seq_map)]
    out_shape = [jax.ShapeDtypeStruct((batch, seq_len, MLSTM_WIDTH), BF16)]
    if emit_state:
        out_specs += [pl.BlockSpec((1, 2, hb, HEAD_DIM, HEAD_DIM), state_map), vec_spec, vec_spec]
        out_shape += [jax.ShapeDtypeStruct((batch, 2, N_HEADS, HEAD_DIM, HEAD_DIM), F32),
                      jax.ShapeDtypeStruct((batch, nhb, 2, SUBLANES, LANES), F32),
                      jax.ShapeDtypeStruct((batch, nhb, 2, SUBLANES, LANES), F32)]
    if cast_weight is not None:
        slab = cast_weight.shape[0] // (batch * nhb)
        assert slab * batch * nhb == cast_weight.shape[0] and slab % (2 * SUBLANES) == 0
        slab_spec = pl.BlockSpec((slab, D_MODEL), lambda b, k: (b * nhb + k, 0))
        in_specs.append(slab_spec)
        args.append(cast_weight)
        out_specs.append(slab_spec)
        out_shape.append(jax.ShapeDtypeStruct(cast_weight.shape, BF16))
    kern = functools.partial(_mlstm_kernel, nc=nc, hb=hb, has_init=init is not None,
                             emit_state=emit_state, cast_weight=cast_weight is not None)
    return pl.pallas_call(
        kern,
        grid=(batch, nhb),
        in_specs=in_specs,
        out_specs=out_specs,
        out_shape=out_shape,
        scratch_shapes=[pltpu.VMEM((seq_len, w), F32),
                        pltpu.VMEM((seq_len, w), F32),
                        pltpu.VMEM((2 * hb, HEAD_DIM, 2 * HEAD_DIM), F32),
                        pltpu.VMEM((2, SUBLANES, LANES), F32),
                        pltpu.VMEM((nc, LANES, LANES), F32),
                        pltpu.VMEM((3, 2, nc, SUBLANES, LANES), F32)],
        compiler_params=_cparams(("arbitrary", "arbitrary")),
        name="mlstm",
    )(*args)


def _shift_rows(x, dlt):
    return pltpu.roll(x, (-dlt) % x.shape[0], axis=0)


def _pool_minus_self(x, half, pos, seg, inv_cnt):
    fwd = x
    k = 1
    while k < half:
        fwd = fwd + jnp.where(pos + k < seg, _shift_rows(fwd, k), 0.0)
        k *= 2
    bwd = jnp.where(pos >= 1, _shift_rows(x, -1), 0.0)
    k = 1
    while k < half:
        bwd = bwd + jnp.where(pos >= k, _shift_rows(bwd, -k), 0.0)
        k *= 2
    return (fwd + bwd) * inv_cnt - x


OUTPROJ_ROW_BLOCKS = 4
OUTPROJ_EPI_ROWS = 64
OUTPROJ_SLICE_COLS = 256


def _outproj_kernel(*refs, cast_mlp_weights):
    hm_ref, hp_ref, x_ref, mod_ref, nw_ref, wout_ref = refs[:6]
    if cast_mlp_weights:
        w1_ref, x1_ref, h2_ref, w1b_ref, mix_ref = refs[6:]
        w1b_ref[...] = w1_ref[...].astype(BF16)
    else:
        x1_ref, h2_ref, mix_ref = refs[6:]
    tm = x_ref.shape[0]
    nblk = OUTPROJ_ROW_BLOCKS
    rb = tm // nblk
    nslice = D_MODEL // OUTPROJ_SLICE_COLS
    mod = mod_ref[0]
    gain1 = nw_ref[1:2, :] * mod[2:3, :]
    gain2 = nw_ref[2:3, :] * (1.0 + mod[4:5, :])
    shift2 = mod[3:4, :]

    def epilogue_slices(r):
        def one(lo):
            rows = slice(lo, lo + OUTPROJ_EPI_ROWS)
            mix = mix_ref[rows, :]
            ms = jnp.mean(mix * mix, axis=-1, keepdims=True)
            x1 = x_ref[rows, :] + (mix * lax.rsqrt(ms + EPS)) * gain1
            x1_ref[rows, :] = x1
            ms2 = jnp.mean(x1 * x1, axis=-1, keepdims=True)
            h2_ref[rows, :] = ((x1 * lax.rsqrt(ms2 + EPS)) * gain2 + shift2).astype(BF16)
        return [functools.partial(one, r * rb + k * OUTPROJ_EPI_ROWS)
                for k in range(rb // OUTPROJ_EPI_ROWS)]

    def run_block(r, pending):
        rows = slice(r * rb, (r + 1) * rb)
        lhs = jnp.concatenate([hm_ref[rows, :], hp_ref[rows, :]], axis=1)
        for idx in range(nslice):
            ns = slice(idx * OUTPROJ_SLICE_COLS, (idx + 1) * OUTPROJ_SLICE_COLS)
            mix_ref[rows, ns] = jnp.dot(lhs, wout_ref[:, ns], preferred_element_type=F32)
            if pending and idx % 4 == 3:
                pending.pop(0)()
        while pending:
            pending.pop(0)()

    pending = []
    for r in range(nblk):
        run_block(r, pending)
        pending = epilogue_slices(r)
    for fn in pending:
        fn()


def _outproj_call(hm, hp, x, mod, mod_map, norm_w, w_out, cast_w1=None, *, tm):
    tokens = x.shape[0]
    steps = tokens // tm
    tok = lambda i: (i, 0)
    in_specs = [pl.BlockSpec((tm, MLSTM_WIDTH), tok),
                pl.BlockSpec((tm, POOL_WIDTH), tok),
                pl.BlockSpec((tm, D_MODEL), tok),
                pl.BlockSpec((1, N_MOD, D_MODEL), mod_map),
                _resident((4, D_MODEL)),
                _resident(w_out.shape)]
    out_specs = [pl.BlockSpec((tm, D_MODEL), tok),
                 pl.BlockSpec((tm, D_MODEL), tok)]
    out_shape = [jax.ShapeDtypeStruct((tokens, D_MODEL), F32),
                 jax.ShapeDtypeStruct((tokens, D_MODEL), BF16)]
    args = [hm, hp, x, mod, norm_w, w_out]
    if cast_w1 is not None:
        slab = D_FF // steps
        assert slab * steps == D_FF and slab % LANES == 0
        slab_spec = pl.BlockSpec((D_MODEL, slab), lambda i: (0, i))
        in_specs.append(slab_spec)
        out_specs.append(slab_spec)
        out_shape.append(jax.ShapeDtypeStruct((D_MODEL, D_FF), BF16))
        args.append(cast_w1)
    return pl.pallas_call(
        functools.partial(_outproj_kernel, cast_mlp_weights=cast_w1 is not None),
        grid=(steps,),
        in_specs=in_specs,
        out_specs=out_specs,
        out_shape=out_shape,
        scratch_shapes=[pltpu.VMEM((tm, D_MODEL), F32)],
        compiler_params=_cparams(("arbitrary",)),
        name="out_proj",
    )(*args)


MLP_LAST_ROW_BLOCKS = 4
MLP_SLICE_COLS = 256


def _mlp_kernel(h_ref, w1_ref, w2_ref, x1_ref, mod_ref, nw_ref, out_ref, acc_ref):
    k = pl.program_id(1)
    last = pl.num_programs(1) - 1
    tm, tf = h_ref.shape[0], w1_ref.shape[1]

    def hidden(rows, cols=slice(None)):
        a = jnp.dot(h_ref[rows, :], w1_ref[:, cols], preferred_element_type=F32)
        return jnp.square(jnp.maximum(a, 0.0)).astype(BF16)

    @pl.when(k == 0)
    def _():
        acc_ref[...] = jnp.dot(hidden(slice(None)), w2_ref[...], preferred_element_type=F32)

    @pl.when((k > 0) & (k < last))
    def _():
        acc_ref[...] += jnp.dot(hidden(slice(None)), w2_ref[...], preferred_element_type=F32)

    @pl.when(k == last)
    def _():
        rb = tm // MLP_LAST_ROW_BLOCKS
        gain = nw_ref[3:4, :] * mod_ref[0, 5:6, :]

        def epilogue_slices(r):
            def one(lo):
                rows = slice(lo, lo + NORM_SLICE_ROWS)
                y = acc_ref[rows, :]
                ms = jnp.mean(y * y, axis=-1, keepdims=True)
                out_ref[rows, :] = x1_ref[rows, :] + (y * lax.rsqrt(ms + EPS)) * gain
            return [functools.partial(one, r * rb + j * NORM_SLICE_ROWS)
                    for j in range(rb // NORM_SLICE_ROWS)]

        pending = []
        n1, n2 = tf // MLP_SLICE_COLS, D_MODEL // MLP_SLICE_COLS
        for r in range(MLP_LAST_ROW_BLOCKS):
            rows = slice(r * rb, (r + 1) * rb)
            parts = []
            for idx in range(n1):
                parts.append(hidden(rows, slice(idx * MLP_SLICE_COLS, (idx + 1) * MLP_SLICE_COLS)))
                if pending and idx % 2 == 1:
                    pending.pop(0)()
            a = jnp.concatenate(parts, axis=1)
            for idx in range(n2):
                ns = slice(idx * MLP_SLICE_COLS, (idx + 1) * MLP_SLICE_COLS)
                acc_ref[rows, ns] += jnp.dot(a, w2_ref[:, ns], preferred_element_type=F32)
                if pending and idx % 2 == 1:
                    pending.pop(0)()
            while pending:
                pending.pop(0)()
            pending = epilogue_slices(r)
        for fn in pending:
            fn()


def _mlp_call(h2, x1, mod, mod_map, norm_w, w1, w2, *, tm, tf):
    tokens = x1.shape[0]
    tok = lambda i, k: (i, 0)
    return pl.pallas_call(
        _mlp_kernel,
        grid=(tokens // tm, D_FF // tf),
        in_specs=[pl.BlockSpec((tm, D_MODEL), tok),
                  pl.BlockSpec((D_MODEL, tf), lambda i, k: (0, k)),
                  pl.BlockSpec((tf, D_MODEL), lambda i, k: (k, 0)),
                  pl.BlockSpec((tm, D_MODEL), tok),
                  pl.BlockSpec((1, N_MOD, D_MODEL), lambda i, k: mod_map(i)),
                  pl.BlockSpec((4, D_MODEL), lambda i, k: (0, 0))],
        out_specs=pl.BlockSpec((tm, D_MODEL), tok),
        out_shape=jax.ShapeDtypeStruct((tokens, D_MODEL), F32),
        scratch_shapes=[pltpu.VMEM((tm, D_MODEL), F32)],
        compiler_params=_cparams(("arbitrary", "arbitrary")),
        name="mlp",
    )(h2, w1, w2, x1, mod, norm_w)


def _gate_weights(w_g, gate_bias):
    hb = HEADS_PER_STEP
    nhb = N_HEADS // hb
    wg = w_g.reshape(D_MODEL, N_GATES, nhb, hb)
    wg = jnp.pad(wg, ((0, 0), (0, 0), (0, 0), (0, SUBLANES - hb)))
    wgt = wg.transpose(2, 1, 3, 0).reshape(nhb * GATE_ROWS, D_MODEL)
    gb = jnp.pad(gate_bias.reshape(N_GATES, nhb, hb), ((0, 0), (0, 0), (0, SUBLANES - hb)))
    gb = gb.transpose(1, 0, 2).reshape(nhb * GATE_ROWS)
    return wgt.astype(BF16), gb.astype(F32)


def _pack_head_rows(x):
    hb = HEADS_PER_STEP
    nhb = N_HEADS // hb
    x = x.reshape(x.shape[:2] + (nhb, hb) + x.shape[3:])
    x = jnp.moveaxis(x, 2, 1)
    pad = [(0, 0)] * x.ndim
    pad[3] = (0, SUBLANES - hb)
    return jnp.pad(x, pad)


def _unpack_head_rows(x):
    x = jnp.moveaxis(x[:, :, :, :HEADS_PER_STEP], 1, 2)
    return x.reshape(x.shape[:2] + (N_HEADS,) + x.shape[4:])


def _mod_map(mod_row, seq_len, tm):
    if mod_row is None:
        per = seq_len // tm
        return lambda i: (1 + i // per, 0, 0)
    return lambda i: (mod_row, 0, 0)


def _mixer_stage(x, mod, mod_row, init, params, w_out, cast_rows, cast_w1=None, *, seg, tm, emit_state):
    batch, seq_len, _ = x.shape
    wts, mlstm_nw, pool_w, pool_scale, norm_w = params
    tokens = batch * seq_len
    xf = x.reshape(tokens, D_MODEL)
    q, kt, v, og, hp, stats = _inproj_call(xf, mod, mod_row, norm_w, wts, pool_w, pool_scale,
                                           batch=batch, seq_len=seq_len, tm=tm, seg=seg)
    shp = (batch, seq_len, MLSTM_WIDTH)
    res = _mlstm_call(q.reshape(shp), kt, v.reshape(shp), og.reshape(shp), stats, mlstm_nw, init,
                      cast_rows, batch=batch, seq_len=seq_len, emit_state=emit_state)
    hm = res[0].reshape(tokens, MLSTM_WIDTH)
    outs = _outproj_call(hm, hp, xf, mod, _mod_map(mod_row, seq_len, tm), norm_w,
                         res[-1] if w_out is None else w_out, cast_w1, tm=tm)
    return outs, res[1:]


def kernel(x_prompt, x_sample, c, state_C, state_n, state_m, c_ctx, w_in, gate_bias, mlstm_norm_w,
           pool_w, pool_scale, w_out, ada_w, ada_b, norm_w, w1, w2):
    m4 = 4 * MLSTM_WIDTH
    yp, ys = x_prompt, x_sample
    new_c, new_n, new_m = [], [], []
    for layer in range(w_in.shape[0]):
        wt = w_in[layer].T.astype(BF16)
        wgt, gb = _gate_weights(w_in[layer][:, m4:m4 + N_GATES * N_HEADS], gate_bias[layer])
        wkg = jnp.concatenate([wt[MLSTM_WIDTH:2 * MLSTM_WIDTH], wgt], axis=0)
        wts = (wt[0:MLSTM_WIDTH],
               wkg,
               wt[2 * MLSTM_WIDTH:3 * MLSTM_WIDTH],
               wt[3 * MLSTM_WIDTH:m4],
               wt[m4 + N_GATES * N_HEADS:],
               gb)
        params = (wts, mlstm_norm_w[layer].reshape(1, MLSTM_WIDTH), pool_w[layer].astype(BF16),
                  pool_scale[layer], norm_w[layer])

        n_lat = c.shape[0]
        rows = 16
        cond = jnp.concatenate([c_ctx[None, :], c, jnp.zeros((rows - 1 - n_lat, D_MODEL), F32)], axis=0)
        mod = _mod_call(cond, ada_w[layer], ada_b[layer]).reshape(rows, N_MOD, D_MODEL)

        tm = 512
        (x1p, h2p), (cc, nn, mm, w_out_b) = _mixer_stage(yp, mod, 0, None, params, None, w_out[layer],
                                                         seg=yp.shape[1], tm=tm, emit_state=True)
        new_c.append(cc)
        new_n.append(_unpack_head_rows(nn))
        new_m.append(_unpack_head_rows(mm)[..., 0])

        sc = state_C[:, layer].astype(F32)
        sn = _pack_head_rows(state_n[:, layer].astype(F32))
        sm = _pack_head_rows(jnp.broadcast_to(state_m[:, layer].astype(F32)[..., None],
                                              (n_lat, 2, N_HEADS, LANES)))
        (x1s, h2s, w1b), (w2b,) = _mixer_stage(ys, mod, None, (sc, sn, sm), params, w_out_b, w2[layer],
                                               w1[layer], seg=GRID_W, tm=tm, emit_state=False)
        yp = _mlp_call(h2p, x1p, mod, _mod_map(0, yp.shape[1], tm), norm_w[layer], w1b, w2b,
                       tm=tm, tf=1024).reshape(yp.shape)
        ys = _mlp_call(h2s, x1s, mod, _mod_map(None, ys.shape[1], tm), norm_w[layer], w1b, w2b,
                       tm=tm, tf=1024).reshape(ys.shape)
    return (yp, ys, jnp.stack(new_c, axis=1), jnp.stack(new_n, axis=1), jnp.stack(new_m, axis=1))
```

```python
import functools

import jax
import jax.numpy as jnp
from jax import lax
from jax.experimental import pallas as pl
from jax.experimental.pallas import tpu as pltpu

F32 = jnp.float32
BF16 = jnp.bfloat16

D_MODEL = 2048
MLSTM_WIDTH = 1024
N_HEADS = 8
HEAD_DIM = 128
POOL_WIDTH = 1024
POOL_WINDOWS = (2, 4, 8, 16)
POOL_GROUP_DIM = 256
N_GATES = 4
D_FF = 8192
CHUNK = 128
N_MOD = 6
GRID_W = 64
EPS = 1e-6

LANES = 128
SUBLANES = 8
VMEM_LIMIT = 56 * 1024 * 1024

HEADS_PER_STEP = 4
GATE_ROWS = N_GATES * SUBLANES
GATE_STATS = 5
STAT_ROWS = 2 * GATE_STATS * SUBLANES
FINALIZE_UNROLL = 4


def _cparams(sem):
    return pltpu.CompilerParams(dimension_semantics=sem, vmem_limit_bytes=VMEM_LIMIT)


def _resident(shape):
    nd = len(shape)
    return pl.BlockSpec(shape, lambda *_: (0,) * nd, pipeline_mode=pl.Buffered(1))


def _rms(x, w):
    ms = jnp.mean(x * x, axis=-1, keepdims=True)
    return x * lax.rsqrt(ms + EPS) * w


def _mod_kernel(cond_ref, w_ref, b_ref, out_ref):
    c = cond_ref[...]
    s = c * jax.nn.sigmoid(c)
    out_ref[...] = jnp.dot(s.astype(BF16), w_ref[...].astype(BF16),
                           preferred_element_type=F32) + b_ref[...]


def _mod_call(cond, ada_w, ada_b):
    rows = cond.shape[0]
    n = ada_w.shape[1]
    tn = 1536
    return pl.pallas_call(
        _mod_kernel,
        grid=(n // tn,),
        in_specs=[pl.BlockSpec((rows, D_MODEL), lambda j: (0, 0)),
                  pl.BlockSpec((D_MODEL, tn), lambda j: (0, j)),
                  pl.BlockSpec((1, tn), lambda j: (0, j))],
        out_specs=pl.BlockSpec((rows, tn), lambda j: (0, j)),
        out_shape=jax.ShapeDtypeStruct((rows, n), F32),
        compiler_params=_cparams(("arbitrary",)),
        name="mod",
    )(cond, ada_w, ada_b.reshape(1, n))


NORM_SLICE_ROWS = 64


def _lane_scan_pair(fwd, bwd, op, ident):
    lane = lax.broadcasted_iota(jnp.int32, fwd.shape, 1)
    k = 1
    while k < LANES:
        f_shift = pltpu.roll(fwd, k, axis=1)
        b_shift = pltpu.roll(bwd, LANES - k, axis=1)
        fwd = op(fwd, jnp.where(lane >= k, f_shift, ident))
        bwd = op(bwd, jnp.where(lane < LANES - k, b_shift, ident))
        k *= 2
    return fwd, bwd


def _inproj_kernel(x_ref, mod_ref, nw_ref, wq_ref, wkg_ref, wv_ref, wo_ref, wu_ref, gb_ref, pw_ref, ps_ref,
                   q_ref, kt_ref, v_ref, og_ref, hp_ref, g_ref, *, seqs, chunks, seg):
    x = x_ref[...]
    mod = mod_ref[0]
    tm = x.shape[0]
    h = (_rms(x, nw_ref[0:1, :]) * (1.0 + mod[1:2, :]) + mod[0:1, :]).astype(BF16)
    ktg = lax.dot_general(wkg_ref[...], h, (((1,), (1,)), ((), ())), preferred_element_type=F32)
    kt = ktg[0:MLSTM_WIDTH].astype(BF16)
    gt = ktg[MLSTM_WIDTH:] + gb_ref[...]
    nhb = gt.shape[0] // GATE_ROWS
    for s in range(seqs):
        for c in range(chunks):
            lo = (s * chunks + c) * CHUNK
            kt_ref[s, c] = kt[:, lo:lo + CHUNK]
    def proj(wt_ref):
        return lax.dot_general(h, wt_ref[...], (((1,), (1,)), ((), ())), preferred_element_type=F32)

    u = proj(wu_ref)
    pos = lax.broadcasted_iota(jnp.int32, (tm, 1), 0) % seg

    def pool_group(gi):
        half = POOL_WINDOWS[gi] // 2
        inv_cnt = 1.0 / (jnp.minimum(pos + half, seg) - jnp.maximum(pos - half, 0)).astype(F32)
        cs = slice(gi * POOL_GROUP_DIM, (gi + 1) * POOL_GROUP_DIM)
        p = jnp.concatenate(
            [_pool_minus_self(u[:, lo:lo + LANES], half, pos, seg, inv_cnt).astype(BF16)
             for lo in range(cs.start, cs.stop, LANES)], axis=1)
        y = jnp.dot(p, pw_ref[gi], preferred_element_type=F32)
        hp_ref[:, cs] = (y * ps_ref[0:1, cs]).astype(BF16)

    q = proj(wq_ref) * (HEAD_DIM ** -0.5)
    q_ref[...] = q.astype(BF16)
    pool_group(0)
    pool_group(1)
    v_ref[...] = proj(wv_ref).astype(BF16)
    pool_group(2)
    og_ref[...] = jax.nn.sigmoid(proj(wo_ref))
    pool_group(3)

    groups = [(s, c, hbk) for s in range(seqs) for c in range(chunks) for hbk in range(nhb)]

    def gate(g):
        return jnp.concatenate(
            [gt[hbk * GATE_ROWS + g * SUBLANES:hbk * GATE_ROWS + (g + 1) * SUBLANES,
                (s * chunks + c) * CHUNK:(s * chunks + c + 1) * CHUNK] for s, c, hbk in groups], axis=0)

    li = (gate(0), gate(2))
    lf = (jax.nn.log_sigmoid(gate(1)), jax.nn.log_sigmoid(gate(3)))
    b = _lane_scan_pair(lf[0], lf[1], jnp.add, 0.0)
    a = (li[0] - b[0], li[1] - b[1])
    pm = _lane_scan_pair(a[0], a[1], jnp.maximum, jnp.float32(-jnp.inf))
    stats = []
    for d in range(2):
        amax = jnp.broadcast_to(jnp.max(a[d], axis=1, keepdims=True), a[d].shape)
        bsum = jnp.broadcast_to(jnp.sum(lf[d], axis=1, keepdims=True), a[d].shape)
        stats += [a[d], b[d], pm[d], amax, bsum]
    for gi, (s, c, hbk) in enumerate(groups):
        rs = slice(gi * SUBLANES, (gi + 1) * SUBLANES)
        g_ref[s, c, hbk * STAT_ROWS:(hbk + 1) * STAT_ROWS, :] = jnp.concatenate(
            [st[rs] for st in stats], axis=0)


def _inproj_call(x, mod, mod_row, norm_w, wts, pool_w, pool_scale, *, batch, seq_len, tm, seg):
    wt, wkg, wu, gb = wts
    row_block = lambda j: pl.BlockSpec((MLSTM_WIDTH, D_MODEL), lambda *_: (j, 0),
                                       pipeline_mode=pl.Buffered(1))
    tokens = batch * seq_len
    nc = seq_len // CHUNK
    grows = wkg.shape[0] - MLSTM_WIDTH
    srows = grows // GATE_ROWS * STAT_ROWS
    if tm >= seq_len:
        seqs, chunks = tm // seq_len, nc
        kt_map = lambda i: (i, 0, 0, 0)
    else:
        seqs, chunks = 1, tm // CHUNK
        per = seq_len // tm
        kt_map = lambda i: (i // per, i % per, 0, 0)
    tiles_per_seq = max(seq_len // tm, 1)
    if mod_row is None:
        mod_map = lambda i: (1 + i // tiles_per_seq, 0, 0)
    else:
        mod_map = lambda i: (mod_row, 0, 0)
    tok = lambda i: (i, 0)
    gbb = jnp.broadcast_to(gb[:, None], (grows, tm))
    kern = functools.partial(_inproj_kernel, seqs=seqs, chunks=chunks, seg=seg)
    return pl.pallas_call(
        kern,
        grid=(tokens // tm,),
        in_specs=[pl.BlockSpec((tm, D_MODEL), tok),
                  pl.BlockSpec((1, N_MOD, D_MODEL), mod_map),
                  _resident((4, D_MODEL)),
                  row_block(0), _resident(wkg.shape), row_block(2),
                  row_block(3), _resident(wu.shape),
                  _resident((grows, tm)),
                  _resident(pool_w.shape),
                  _resident((1, POOL_WIDTH))],
        out_specs=[pl.BlockSpec((tm, MLSTM_WIDTH), tok),
                   pl.BlockSpec((seqs, chunks, MLSTM_WIDTH, CHUNK), kt_map),
                   pl.BlockSpec((tm, MLSTM_WIDTH), tok),
                   pl.BlockSpec((tm, MLSTM_WIDTH), tok),
                   pl.BlockSpec((tm, POOL_WIDTH), tok),
                   pl.BlockSpec((seqs, chunks, srows, CHUNK), kt_map)],
        out_shape=[jax.ShapeDtypeStruct((tokens, MLSTM_WIDTH), BF16),
                   jax.ShapeDtypeStruct((batch, nc, MLSTM_WIDTH, CHUNK), BF16),
                   jax.ShapeDtypeStruct((tokens, MLSTM_WIDTH), BF16),
                   jax.ShapeDtypeStruct((tokens, MLSTM_WIDTH), F32),
                   jax.ShapeDtypeStruct((tokens, POOL_WIDTH), BF16),
                   jax.ShapeDtypeStruct((batch, nc, srows, CHUNK), F32)],
        compiler_params=_cparams(("arbitrary",)),
        name="in_proj",
    )(x, mod, norm_w, wt, wkg, wt, wt, wu, gbb, pool_w, pool_scale.reshape(1, POOL_WIDTH))


def _chunk_start(c):
    return c * CHUNK if isinstance(c, int) else pl.multiple_of(c * CHUNK, CHUNK)


def _mlstm_kernel(*refs, nc, hb, has_init, emit_state, cast_weight):
    it = iter(refs)
    q_ref, kt_ref, v_ref, o_ref, g_ref, nw_ref = (next(it) for _ in range(6))
    if has_init:
        c0_ref, n0_ref, m0_ref = (next(it) for _ in range(3))
    if cast_weight:
        wf_ref = next(it)
    out_ref = next(it)
    if emit_state:
        cout_ref, nout_ref, mout_ref = (next(it) for _ in range(3))
    if cast_weight:
        next(it)[...] = wf_ref[...].astype(BF16)
    hf_sc, hb_sc, cn_sc, m_sc, xt_sc, rows_sc = (next(it) for _ in range(6))

    if has_init:
        for d in range(2):
            for j in range(hb):
                cn_sc[d * hb + j, :, 0:HEAD_DIM] = c0_ref[0, d, j]
                n_row = n0_ref[0, 0, d][j:j + 1, :]
                cn_sc[d * hb + j, :, HEAD_DIM:] = jnp.broadcast_to(n_row, (HEAD_DIM, LANES)).T
    else:
        cn_sc[...] = jnp.zeros_like(cn_sc)

    neg_inf = jnp.float32(-jnp.inf)

    cols = []
    for d in range(2):
        def stat(k):
            lo = (d * GATE_STATS + k) * SUBLANES
            return g_ref[0, :, lo:lo + SUBLANES, :].reshape(nc * SUBLANES, LANES)
        a, b, pm, amax, bsum = (stat(k) for k in range(GATE_STATS))
        m = m0_ref[0, 0, d] if has_init else jnp.zeros((SUBLANES, LANES), F32)
        m_prev = [None] * nc
        for ci in (range(nc) if d == 0 else range(nc - 1, -1, -1)):
            rs = slice(ci * SUBLANES, (ci + 1) * SUBLANES)
            m_prev[ci] = m
            m = bsum[rs] + jnp.maximum(m, amax[rs])
        m_sc[d] = m
        mp = jnp.concatenate(m_prev, axis=0)
        mu = jnp.maximum(mp, pm)
        mu_last = jnp.maximum(mp, amax)
        rows_sc[0, d] = a.reshape(nc, SUBLANES, LANES)
        rows_sc[1, d] = jnp.exp(a - mu_last).reshape(nc, SUBLANES, LANES)
        rows_sc[2, d] = jnp.exp(mp - mu_last).reshape(nc, SUBLANES, LANES)
        cols.append((mu, jnp.exp(mp - mu), jnp.exp(-(b + mu))))
    zpad = jnp.zeros((LANES - 6 * SUBLANES, LANES), F32)
    for c in range(nc):
        pieces = []
        for d, ci in ((0, c), (1, nc - 1 - c)):
            pieces += [x[ci * SUBLANES:(ci + 1) * SUBLANES] for x in cols[d]]
        xt_sc[c] = jnp.concatenate(pieces + [zpad], axis=0).T

    row = lax.broadcasted_iota(jnp.int32, (CHUNK, CHUNK), 0)
    col = lax.broadcasted_iota(jnp.int32, (CHUNK, CHUNK), 1)
    causal = (col <= row, col >= row)
    ones_blk = jnp.ones((CHUNK, HEAD_DIM), BF16)

    def step(c, carry):
        chunk_of = (c, nc - 1 - c)
        xt = xt_sc[c]
        chains = []
        for d in range(2):
            ch = chunk_of[d]
            t0 = _chunk_start(ch)
            a_rows = rows_sc[0, d, ch]
            for j in range(hb):
                hs = slice(j * HEAD_DIM, (j + 1) * HEAD_DIM)
                qc = q_ref[0, pl.ds(t0, CHUNK), hs]
                base = 3 * d * SUBLANES + j
                mu_c = xt[:, base:base + 1]
                g_c = xt[:, base + SUBLANES:base + SUBLANES + 1]
                s = jnp.dot(qc, kt_ref[0, ch, hs, :], preferred_element_type=F32)
                p = s * jnp.exp(jnp.where(causal[d], a_rows[j:j + 1, :] - mu_c, neg_inf))
                gq = (qc.astype(F32) * g_c).astype(BF16)
                chains.append((d, j, ch, t0, jnp.concatenate([p.astype(BF16), gq], axis=1)))

        for d, j, ch, t0, lhs in chains:
            hs = slice(j * HEAD_DIM, (j + 1) * HEAD_DIM)
            base = 3 * d * SUBLANES + j
            en_c = xt[:, base + 2 * SUBLANES:base + 2 * SUBLANES + 1]
            v1 = jnp.concatenate([v_ref[0, pl.ds(t0, CHUNK), hs], ones_blk], axis=1)
            cn = cn_sc[d * hb + j]
            tot = jnp.dot(lhs, jnp.concatenate([v1, cn.astype(BF16)], axis=0),
                          preferred_element_type=F32)
            hval = tot[:, :HEAD_DIM] / jnp.maximum(jnp.abs(tot[:, HEAD_DIM:]), en_c)
            dst = hf_sc if d == 0 else hb_sc
            dst[pl.ds(t0, CHUNK), hs] = hval

            e_row = rows_sc[1, d, ch][j:j + 1, :]
            gs_row = rows_sc[2, d, ch][j:j + 1, :]
            wkt = (kt_ref[0, ch, hs, :].astype(F32) * e_row).astype(BF16)
            dcn = jnp.dot(wkt, v1, preferred_element_type=F32)
            gs_blk = jnp.broadcast_to(gs_row, (HEAD_DIM, LANES))
            cn_sc[d * hb + j] = jnp.concatenate([gs_blk, gs_blk], axis=1) * cn + dcn
        return carry

    if nc <= 2:
        for c in range(nc):
            step(c, 0)
    else:
        lax.fori_loop(0, nc, step, 0, unroll=2)

    def finalize(c, carry):
        t0 = _chunk_start(c)
        hsum = hf_sc[pl.ds(t0, CHUNK), :] + hb_sc[pl.ds(t0, CHUNK), :]
        og = o_ref[0, pl.ds(t0, CHUNK), :]
        for j in range(hb):
            hs = slice(j * HEAD_DIM, (j + 1) * HEAD_DIM)
            hn = _rms(hsum[:, hs], nw_ref[0:1, hs])
            out_ref[0, pl.ds(t0, CHUNK), hs] = (hn * og[:, hs]).astype(BF16)
        return carry

    if nc <= 2:
        for c in range(nc):
            finalize(c, 0)
    else:
        lax.fori_loop(0, nc, finalize, 0, unroll=FINALIZE_UNROLL)

    if emit_state:
        pad_rows = jnp.zeros((SUBLANES - hb, LANES), F32)
        for d in range(2):
            n_rows = []
            for j in range(hb):
                cout_ref[0, d, j] = cn_sc[d * hb + j, :, 0:HEAD_DIM]
                n_rows.append(cn_sc[d * hb + j, :, HEAD_DIM:].T[0:1, :])
            nout_ref[0, 0, d] = jnp.concatenate(n_rows + ([pad_rows] if hb < SUBLANES else []), axis=0)
            mout_ref[0, 0, d] = m_sc[d]


def _mlstm_call(q, kt, v, o, stats, nw, init, cast_weight=None, *, batch, seq_len, emit_state):
    hb = HEADS_PER_STEP
    nhb = N_HEADS // hb
    nc = seq_len // CHUNK
    w = hb * HEAD_DIM
    seq_map = lambda b, k: (b, 0, k)
    in_specs = [pl.BlockSpec((1, seq_len, w), seq_map),
                pl.BlockSpec((1, nc, w, CHUNK), lambda b, k: (b, 0, k, 0)),
                pl.BlockSpec((1, seq_len, w), seq_map),
                pl.BlockSpec((1, seq_len, w), seq_map),
                pl.BlockSpec((1, nc, STAT_ROWS, CHUNK), lambda b, k: (b, 0, k, 0)),
                pl.BlockSpec((1, w), lambda b, k: (0, k))]
    args = [q, kt, v, o, stats, nw]
    state_map = lambda b, k: (b, 0, k, 0, 0)
    vec_map = lambda b, k: (b, k, 0, 0, 0)
    vec_spec = pl.BlockSpec((1, 1, 2, SUBLANES, LANES), vec_map)
    if init is not None:
        c0, n0, m0 = init
        in_specs += [pl.BlockSpec((1, 2, hb, HEAD_DIM, HEAD_DIM), state_map), vec_spec, vec_spec]
        args += [c0, n0, m0]
    out_specs = [pl.BlockSpec((1, seq_len, w), seq_map)]
    out_shape = [jax.ShapeDtypeStruct((batch, seq_len, MLSTM_WIDTH), BF16)]
    if emit_state:
        out_specs += [pl.BlockSpec((1, 2, hb, HEAD_DIM, HEAD_DIM), state_map), vec_spec, vec_spec]
        out_shape += [jax.ShapeDtypeStruct((batch, 2, N_HEADS, HEAD_DIM, HEAD_DIM), F32),
                      jax.ShapeDtypeStruct((batch, nhb, 2, SUBLANES, LANES), F32),
                      jax.ShapeDtypeStruct((batch, nhb, 2, SUBLANES, LANES), F32)]
    if cast_weight is not None:
        slab = cast_weight.shape[0] // (batch * nhb)
        assert slab * batch * nhb == cast_weight.shape[0] and slab % (2 * SUBLANES) == 0
        slab_spec = pl.BlockSpec((slab, D_MODEL), lambda b, k: (b * nhb + k, 0))
        in_specs.append(slab_spec)
        args.append(cast_weight)
        out_specs.append(slab_spec)
        out_shape.append(jax.ShapeDtypeStruct(cast_weight.shape, BF16))
    kern = functools.partial(_mlstm_kernel, nc=nc, hb=hb, has_init=init is not None,
                             emit_state=emit_state, cast_weight=cast_weight is not None)
    return pl.pallas_call(
        kern,
        grid=(batch, nhb),
        in_specs=in_specs,
        out_specs=out_specs,
        out_shape=out_shape,
        scratch_shapes=[pltpu.VMEM((seq_len, w), F32),
                        pltpu.VMEM((seq_len, w), F32),
                        pltpu.VMEM((2 * hb, HEAD_DIM, 2 * HEAD_DIM), F32),
                        pltpu.VMEM((2, SUBLANES, LANES), F32),
                        pltpu.VMEM((nc, LANES, LANES), F32),
                        pltpu.VMEM((3, 2, nc, SUBLANES, LANES), F32)],
        compiler_params=_cparams(("arbitrary", "arbitrary")),
        name="mlstm",
    )(*args)


def _shift_rows(x, dlt):
    return pltpu.roll(x, (-dlt) % x.shape[0], axis=0)


def _pool_minus_self(x, half, pos, seg, inv_cnt):
    fwd = x
    k = 1
    while k < half:
        fwd = fwd + jnp.where(pos + k < seg, _shift_rows(fwd, k), 0.0)
        k *= 2
    bwd = jnp.where(pos >= 1, _shift_rows(x, -1), 0.0)
    k = 1
    while k < half:
        bwd = bwd + jnp.where(pos >= k, _shift_rows(bwd, -k), 0.0)
        k *= 2
    return (fwd + bwd) * inv_cnt - x


OUTPROJ_ROW_BLOCKS = 4
OUTPROJ_EPI_ROWS = 64
OUTPROJ_SLICE_COLS = 256


def _outproj_kernel(*refs, cast_mlp_weights):
    hm_ref, hp_ref, x_ref, mod_ref, nw_ref, wout_ref = refs[:6]
    if cast_mlp_weights:
        w1_ref, x1_ref, h2_ref, w1b_ref, mix_ref = refs[6:]
        w1b_ref[...] = w1_ref[...].astype(BF16)
    else:
        x1_ref, h2_ref, mix_ref = refs[6:]
    tm = x_ref.shape[0]
    nblk = OUTPROJ_ROW_BLOCKS
    rb = tm // nblk
    nslice = D_MODEL // OUTPROJ_SLICE_COLS
    mod = mod_ref[0]
    gain1 = nw_ref[1:2, :] * mod[2:3, :]
    gain2 = nw_ref[2:3, :] * (1.0 + mod[4:5, :])
    shift2 = mod[3:4, :]

    def epilogue_slices(r):
        def one(lo):
            rows = slice(lo, lo + OUTPROJ_EPI_ROWS)
            mix = mix_ref[rows, :]
            ms = jnp.mean(mix * mix, axis=-1, keepdims=True)
            x1 = x_ref[rows, :] + (mix * lax.rsqrt(ms + EPS)) * gain1
            x1_ref[rows, :] = x1
            ms2 = jnp.mean(x1 * x1, axis=-1, keepdims=True)
            h2_ref[rows, :] = ((x1 * lax.rsqrt(ms2 + EPS)) * gain2 + shift2).astype(BF16)
        return [functools.partial(one, r * rb + k * OUTPROJ_EPI_ROWS)
                for k in range(rb // OUTPROJ_EPI_ROWS)]

    def run_block(r, pending):
        rows = slice(r * rb, (r + 1) * rb)
        lhs = jnp.concatenate([hm_ref[rows, :], hp_ref[rows, :]], axis=1)
        for idx in range(nslice):
            ns = slice(idx * OUTPROJ_SLICE_COLS, (idx + 1) * OUTPROJ_SLICE_COLS)
            mix_ref[rows, ns] = jnp.dot(lhs, wout_ref[:, ns], preferred_element_type=F32)
            if pending and idx % 4 == 3:
                pending.pop(0)()
        while pending:
            pending.pop(0)()

    pending = []
    for r in range(nblk):
        run_block(r, pending)
        pending = epilogue_slices(r)
    for fn in pending:
        fn()


def _outproj_call(hm, hp, x, mod, mod_map, norm_w, w_out, cast_w1=None, *, tm):
    tokens = x.shape[0]
    steps = tokens // tm
    tok = lambda i: (i, 0)
    in_specs = [pl.BlockSpec((tm, MLSTM_WIDTH), tok),
                pl.BlockSpec((tm, POOL_WIDTH), tok),
                pl.BlockSpec((tm, D_MODEL), tok),
                pl.BlockSpec((1, N_MOD, D_MODEL), mod_map),
                _resident((4, D_MODEL)),
                _resident(w_out.shape)]
    out_specs = [pl.BlockSpec((tm, D_MODEL), tok),
                 pl.BlockSpec((tm, D_MODEL), tok)]
    out_shape = [jax.ShapeDtypeStruct((tokens, D_MODEL), F32),
                 jax.ShapeDtypeStruct((tokens, D_MODEL), BF16)]
    args = [hm, hp, x, mod, norm_w, w_out]
    if cast_w1 is not None:
        slab = D_FF // steps
        assert slab * steps == D_FF and slab % LANES == 0
        slab_spec = pl.BlockSpec((D_MODEL, slab), lambda i: (0, i))
        in_specs.append(slab_spec)
        out_specs.append(slab_spec)
        out_shape.append(jax.ShapeDtypeStruct((D_MODEL, D_FF), BF16))
        args.append(cast_w1)
    return pl.pallas_call(
        functools.partial(_outproj_kernel, cast_mlp_weights=cast_w1 is not None),
        grid=(steps,),
        in_specs=in_specs,
        out_specs=out_specs,
        out_shape=out_shape,
        scratch_shapes=[pltpu.VMEM((tm, D_MODEL), F32)],
        compiler_params=_cparams(("arbitrary",)),
        name="out_proj",
    )(*args)


MLP_LAST_ROW_BLOCKS = 4
MLP_SLICE_COLS = 256


def _mlp_kernel(h_ref, w1_ref, w2_ref, x1_ref, mod_ref, nw_ref, out_ref, acc_ref):
    k = pl.program_id(1)
    last = pl.num_programs(1) - 1
    tm, tf = h_ref.shape[0], w1_ref.shape[1]

    def hidden(rows, cols=slice(None)):
        a = jnp.dot(h_ref[rows, :], w1_ref[:, cols], preferred_element_type=F32)
        return jnp.square(jnp.maximum(a, 0.0)).astype(BF16)

    @pl.when(k == 0)
    def _():
        acc_ref[...] = jnp.dot(hidden(slice(None)), w2_ref[...], preferred_element_type=F32)

    @pl.when((k > 0) & (k < last))
    def _():
        acc_ref[...] += jnp.dot(hidden(slice(None)), w2_ref[...], preferred_element_type=F32)

    @pl.when(k == last)
    def _():
        rb = tm // MLP_LAST_ROW_BLOCKS
        gain = nw_ref[3:4, :] * mod_ref[0, 5:6, :]

        def epilogue_slices(r):
            def one(lo):
                rows = slice(lo, lo + NORM_SLICE_ROWS)
                y = acc_ref[rows, :]
                ms = jnp.mean(y * y, axis=-1, keepdims=True)
                out_ref[rows, :] = x1_ref[rows, :] + (y * lax.rsqrt(ms + EPS)) * gain
            return [functools.partial(one, r * rb + j * NORM_SLICE_ROWS)
                    for j in range(rb // NORM_SLICE_ROWS)]

        pending = []
        n1, n2 = tf // MLP_SLICE_COLS, D_MODEL // MLP_SLICE_COLS
        for r in range(MLP_LAST_ROW_BLOCKS):
            rows = slice(r * rb, (r + 1) * rb)
            parts = []
            for idx in range(n1):
                parts.append(hidden(rows, slice(idx * MLP_SLICE_COLS, (idx + 1) * MLP_SLICE_COLS)))
                if pending and idx % 2 == 1:
                    pending.pop(0)()
            a = jnp.concatenate(parts, axis=1)
            for idx in range(n2):
                ns = slice(idx * MLP_SLICE_COLS, (idx + 1) * MLP_SLICE_COLS)
                acc_ref[rows, ns] += jnp.dot(a, w2_ref[:, ns], preferred_element_type=F32)
                if pending and idx % 2 == 1:
                    pending.pop(0)()
            while pending:
                pending.pop(0)()
            pending = epilogue_slices(r)
        for fn in pending:
            fn()


def _mlp_call(h2, x1, mod, mod_map, norm_w, w1, w2, *, tm, tf):
    tokens = x1.shape[0]
    tok = lambda i, k: (i, 0)
    return pl.pallas_call(
        _mlp_kernel,
        grid=(tokens // tm, D_FF // tf),
        in_specs=[pl.BlockSpec((tm, D_MODEL), tok),
                  pl.BlockSpec((D_MODEL, tf), lambda i, k: (0, k)),
                  pl.BlockSpec((tf, D_MODEL), lambda i, k: (k, 0)),
                  pl.BlockSpec((tm, D_MODEL), tok),
                  pl.BlockSpec((1, N_MOD, D_MODEL), lambda i, k: mod_map(i)),
                  pl.BlockSpec((4, D_MODEL), lambda i, k: (0, 0))],
        out_specs=pl.BlockSpec((tm, D_MODEL), tok),
        out_shape=jax.ShapeDtypeStruct((tokens, D_MODEL), F32),
        scratch_shapes=[pltpu.VMEM((tm, D_MODEL), F32)],
        compiler_params=_cparams(("arbitrary", "arbitrary")),
        name="mlp",
    )(h2, w1, w2, x1, mod, norm_w)


def _gate_weights(w_g, gate_bias):
    hb = HEADS_PER_STEP
    nhb = N_HEADS // hb
    wg = w_g.reshape(D_MODEL, N_GATES, nhb, hb)
    wg = jnp.pad(wg, ((0, 0), (0, 0), (0, 0), (0, SUBLANES - hb)))
    wgt = wg.transpose(2, 1, 3, 0).reshape(nhb * GATE_ROWS, D_MODEL)
    gb = jnp.pad(gate_bias.reshape(N_GATES, nhb, hb), ((0, 0), (0, 0), (0, SUBLANES - hb)))
    gb = gb.transpose(1, 0, 2).reshape(nhb * GATE_ROWS)
    return wgt.astype(BF16), gb.astype(F32)


def _pack_head_rows(x):
    hb = HEADS_PER_STEP
    nhb = N_HEADS // hb
    x = x.reshape(x.shape[:2] + (nhb, hb) + x.shape[3:])
    x = jnp.moveaxis(x, 2, 1)
    pad = [(0, 0)] * x.ndim
    pad[3] = (0, SUBLANES - hb)
    return jnp.pad(x, pad)


def _unpack_head_rows(x):
    x = jnp.moveaxis(x[:, :, :, :HEADS_PER_STEP], 1, 2)
    return x.reshape(x.shape[:2] + (N_HEADS,) + x.shape[4:])


def _mod_map(mod_row, seq_len, tm):
    if mod_row is None:
        per = seq_len // tm
        return lambda i: (1 + i // per, 0, 0)
    return lambda i: (mod_row, 0, 0)


def _mixer_stage(x, mod, mod_row, init, params, w_out, cast_rows, cast_w1=None, *, seg, tm, emit_state):
    batch, seq_len, _ = x.shape
    wts, mlstm_nw, pool_w, pool_scale, norm_w = params
    tokens = batch * seq_len
    xf = x.reshape(tokens, D_MODEL)
    q, kt, v, og, hp, stats = _inproj_call(xf, mod, mod_row, norm_w, wts, pool_w, pool_scale,
                                           batch=batch, seq_len=seq_len, tm=tm, seg=seg)
    shp = (batch, seq_len, MLSTM_WIDTH)
    res = _mlstm_call(q.reshape(shp), kt, v.reshape(shp), og.reshape(shp), stats, mlstm_nw, init,
                      cast_rows, batch=batch, seq_len=seq_len, emit_state=emit_state)
    hm = res[0].reshape(tokens, MLSTM_WIDTH)
    outs = _outproj_call(hm, hp, xf, mod, _mod_map(mod_row, seq_len, tm), norm_w,
                         res[-1] if w_out is None else w_out, cast_w1, tm=tm)
    return outs, res[1:]


def kernel(x_prompt, x_sample, c, state_C, state_n, state_m, c_ctx, w_in, gate_bias, mlstm_norm_w,
           pool_w, pool_scale, w_out, ada_w, ada_b, norm_w, w1, w2):
    m4 = 4 * MLSTM_WIDTH
    yp, ys = x_prompt, x_sample
    new_c, new_n, new_m = [], [], []
    for layer in range(w_in.shape[0]):
        wt = w_in[layer].T.astype(BF16)
        wgt, gb = _gate_weights(w_in[layer][:, m4:m4 + N_GATES * N_HEADS], gate_bias[layer])
        wkg = jnp.concatenate([wt[MLSTM_WIDTH:2 * MLSTM_WIDTH], wgt], axis=0)
        wts = (wt, wkg, wt[m4 + N_GATES * N_HEADS:], gb)
        params = (wts, mlstm_norm_w[layer].reshape(1, MLSTM_WIDTH), pool_w[layer].astype(BF16),
                  pool_scale[layer], norm_w[layer])

        n_lat = c.shape[0]
        rows = 16
        cond = jnp.concatenate([c_ctx[None, :], c, jnp.zeros((rows - 1 - n_lat, D_MODEL), F32)], axis=0)
        mod = _mod_call(cond, ada_w[layer], ada_b[layer]).reshape(rows, N_MOD, D_MODEL)

        tm = 512
        (x1p, h2p), (cc, nn, mm, w_out_b) = _mixer_stage(yp, mod, 0, None, params, None, w_out[layer],
                                                         seg=yp.shape[1], tm=tm, emit_state=True)
        new_c.append(cc)
        new_n.append(_unpack_head_rows(nn))
        new_m.append(_unpack_head_rows(mm)[..., 0])

        sc = state_C[:, layer].astype(F32)
        sn = _pack_head_rows(state_n[:, layer].astype(F32))
        sm = _pack_head_rows(jnp.broadcast_to(state_m[:, layer].astype(F32)[..., None],
                                              (n_lat, 2, N_HEADS, LANES)))
        (x1s, h2s, w1b), (w2b,) = _mixer_stage(ys, mod, None, (sc, sn, sm), params, w_out_b, w2[layer],
                                               w1[layer], seg=GRID_W, tm=tm, emit_state=False)
        yp = _mlp_call(h2p, x1p, mod, _mod_map(0, yp.shape[1], tm), norm_w[layer], w1b, w2b,
                       tm=tm, tf=1024).reshape(yp.shape)
        ys = _mlp_call(h2s, x1s, mod, _mod_map(None, ys.shape[1], tm), norm_w[layer], w1b, w2b,
                       tm=tm, tf=1024).reshape(ys.shape)
    return (yp, ys, jnp.stack(new_c, axis=1), jnp.stack(new_n, axis=1), jnp.stack(new_m, axis=1))
```

```python
import functools

import jax
import jax.numpy as jnp
from jax import lax
from jax.experimental import pallas as pl
from jax.experimental.pallas import tpu as pltpu

F32 = jnp.float32
BF16 = jnp.bfloat16

D_MODEL = 2048
MLSTM_WIDTH = 1024
N_HEADS = 8
HEAD_DIM = 128
POOL_WIDTH = 1024
POOL_WINDOWS = (2, 4, 8, 16)
POOL_GROUP_DIM = 256
N_GATES = 4
D_FF = 8192
CHUNK = 128
N_MOD = 6
GRID_W = 64
EPS = 1e-6

LANES = 128
SUBLANES = 8
VMEM_LIMIT = 56 * 1024 * 1024

TOKEN_TILE = 512
MLP_TF = 1024
MOD_TN = 1536
HEADS_PER_STEP = 4
GATE_ROWS = N_GATES * SUBLANES
GATE_STATS = 5
STAT_ROWS = 2 * GATE_STATS * SUBLANES
FINALIZE_UNROLL = 4


def _cparams(sem):
    return pltpu.CompilerParams(dimension_semantics=sem, vmem_limit_bytes=VMEM_LIMIT)


def _resident(shape):
    nd = len(shape)
    return pl.BlockSpec(shape, lambda *_: (0,) * nd, pipeline_mode=pl.Buffered(1))


def _rms(x, w):
    ms = jnp.mean(x * x, axis=-1, keepdims=True)
    return x * lax.rsqrt(ms + EPS) * w


def _mod_kernel(cond_ref, w_ref, b_ref, out_ref):
    c = cond_ref[...]
    s = c * jax.nn.sigmoid(c)
    out_ref[...] = jnp.dot(s.astype(BF16), w_ref[...].astype(BF16),
                           preferred_element_type=F32) + b_ref[...]


def _mod_call(cond, ada_w, ada_b):
    rows = cond.shape[0]
    n = ada_w.shape[1]
    tn = MOD_TN
    return pl.pallas_call(
        _mod_kernel,
        grid=(n // tn,),
        in_specs=[pl.BlockSpec((rows, D_MODEL), lambda j: (0, 0)),
                  pl.BlockSpec((D_MODEL, tn), lambda j: (0, j)),
                  pl.BlockSpec((1, tn), lambda j: (0, j))],
        out_specs=pl.BlockSpec((rows, tn), lambda j: (0, j)),
        out_shape=jax.ShapeDtypeStruct((rows, n), F32),
        compiler_params=_cparams(("arbitrary",)),
        name="mod",
    )(cond, ada_w, ada_b.reshape(1, n))


NORM_SLICE_ROWS = 64


def _lane_scan_pair(fwd, bwd, op, ident):
    lane = lax.broadcasted_iota(jnp.int32, fwd.shape, 1)
    k = 1
    while k < LANES:
        f_shift = pltpu.roll(fwd, k, axis=1)
        b_shift = pltpu.roll(bwd, LANES - k, axis=1)
        fwd = op(fwd, jnp.where(lane >= k, f_shift, ident))
        bwd = op(bwd, jnp.where(lane < LANES - k, b_shift, ident))
        k *= 2
    return fwd, bwd


def _inproj_kernel(x_ref, mod_ref, nw_ref, wq_ref, wkg_ref, wv_ref, wo_ref, wu_ref, gb_ref, pw_ref, ps_ref,
                   q_ref, kt_ref, v_ref, og_ref, hp_ref, g_ref, *, seqs, chunks, seg):
    x = x_ref[...]
    mod = mod_ref[0]
    tm = x.shape[0]
    h = (_rms(x, nw_ref[0:1, :]) * (1.0 + mod[1:2, :]) + mod[0:1, :]).astype(BF16)
    ktg = lax.dot_general(wkg_ref[...], h, (((1,), (1,)), ((), ())), preferred_element_type=F32)
    kt = ktg[0:MLSTM_WIDTH].astype(BF16)
    gt = ktg[MLSTM_WIDTH:] + gb_ref[...]
    nhb = gt.shape[0] // GATE_ROWS
    for s in range(seqs):
        for c in range(chunks):
            lo = (s * chunks + c) * CHUNK
            kt_ref[s, c] = kt[:, lo:lo + CHUNK]
    def proj(wt_ref):
        return lax.dot_general(h, wt_ref[...], (((1,), (1,)), ((), ())), preferred_element_type=F32)

    u = proj(wu_ref)
    pos = lax.broadcasted_iota(jnp.int32, (tm, 1), 0) % seg

    def pool_group(gi):
        half = POOL_WINDOWS[gi] // 2
        inv_cnt = 1.0 / (jnp.minimum(pos + half, seg) - jnp.maximum(pos - half, 0)).astype(F32)
        cs = slice(gi * POOL_GROUP_DIM, (gi + 1) * POOL_GROUP_DIM)
        p = jnp.concatenate(
            [_pool_minus_self(u[:, lo:lo + LANES], half, pos, seg, inv_cnt).astype(BF16)
             for lo in range(cs.start, cs.stop, LANES)], axis=1)
        y = jnp.dot(p, pw_ref[gi], preferred_element_type=F32)
        hp_ref[:, cs] = (y * ps_ref[0:1, cs]).astype(BF16)

    q = proj(wq_ref) * (HEAD_DIM ** -0.5)
    q_ref[...] = q.astype(BF16)
    pool_group(0)
    pool_group(1)
    v_ref[...] = proj(wv_ref).astype(BF16)
    pool_group(2)
    og_ref[...] = jax.nn.sigmoid(proj(wo_ref))
    pool_group(3)

    groups = [(s, c, hbk) for s in range(seqs) for c in range(chunks) for hbk in range(nhb)]

    def gate(g):
        return jnp.concatenate(
            [gt[hbk * GATE_ROWS + g * SUBLANES:hbk * GATE_ROWS + (g + 1) * SUBLANES,
                (s * chunks + c) * CHUNK:(s * chunks + c + 1) * CHUNK] for s, c, hbk in groups], axis=0)

    li = (gate(0), gate(2))
    lf = (jax.nn.log_sigmoid(gate(1)), jax.nn.log_sigmoid(gate(3)))
    b = _lane_scan_pair(lf[0], lf[1], jnp.add, 0.0)
    a = (li[0] - b[0], li[1] - b[1])
    pm = _lane_scan_pair(a[0], a[1], jnp.maximum, jnp.float32(-jnp.inf))
    stats = []
    for d in range(2):
        amax = jnp.broadcast_to(jnp.max(a[d], axis=1, keepdims=True), a[d].shape)
        bsum = jnp.broadcast_to(jnp.sum(lf[d], axis=1, keepdims=True), a[d].shape)
        stats += [a[d], b[d], pm[d], amax, bsum]
    for gi, (s, c, hbk) in enumerate(groups):
        rs = slice(gi * SUBLANES, (gi + 1) * SUBLANES)
        g_ref[s, c, hbk * STAT_ROWS:(hbk + 1) * STAT_ROWS, :] = jnp.concatenate(
            [st[rs] for st in stats], axis=0)


def _inproj_call(x, mod, mod_row, norm_w, wts, pool_w, pool_scale, *, batch, seq_len, tm, seg):
    wt, wkg, wu, gb = wts
    row_block = lambda j: pl.BlockSpec((MLSTM_WIDTH, D_MODEL), lambda *_: (j, 0),
                                       pipeline_mode=pl.Buffered(1))
    tokens = batch * seq_len
    nc = seq_len // CHUNK
    grows = wkg.shape[0] - MLSTM_WIDTH
    srows = grows // GATE_ROWS * STAT_ROWS
    if tm >= seq_len:
        seqs, chunks = tm // seq_len, nc
        kt_map = lambda i: (i, 0, 0, 0)
    else:
        seqs, chunks = 1, tm // CHUNK
        per = seq_len // tm
        kt_map = lambda i: (i // per, i % per, 0, 0)
    tiles_per_seq = max(seq_len // tm, 1)
    if mod_row is None:
        mod_map = lambda i: (1 + i // tiles_per_seq, 0, 0)
    else:
        mod_map = lambda i: (mod_row, 0, 0)
    tok = lambda i: (i, 0)
    gbb = jnp.broadcast_to(gb[:, None], (grows, tm))
    kern = functools.partial(_inproj_kernel, seqs=seqs, chunks=chunks, seg=seg)
    return pl.pallas_call(
        kern,
        grid=(tokens // tm,),
        in_specs=[pl.BlockSpec((tm, D_MODEL), tok),
                  pl.BlockSpec((1, N_MOD, D_MODEL), mod_map),
                  _resident((4, D_MODEL)),
                  row_block(0), _resident(wkg.shape), row_block(2),
                  row_block(3), _resident(wu.shape),
                  _resident((grows, tm)),
                  _resident(pool_w.shape),
                  _resident((1, POOL_WIDTH))],
        out_specs=[pl.BlockSpec((tm, MLSTM_WIDTH), tok),
                   pl.BlockSpec((seqs, chunks, MLSTM_WIDTH, CHUNK), kt_map),
                   pl.BlockSpec((tm, MLSTM_WIDTH), tok),
                   pl.BlockSpec((tm, MLSTM_WIDTH), tok),
                   pl.BlockSpec((tm, POOL_WIDTH), tok),
                   pl.BlockSpec((seqs, chunks, srows, CHUNK), kt_map)],
        out_shape=[jax.ShapeDtypeStruct((tokens, MLSTM_WIDTH), BF16),
                   jax.ShapeDtypeStruct((batch, nc, MLSTM_WIDTH, CHUNK), BF16),
                   jax.ShapeDtypeStruct((tokens, MLSTM_WIDTH), BF16),
                   jax.ShapeDtypeStruct((tokens, MLSTM_WIDTH), F32),
                   jax.ShapeDtypeStruct((tokens, POOL_WIDTH), BF16),
                   jax.ShapeDtypeStruct((batch, nc, srows, CHUNK), F32)],
        compiler_params=_cparams(("arbitrary",)),
        name="in_proj",
    )(x, mod, norm_w, wt, wkg, wt, wt, wu, gbb, pool_w, pool_scale.reshape(1, POOL_WIDTH))


def _chunk_start(c):
    return c * CHUNK if isinstance(c, int) else pl.multiple_of(c * CHUNK, CHUNK)


def _mlstm_kernel(*refs, nc, hb, has_init, emit_state, cast_weight):
    it = iter(refs)
    q_ref, kt_ref, v_ref, o_ref, g_ref, nw_ref = (next(it) for _ in range(6))
    if has_init:
        c0_ref, n0_ref, m0_ref = (next(it) for _ in range(3))
    if cast_weight:
        wf_ref = next(it)
    out_ref = next(it)
    if emit_state:
        cout_ref, nout_ref, mout_ref = (next(it) for _ in range(3))
    if cast_weight:
        next(it)[...] = wf_ref[...].astype(BF16)
    hf_sc, hb_sc, cn_sc, m_sc, xt_sc, rows_sc = (next(it) for _ in range(6))

    if has_init:
        for d in range(2):
            for j in range(hb):
                cn_sc[d * hb + j, :, 0:HEAD_DIM] = c0_ref[0, d, j]
                n_row = n0_ref[0, 0, d][j:j + 1, :]
                cn_sc[d * hb + j, :, HEAD_DIM:] = jnp.broadcast_to(n_row, (HEAD_DIM, LANES)).T
    else:
        cn_sc[...] = jnp.zeros_like(cn_sc)

    neg_inf = jnp.float32(-jnp.inf)

    cols = []
    for d in range(2):
        def stat(k):
            lo = (d * GATE_STATS + k) * SUBLANES
            return g_ref[0, :, lo:lo + SUBLANES, :].reshape(nc * SUBLANES, LANES)
        a, b, pm, amax, bsum = (stat(k) for k in range(GATE_STATS))
        m = m0_ref[0, 0, d] if has_init else jnp.zeros((SUBLANES, LANES), F32)
        m_prev = [None] * nc
        for ci in (range(nc) if d == 0 else range(nc - 1, -1, -1)):
            rs = slice(ci * SUBLANES, (ci + 1) * SUBLANES)
            m_prev[ci] = m
            m = bsum[rs] + jnp.maximum(m, amax[rs])
        m_sc[d] = m
        mp = jnp.concatenate(m_prev, axis=0)
        mu = jnp.maximum(mp, pm)
        mu_last = jnp.maximum(mp, amax)
        rows_sc[0, d] = a.reshape(nc, SUBLANES, LANES)
        rows_sc[1, d] = jnp.exp(a - mu_last).reshape(nc, SUBLANES, LANES)
        rows_sc[2, d] = jnp.exp(mp - mu_last).reshape(nc, SUBLANES, LANES)
        cols.append((mu, jnp.exp(mp - mu), jnp.exp(-(b + mu))))
    zpad = jnp.zeros((LANES - 6 * SUBLANES, LANES), F32)
    for c in range(nc):
        pieces = []
        for d, ci in ((0, c), (1, nc - 1 - c)):
            pieces += [x[ci * SUBLANES:(ci + 1) * SUBLANES] for x in cols[d]]
        xt_sc[c] = jnp.concatenate(pieces + [zpad], axis=0).T

    row = lax.broadcasted_iota(jnp.int32, (CHUNK, CHUNK), 0)
    col = lax.broadcasted_iota(jnp.int32, (CHUNK, CHUNK), 1)
    causal = (col <= row, col >= row)
    ones_blk = jnp.ones((CHUNK, HEAD_DIM), BF16)

    def step(c, carry):
        chunk_of = (c, nc - 1 - c)
        xt = xt_sc[c]
        chains = []
        for d in range(2):
            ch = chunk_of[d]
            t0 = _chunk_start(ch)
            a_rows = rows_sc[0, d, ch]
            for j in range(hb):
                hs = slice(j * HEAD_DIM, (j + 1) * HEAD_DIM)
                qc = q_ref[0, pl.ds(t0, CHUNK), hs]
                base = 3 * d * SUBLANES + j
                mu_c = xt[:, base:base + 1]
                g_c = xt[:, base + SUBLANES:base + SUBLANES + 1]
                s = jnp.dot(qc, kt_ref[0, ch, hs, :], preferred_element_type=F32)
                p = s * jnp.exp(jnp.where(causal[d], a_rows[j:j + 1, :] - mu_c, neg_inf))
                gq = (qc.astype(F32) * g_c).astype(BF16)
                chains.append((d, j, ch, t0, jnp.concatenate([p.astype(BF16), gq], axis=1)))

        for d, j, ch, t0, lhs in chains:
            hs = slice(j * HEAD_DIM, (j + 1) * HEAD_DIM)
            base = 3 * d * SUBLANES + j
            en_c = xt[:, base + 2 * SUBLANES:base + 2 * SUBLANES + 1]
            v1 = jnp.concatenate([v_ref[0, pl.ds(t0, CHUNK), hs], ones_blk], axis=1)
            cn = cn_sc[d * hb + j]
            tot = jnp.dot(lhs, jnp.concatenate([v1, cn.astype(BF16)], axis=0),
                          preferred_element_type=F32)
            hval = tot[:, :HEAD_DIM] / jnp.maximum(jnp.abs(tot[:, HEAD_DIM:]), en_c)
            dst = hf_sc if d == 0 else hb_sc
            dst[pl.ds(t0, CHUNK), hs] = hval

            e_row = rows_sc[1, d, ch][j:j + 1, :]
            gs_row = rows_sc[2, d, ch][j:j + 1, :]
            wkt = (kt_ref[0, ch, hs, :].astype(F32) * e_row).astype(BF16)
            dcn = jnp.dot(wkt, v1, preferred_element_type=F32)
            gs_blk = jnp.broadcast_to(gs_row, (HEAD_DIM, LANES))
            cn_sc[d * hb + j] = jnp.concatenate([gs_blk, gs_blk], axis=1) * cn + dcn
        return carry

    if nc <= 2:
        for c in range(nc):
            step(c, 0)
    else:
        lax.fori_loop(0, nc, step, 0, unroll=2)

    def finalize(c, carry):
        t0 = _chunk_start(c)
        hsum = hf_sc[pl.ds(t0, CHUNK), :] + hb_sc[pl.ds(t0, CHUNK), :]
        og = o_ref[0, pl.ds(t0, CHUNK), :]
        for j in range(hb):
            hs = slice(j * HEAD_DIM, (j + 1) * HEAD_DIM)
            hn = _rms(hsum[:, hs], nw_ref[0:1, hs])
            out_ref[0, pl.ds(t0, CHUNK), hs] = (hn * og[:, hs]).astype(BF16)
        return carry

    if nc <= 2:
        for c in range(nc):
            finalize(c, 0)
    else:
        lax.fori_loop(0, nc, finalize, 0, unroll=FINALIZE_UNROLL)

    if emit_state:
        pad_rows = jnp.zeros((SUBLANES - hb, LANES), F32)
        for d in range(2):
            n_rows = []
            for j in range(hb):
                cout_ref[0, d, j] = cn_sc[d * hb + j, :, 0:HEAD_DIM]
                n_rows.append(cn_sc[d * hb + j, :, HEAD_DIM:].T[0:1, :])
            nout_ref[0, 0, d] = jnp.concatenate(n_rows + ([pad_rows] if hb < SUBLANES else []), axis=0)
            mout_ref[0, 0, d] = m_sc[d]


def _mlstm_call(q, kt, v, o, stats, nw, init, cast_weight=None, *, batch, seq_len, emit_state):
    hb = HEADS_PER_STEP
    nhb = N_HEADS // hb
    nc = seq_len // CHUNK
    w = hb * HEAD_DIM
    seq_map = lambda b, k: (b, 0, k)
    in_specs = [pl.BlockSpec((1, seq_len, w), seq_map),
                pl.BlockSpec((1, nc, w, CHUNK), lambda b, k: (b, 0, k, 0)),
                pl.BlockSpec((1, seq_len, w), seq_map),
                pl.BlockSpec((1, seq_len, w), seq_map),
                pl.BlockSpec((1, nc, STAT_ROWS, CHUNK), lambda b, k: (b, 0, k, 0)),
                pl.BlockSpec((1, w), lambda b, k: (0, k))]
    args = [q, kt, v, o, stats, nw]
    state_map = lambda b, k: (b, 0, k, 0, 0)
    vec_map = lambda b, k: (b, k, 0, 0, 0)
    vec_spec = pl.BlockSpec((1, 1, 2, SUBLANES, LANES), vec_map)
    if init is not None:
        c0, n0, m0 = init
        in_specs += [pl.BlockSpec((1, 2, hb, HEAD_DIM, HEAD_DIM), state_map), vec_spec, vec_spec]
        args += [c0, n0, m0]
    out_specs = [pl.BlockSpec((1, seq_len, w), seq_map)]
    out_shape = [jax.ShapeDtypeStruct((batch, seq_len, MLSTM_WIDTH), BF16)]
    if emit_state:
        out_specs += [pl.BlockSpec((1, 2, hb, HEAD_DIM, HEAD_DIM), state_map), vec_spec, vec_spec]
        out_shape += [jax.ShapeDtypeStruct((batch, 2, N_HEADS, HEAD_DIM, HEAD_DIM), F32),
                      jax.ShapeDtypeStruct((batch, nhb, 2, SUBLANES, LANES), F32),
                      jax.ShapeDtypeStruct((batch, nhb, 2, SUBLANES, LANES), F32)]
    if cast_weight is not None:
        slab = cast_weight.shape[0] // (batch * nhb)
        assert slab * batch * nhb == cast_weight.shape[0] and slab % (2 * SUBLANES) == 0
        slab_spec = pl.BlockSpec((slab, D_MODEL), lambda b, k: (b * nhb + k, 0))
        in_specs.append(slab_spec)
        args.append(cast_weight)
        out_specs.append(slab_spec)
        out_shape.append(jax.ShapeDtypeStruct(cast_weight.shape, BF16))
    kern = functools.partial(_mlstm_kernel, nc=nc, hb=hb, has_init=init is not None,
                             emit_state=emit_state, cast_weight=cast_weight is not None)
    return pl.pallas_call(
        kern,
        grid=(batch, nhb),
        in_specs=in_specs,
        out_specs=out_specs,
        out_shape=out_shape,
        scratch_shapes=[pltpu.VMEM((seq_len, w), F32),
                        pltpu.VMEM((seq_len, w), F32),
                        pltpu.VMEM((2 * hb, HEAD_DIM, 2 * HEAD_DIM), F32),
                        pltpu.VMEM((2, SUBLANES, LANES), F32),
                        pltpu.VMEM((nc, LANES, LANES), F32),
                        pltpu.VMEM((3, 2, nc, SUBLANES, LANES), F32)],
        compiler_params=_cparams(("arbitrary", "arbitrary")),
        name="mlstm",
    )(*args)


def _shift_rows(x, dlt):
    return pltpu.roll(x, (-dlt) % x.shape[0], axis=0)


def _pool_minus_self(x, half, pos, seg, inv_cnt):
    fwd = x
    k = 1
    while k < half:
        fwd = fwd + jnp.where(pos + k < seg, _shift_rows(fwd, k), 0.0)
        k *= 2
    bwd = jnp.where(pos >= 1, _shift_rows(x, -1), 0.0)
    k = 1
    while k < half:
        bwd = bwd + jnp.where(pos >= k, _shift_rows(bwd, -k), 0.0)
        k *= 2
    return (fwd + bwd) * inv_cnt - x


OUTPROJ_ROW_BLOCKS = 4
OUTPROJ_EPI_ROWS = 64
OUTPROJ_SLICE_COLS = 256


def _outproj_kernel(*refs, cast_mlp_weights):
    hm_ref, hp_ref, x_ref, mod_ref, nw_ref, wout_ref = refs[:6]
    if cast_mlp_weights:
        w1_ref, x1_ref, h2_ref, w1b_ref, mix_ref = refs[6:]
        w1b_ref[...] = w1_ref[...].astype(BF16)
    else:
        x1_ref, h2_ref, mix_ref = refs[6:]
    tm = x_ref.shape[0]
    nblk = OUTPROJ_ROW_BLOCKS
    rb = tm // nblk
    nslice = D_MODEL // OUTPROJ_SLICE_COLS
    mod = mod_ref[0]
    gain1 = nw_ref[1:2, :] * mod[2:3, :]
    gain2 = nw_ref[2:3, :] * (1.0 + mod[4:5, :])
    shift2 = mod[3:4, :]

    def epilogue_slices(r):
        def one(lo):
            rows = slice(lo, lo + OUTPROJ_EPI_ROWS)
            mix = mix_ref[rows, :]
            ms = jnp.mean(mix * mix, axis=-1, keepdims=True)
            x1 = x_ref[rows, :] + (mix * lax.rsqrt(ms + EPS)) * gain1
            x1_ref[rows, :] = x1
            ms2 = jnp.mean(x1 * x1, axis=-1, keepdims=True)
            h2_ref[rows, :] = ((x1 * lax.rsqrt(ms2 + EPS)) * gain2 + shift2).astype(BF16)
        return [functools.partial(one, r * rb + k * OUTPROJ_EPI_ROWS)
                for k in range(rb // OUTPROJ_EPI_ROWS)]

    def run_block(r, pending):
        rows = slice(r * rb, (r + 1) * rb)
        lhs = jnp.concatenate([hm_ref[rows, :], hp_ref[rows, :]], axis=1)
        for idx in range(nslice):
            ns = slice(idx * OUTPROJ_SLICE_COLS, (idx + 1) * OUTPROJ_SLICE_COLS)
            mix_ref[rows, ns] = jnp.dot(lhs, wout_ref[:, ns], preferred_element_type=F32)
            if pending and idx % 4 == 3:
                pending.pop(0)()
        while pending:
            pending.pop(0)()

    pending = []
    for r in range(nblk):
        run_block(r, pending)
        pending = epilogue_slices(r)
    for fn in pending:
        fn()


def _outproj_call(hm, hp, x, mod, mod_map, norm_w, w_out, cast_w1=None, *, tm):
    tokens = x.shape[0]
    steps = tokens // tm
    tok = lambda i: (i, 0)
    in_specs = [pl.BlockSpec((tm, MLSTM_WIDTH), tok),
                pl.BlockSpec((tm, POOL_WIDTH), tok),
                pl.BlockSpec((tm, D_MODEL), tok),
                pl.BlockSpec((1, N_MOD, D_MODEL), mod_map),
                _resident((4, D_MODEL)),
                _resident(w_out.shape)]
    out_specs = [pl.BlockSpec((tm, D_MODEL), tok),
                 pl.BlockSpec((tm, D_MODEL), tok)]
    out_shape = [jax.ShapeDtypeStruct((tokens, D_MODEL), F32),
                 jax.ShapeDtypeStruct((tokens, D_MODEL), BF16)]
    args = [hm, hp, x, mod, norm_w, w_out]
    if cast_w1 is not None:
        slab = D_FF // steps
        assert slab * steps == D_FF and slab % LANES == 0
        slab_spec = pl.BlockSpec((D_MODEL, slab), lambda i: (0, i))
        in_specs.append(slab_spec)
        out_specs.append(slab_spec)
        out_shape.append(jax.ShapeDtypeStruct((D_MODEL, D_FF), BF16))
        args.append(cast_w1)
    return pl.pallas_call(
        functools.partial(_outproj_kernel, cast_mlp_weights=cast_w1 is not None),
        grid=(steps,),
        in_specs=in_specs,
        out_specs=out_specs,
        out_shape=out_shape,
        scratch_shapes=[pltpu.VMEM((tm, D_MODEL), F32)],
        compiler_params=_cparams(("arbitrary",)),
        name="out_proj",
    )(*args)


MLP_LAST_ROW_BLOCKS = (256, 256)
MLP_SLICE_COLS = 256


def _mlp_kernel(h_ref, w1_ref, w2_ref, x1_ref, mod_ref, nw_ref, out_ref, acc_ref):
    k = pl.program_id(1)
    last = pl.num_programs(1) - 1
    tm, tf = h_ref.shape[0], w1_ref.shape[1]

    def hidden(rows, cols=slice(None)):
        a = jnp.dot(h_ref[rows, :], w1_ref[:, cols], preferred_element_type=F32)
        return jnp.square(jnp.maximum(a, 0.0)).astype(BF16)

    @pl.when(k == 0)
    def _():
        acc_ref[...] = jnp.dot(hidden(slice(None)), w2_ref[...], preferred_element_type=F32)

    @pl.when((k > 0) & (k < last))
    def _():
        acc_ref[...] += jnp.dot(hidden(slice(None)), w2_ref[...], preferred_element_type=F32)

    @pl.when(k == last)
    def _():
        assert sum(MLP_LAST_ROW_BLOCKS) == tm
        gain = nw_ref[3:4, :] * mod_ref[0, 5:6, :]

        def epilogue_slices(lo, nrows):
            def one(lo):
                rows = slice(lo, lo + NORM_SLICE_ROWS)
                y = acc_ref[rows, :]
                ms = jnp.mean(y * y, axis=-1, keepdims=True)
                out_ref[rows, :] = x1_ref[rows, :] + (y * lax.rsqrt(ms + EPS)) * gain
            return [functools.partial(one, lo + j * NORM_SLICE_ROWS)
                    for j in range(nrows // NORM_SLICE_ROWS)]

        pending = []
        n1, n2 = tf // MLP_SLICE_COLS, D_MODEL // MLP_SLICE_COLS
        lo = 0
        for nrows in MLP_LAST_ROW_BLOCKS:
            rows = slice(lo, lo + nrows)
            parts = []
            for idx in range(n1):
                parts.append(hidden(rows, slice(idx * MLP_SLICE_COLS, (idx + 1) * MLP_SLICE_COLS)))
                if pending and idx % 2 == 1:
                    pending.pop(0)()
            a = jnp.concatenate(parts, axis=1)
            for idx in range(n2):
                ns = slice(idx * MLP_SLICE_COLS, (idx + 1) * MLP_SLICE_COLS)
                acc_ref[rows, ns] += jnp.dot(a, w2_ref[:, ns], preferred_element_type=F32)
                if pending and idx % 2 == 1:
                    pending.pop(0)()
            while pending:
                pending.pop(0)()
            pending = epilogue_slices(lo, nrows)
            lo += nrows
        for fn in pending:
            fn()


def _mlp_call(h2, x1, mod, mod_map, norm_w, w1, w2, *, tm, tf):
    tokens = x1.shape[0]
    tok = lambda i, k: (i, 0)
    return pl.pallas_call(
        _mlp_kernel,
        grid=(tokens // tm, D_FF // tf),
        in_specs=[pl.BlockSpec((tm, D_MODEL), tok),
                  pl.BlockSpec((D_MODEL, tf), lambda i, k: (0, k)),
                  pl.BlockSpec((tf, D_MODEL), lambda i, k: (k, 0)),
                  pl.BlockSpec((tm, D_MODEL), tok),
                  pl.BlockSpec((1, N_MOD, D_MODEL), lambda i, k: mod_map(i)),
                  pl.BlockSpec((4, D_MODEL), lambda i, k: (0, 0))],
        out_specs=pl.BlockSpec((tm, D_MODEL), tok),
        out_shape=jax.ShapeDtypeStruct((tokens, D_MODEL), F32),
        scratch_shapes=[pltpu.VMEM((tm, D_MODEL), F32)],
        compiler_params=_cparams(("arbitrary", "arbitrary")),
        name="mlp",
    )(h2, w1, w2, x1, mod, norm_w)


def _gate_weights(w_g, gate_bias):
    hb = HEADS_PER_STEP
    nhb = N_HEADS // hb
    wg = w_g.reshape(D_MODEL, N_GATES, nhb, hb)
    wg = jnp.pad(wg, ((0, 0), (0, 0), (0, 0), (0, SUBLANES - hb)))
    wgt = wg.transpose(2, 1, 3, 0).reshape(nhb * GATE_ROWS, D_MODEL)
    gb = jnp.pad(gate_bias.reshape(N_GATES, nhb, hb), ((0, 0), (0, 0), (0, SUBLANES - hb)))
    gb = gb.transpose(1, 0, 2).reshape(nhb * GATE_ROWS)
    return wgt.astype(BF16), gb.astype(F32)


def _pack_head_rows(x):
    hb = HEADS_PER_STEP
    nhb = N_HEADS // hb
    x = x.reshape(x.shape[:2] + (nhb, hb) + x.shape[3:])
    x = jnp.moveaxis(x, 2, 1)
    pad = [(0, 0)] * x.ndim
    pad[3] = (0, SUBLANES - hb)
    return jnp.pad(x, pad)


def _unpack_head_rows(x):
    x = jnp.moveaxis(x[:, :, :, :HEADS_PER_STEP], 1, 2)
    return x.reshape(x.shape[:2] + (N_HEADS,) + x.shape[4:])


def _mod_map(mod_row, seq_len, tm):
    if mod_row is None:
        per = seq_len // tm
        return lambda i: (1 + i // per, 0, 0)
    return lambda i: (mod_row, 0, 0)


def _mixer_stage(x, mod, mod_row, init, params, w_out, cast_rows, cast_w1=None, *, seg, tm, emit_state):
    batch, seq_len, _ = x.shape
    wts, mlstm_nw, pool_w, pool_scale, norm_w = params
    tokens = batch * seq_len
    xf = x.reshape(tokens, D_MODEL)
    q, kt, v, og, hp, stats = _inproj_call(xf, mod, mod_row, norm_w, wts, pool_w, pool_scale,
                                           batch=batch, seq_len=seq_len, tm=tm, seg=seg)
    shp = (batch, seq_len, MLSTM_WIDTH)
    res = _mlstm_call(q.reshape(shp), kt, v.reshape(shp), og.reshape(shp), stats, mlstm_nw, init,
                      cast_rows, batch=batch, seq_len=seq_len, emit_state=emit_state)
    hm = res[0].reshape(tokens, MLSTM_WIDTH)
    outs = _outproj_call(hm, hp, xf, mod, _mod_map(mod_row, seq_len, tm), norm_w,
                         res[-1] if w_out is None else w_out, cast_w1, tm=tm)
    return outs, res[1:]


def kernel(x_prompt, x_sample, c, state_C, state_n, state_m, c_ctx, w_in, gate_bias, mlstm_norm_w,
           pool_w, pool_scale, w_out, ada_w, ada_b, norm_w, w1, w2):
    m4 = 4 * MLSTM_WIDTH
    yp, ys = x_prompt, x_sample
    new_c, new_n, new_m = [], [], []
    for layer in range(w_in.shape[0]):
        wt = w_in[layer].T.astype(BF16)
        wgt, gb = _gate_weights(w_in[layer][:, m4:m4 + N_GATES * N_HEADS], gate_bias[layer])
        wkg = jnp.concatenate([wt[MLSTM_WIDTH:2 * MLSTM_WIDTH], wgt], axis=0)
        wts = (wt, wkg, wt[m4 + N_GATES * N_HEADS:], gb)
        params = (wts, mlstm_norm_w[layer].reshape(1, MLSTM_WIDTH), pool_w[layer].astype(BF16),
                  pool_scale[layer], norm_w[layer])

        n_lat = c.shape[0]
        rows = 16
        cond = jnp.concatenate([c_ctx[None, :], c, jnp.zeros((rows - 1 - n_lat, D_MODEL), F32)], axis=0)
        mod = _mod_call(cond, ada_w[layer], ada_b[layer]).reshape(rows, N_MOD, D_MODEL)

        tm = TOKEN_TILE
        (x1p, h2p), (cc, nn, mm, w_out_b) = _mixer_stage(yp, mod, 0, None, params, None, w_out[layer],
                                                         seg=yp.shape[1], tm=tm, emit_state=True)
        new_c.append(cc)
        new_n.append(_unpack_head_rows(nn))
        new_m.append(_unpack_head_rows(mm)[..., 0])

        sc = state_C[:, layer].astype(F32)
        sn = _pack_head_rows(state_n[:, layer].astype(F32))
        sm = _pack_head_rows(jnp.broadcast_to(state_m[:, layer].astype(F32)[..., None],
                                              (n_lat, 2, N_HEADS, LANES)))
        (x1s, h2s, w1b), (w2b,) = _mixer_stage(ys, mod, None, (sc, sn, sm), params, w_out_b, w2[layer],
                                               w1[layer], seg=GRID_W, tm=tm, emit_state=False)
        yp = _mlp_call(h2p, x1p, mod, _mod_map(0, yp.shape[1], tm), norm_w[layer], w1b, w2b,
                       tm=tm, tf=MLP_TF).reshape(yp.shape)
        ys = _mlp_call(h2s, x1s, mod, _mod_map(None, ys.shape[1], tm), norm_w[layer], w1b, w2b,
                       tm=tm, tf=MLP_TF).reshape(ys.shape)
    return (yp, ys, jnp.stack(new_c, axis=1), jnp.stack(new_n, axis=1), jnp.stack(new_m, axis=1))
```

```python
import functools

import jax
import jax.numpy as jnp
from jax import lax
from jax.experimental import pallas as pl
from jax.experimental.pallas import tpu as pltpu

F32 = jnp.float32
BF16 = jnp.bfloat16

D_MODEL = 2048
MLSTM_WIDTH = 1024
N_HEADS = 8
HEAD_DIM = 128
POOL_WIDTH = 1024
POOL_WINDOWS = (2, 4, 8, 16)
POOL_GROUP_DIM = 256
N_GATES = 4
D_FF = 8192
CHUNK = 128
N_MOD = 6
GRID_W = 64
EPS = 1e-6

LANES = 128
SUBLANES = 8
VMEM_LIMIT = 56 * 1024 * 1024

TOKEN_TILE = 512
MLP_TF = 1024
MOD_TN = 1536
HEADS_PER_STEP = 4
GATE_ROWS = N_GATES * SUBLANES
GATE_STATS = 5
STAT_ROWS = 2 * GATE_STATS * SUBLANES
FINALIZE_UNROLL = 4


def _cparams(sem):
    return pltpu.CompilerParams(dimension_semantics=sem, vmem_limit_bytes=VMEM_LIMIT)


def _resident(shape):
    nd = len(shape)
    return pl.BlockSpec(shape, lambda *_: (0,) * nd, pipeline_mode=pl.Buffered(1))


def _rms(x, w):
    ms = jnp.mean(x * x, axis=-1, keepdims=True)
    return x * lax.rsqrt(ms + EPS) * w


def _mod_kernel(cond_ref, w_ref, b_ref, wf_ref, out_ref, wb_ref):
    c = cond_ref[...]
    s = c * jax.nn.sigmoid(c)
    out_ref[...] = jnp.dot(s.astype(BF16), w_ref[...].astype(BF16),
                           preferred_element_type=F32) + b_ref[...]
    wb_ref[...] = wf_ref[...].astype(BF16)


def _mod_call(cond, ada_w, ada_b, w_rows):
    rows = cond.shape[0]
    n = ada_w.shape[1]
    tn = MOD_TN
    steps = n // tn
    slabs = steps - 1
    slab = w_rows.shape[0] // slabs
    assert slab * slabs == w_rows.shape[0] and slab % (2 * SUBLANES) == 0
    slab_spec = pl.BlockSpec((slab, D_MODEL), lambda j: (jnp.minimum(j, slabs - 1), 0))
    return pl.pallas_call(
        _mod_kernel,
        grid=(steps,),
        in_specs=[pl.BlockSpec((rows, D_MODEL), lambda j: (0, 0)),
                  pl.BlockSpec((D_MODEL, tn), lambda j: (0, j)),
                  pl.BlockSpec((1, tn), lambda j: (0, j)),
                  slab_spec],
        out_specs=[pl.BlockSpec((rows, tn), lambda j: (0, j)), slab_spec],
        out_shape=[jax.ShapeDtypeStruct((rows, n), F32),
                   jax.ShapeDtypeStruct(w_rows.shape, BF16)],
        compiler_params=_cparams(("arbitrary",)),
        name="mod",
    )(cond, ada_w, ada_b.reshape(1, n), w_rows)


NORM_SLICE_ROWS = 64


def _lane_scan_pair(fwd, bwd, op, ident):
    lane = lax.broadcasted_iota(jnp.int32, fwd.shape, 1)
    k = 1
    while k < LANES:
        f_shift = pltpu.roll(fwd, k, axis=1)
        b_shift = pltpu.roll(bwd, LANES - k, axis=1)
        fwd = op(fwd, jnp.where(lane >= k, f_shift, ident))
        bwd = op(bwd, jnp.where(lane < LANES - k, b_shift, ident))
        k *= 2
    return fwd, bwd


def _inproj_kernel(x_ref, mod_ref, nw_ref, wq_ref, wkg_ref, wv_ref, wo_ref, wu_ref, gb_ref, pw_ref, ps_ref,
                   q_ref, kt_ref, v_ref, og_ref, hp_ref, g_ref, *, seqs, chunks, seg):
    x = x_ref[...]
    mod = mod_ref[0]
    tm = x.shape[0]
    h = (_rms(x, nw_ref[0:1, :]) * (1.0 + mod[1:2, :]) + mod[0:1, :]).astype(BF16)
    ktg = lax.dot_general(wkg_ref[...], h, (((1,), (1,)), ((), ())), preferred_element_type=F32)
    kt = ktg[0:MLSTM_WIDTH].astype(BF16)
    gt = ktg[MLSTM_WIDTH:] + gb_ref[...]
    nhb = gt.shape[0] // GATE_ROWS
    for s in range(seqs):
        for c in range(chunks):
            lo = (s * chunks + c) * CHUNK
            kt_ref[s, c] = kt[:, lo:lo + CHUNK]
    def proj(wt_ref):
        return lax.dot_general(h, wt_ref[...], (((1,), (1,)), ((), ())), preferred_element_type=F32)

    u = proj(wu_ref)
    pos = lax.broadcasted_iota(jnp.int32, (tm, 1), 0) % seg

    def pool_group(gi):
        half = POOL_WINDOWS[gi] // 2
        inv_cnt = 1.0 / (jnp.minimum(pos + half, seg) - jnp.maximum(pos - half, 0)).astype(F32)
        cs = slice(gi * POOL_GROUP_DIM, (gi + 1) * POOL_GROUP_DIM)
        p = jnp.concatenate(
            [_pool_minus_self(u[:, lo:lo + LANES], half, pos, seg, inv_cnt).astype(BF16)
             for lo in range(cs.start, cs.stop, LANES)], axis=1)
        y = jnp.dot(p, pw_ref[gi], preferred_element_type=F32)
        hp_ref[:, cs] = (y * ps_ref[0:1, cs]).astype(BF16)

    q = proj(wq_ref) * (HEAD_DIM ** -0.5)
    q_ref[...] = q.astype(BF16)
    pool_group(0)
    pool_group(1)
    v_ref[...] = proj(wv_ref).astype(BF16)
    pool_group(2)
    og_ref[...] = jax.nn.sigmoid(proj(wo_ref))
    pool_group(3)

    groups = [(s, c, hbk) for s in range(seqs) for c in range(chunks) for hbk in range(nhb)]

    def gate(g):
        return jnp.concatenate(
            [gt[hbk * GATE_ROWS + g * SUBLANES:hbk * GATE_ROWS + (g + 1) * SUBLANES,
                (s * chunks + c) * CHUNK:(s * chunks + c + 1) * CHUNK] for s, c, hbk in groups], axis=0)

    li = (gate(0), gate(2))
    lf = (jax.nn.log_sigmoid(gate(1)), jax.nn.log_sigmoid(gate(3)))
    b = _lane_scan_pair(lf[0], lf[1], jnp.add, 0.0)
    a = (li[0] - b[0], li[1] - b[1])
    pm = _lane_scan_pair(a[0], a[1], jnp.maximum, jnp.float32(-jnp.inf))
    stats = []
    for d in range(2):
        amax = jnp.broadcast_to(jnp.max(a[d], axis=1, keepdims=True), a[d].shape)
        bsum = jnp.broadcast_to(jnp.sum(lf[d], axis=1, keepdims=True), a[d].shape)
        stats += [a[d], b[d], pm[d], amax, bsum]
    for gi, (s, c, hbk) in enumerate(groups):
        rs = slice(gi * SUBLANES, (gi + 1) * SUBLANES)
        g_ref[s, c, hbk * STAT_ROWS:(hbk + 1) * STAT_ROWS, :] = jnp.concatenate(
            [st[rs] for st in stats], axis=0)


def _inproj_call(x, mod, mod_row, norm_w, wts, pool_w, pool_scale, *, batch, seq_len, tm, seg):
    wt, wkg, wu, gb = wts
    row_block = lambda j: pl.BlockSpec((MLSTM_WIDTH, D_MODEL), lambda *_: (j, 0),
                                       pipeline_mode=pl.Buffered(1))
    tokens = batch * seq_len
    nc = seq_len // CHUNK
    grows = wkg.shape[0] - MLSTM_WIDTH
    srows = grows // GATE_ROWS * STAT_ROWS
    if tm >= seq_len:
        seqs, chunks = tm // seq_len, nc
        kt_map = lambda i: (i, 0, 0, 0)
    else:
        seqs, chunks = 1, tm // CHUNK
        per = seq_len // tm
        kt_map = lambda i: (i // per, i % per, 0, 0)
    tiles_per_seq = max(seq_len // tm, 1)
    if mod_row is None:
        mod_map = lambda i: (1 + i // tiles_per_seq, 0, 0)
    else:
        mod_map = lambda i: (mod_row, 0, 0)
    tok = lambda i: (i, 0)
    gbb = jnp.broadcast_to(gb[:, None], (grows, tm))
    kern = functools.partial(_inproj_kernel, seqs=seqs, chunks=chunks, seg=seg)
    return pl.pallas_call(
        kern,
        grid=(tokens // tm,),
        in_specs=[pl.BlockSpec((tm, D_MODEL), tok),
                  pl.BlockSpec((1, N_MOD, D_MODEL), mod_map),
                  _resident((4, D_MODEL)),
                  row_block(0), _resident(wkg.shape), row_block(2),
                  row_block(3), _resident(wu.shape),
                  _resident((grows, tm)),
                  _resident(pool_w.shape),
                  _resident((1, POOL_WIDTH))],
        out_specs=[pl.BlockSpec((tm, MLSTM_WIDTH), tok),
                   pl.BlockSpec((seqs, chunks, MLSTM_WIDTH, CHUNK), kt_map),
                   pl.BlockSpec((tm, MLSTM_WIDTH), tok),
                   pl.BlockSpec((tm, MLSTM_WIDTH), tok),
                   pl.BlockSpec((tm, POOL_WIDTH), tok),
                   pl.BlockSpec((seqs, chunks, srows, CHUNK), kt_map)],
        out_shape=[jax.ShapeDtypeStruct((tokens, MLSTM_WIDTH), BF16),
                   jax.ShapeDtypeStruct((batch, nc, MLSTM_WIDTH, CHUNK), BF16),
                   jax.ShapeDtypeStruct((tokens, MLSTM_WIDTH), BF16),
                   jax.ShapeDtypeStruct((tokens, MLSTM_WIDTH), F32),
                   jax.ShapeDtypeStruct((tokens, POOL_WIDTH), BF16),
                   jax.ShapeDtypeStruct((batch, nc, srows, CHUNK), F32)],
        compiler_params=_cparams(("arbitrary",)),
        name="in_proj",
    )(x, mod, norm_w, wt, wkg, wt, wt, wu, gbb, pool_w, pool_scale.reshape(1, POOL_WIDTH))


def _chunk_start(c):
    return c * CHUNK if isinstance(c, int) else pl.multiple_of(c * CHUNK, CHUNK)


def _mlstm_kernel(*refs, nc, hb, has_init, emit_state, cast_weight):
    it = iter(refs)
    q_ref, kt_ref, v_ref, o_ref, g_ref, nw_ref = (next(it) for _ in range(6))
    if has_init:
        c0_ref, n0_ref, m0_ref = (next(it) for _ in range(3))
    if cast_weight:
        wf_ref = next(it)
    out_ref = next(it)
    if emit_state:
        cout_ref, nout_ref, mout_ref = (next(it) for _ in range(3))
    if cast_weight:
        next(it)[...] = wf_ref[...].astype(BF16)
    hf_sc, hb_sc, cn_sc, m_sc, xt_sc, rows_sc = (next(it) for _ in range(6))

    if has_init:
        for d in range(2):
            for j in range(hb):
                cn_sc[d * hb + j, :, 0:HEAD_DIM] = c0_ref[0, d, j]
                n_row = n0_ref[0, 0, d][j:j + 1, :]
                cn_sc[d * hb + j, :, HEAD_DIM:] = jnp.broadcast_to(n_row, (HEAD_DIM, LANES)).T
    else:
        cn_sc[...] = jnp.zeros_like(cn_sc)

    neg_inf = jnp.float32(-jnp.inf)

    cols = []
    for d in range(2):
        def stat(k):
            lo = (d * GATE_STATS + k) * SUBLANES
            return g_ref[0, :, lo:lo + SUBLANES, :].reshape(nc * SUBLANES, LANES)
        a, b, pm, amax, bsum = (stat(k) for k in range(GATE_STATS))
        m = m0_ref[0, 0, d] if has_init else jnp.zeros((SUBLANES, LANES), F32)
        m_prev = [None] * nc
        for ci in (range(nc) if d == 0 else range(nc - 1, -1, -1)):
            rs = slice(ci * SUBLANES, (ci + 1) * SUBLANES)
            m_prev[ci] = m
            m = bsum[rs] + jnp.maximum(m, amax[rs])
        m_sc[d] = m
        mp = jnp.concatenate(m_prev, axis=0)
        mu = jnp.maximum(mp, pm)
        mu_last = jnp.maximum(mp, amax)
        rows_sc[0, d] = a.reshape(nc, SUBLANES, LANES)
        rows_sc[1, d] = jnp.exp(a - mu_last).reshape(nc, SUBLANES, LANES)
        rows_sc[2, d] = jnp.exp(mp - mu_last).reshape(nc, SUBLANES, LANES)
        cols.append((mu, jnp.exp(mp - mu), jnp.exp(-(b + mu))))
    zpad = jnp.zeros((LANES - 6 * SUBLANES, LANES), F32)
    for c in range(nc):
        pieces = []
        for d, ci in ((0, c), (1, nc - 1 - c)):
            pieces += [x[ci * SUBLANES:(ci + 1) * SUBLANES] for x in cols[d]]
        xt_sc[c] = jnp.concatenate(pieces + [zpad], axis=0).T

    row = lax.broadcasted_iota(jnp.int32, (CHUNK, CHUNK), 0)
    col = lax.broadcasted_iota(jnp.int32, (CHUNK, CHUNK), 1)
    causal = (col <= row, col >= row)
    ones_blk = jnp.ones((CHUNK, HEAD_DIM), BF16)

    def step(c, carry):
        chunk_of = (c, nc - 1 - c)
        xt = xt_sc[c]
        chains = []
        for d in range(2):
            ch = chunk_of[d]
            t0 = _chunk_start(ch)
            a_rows = rows_sc[0, d, ch]
            for j in range(hb):
                hs = slice(j * HEAD_DIM, (j + 1) * HEAD_DIM)
                qc = q_ref[0, pl.ds(t0, CHUNK), hs]
                base = 3 * d * SUBLANES + j
                mu_c = xt[:, base:base + 1]
                g_c = xt[:, base + SUBLANES:base + SUBLANES + 1]
                s = jnp.dot(qc, kt_ref[0, ch, hs, :], preferred_element_type=F32)
                p = s * jnp.exp(jnp.where(causal[d], a_rows[j:j + 1, :] - mu_c, neg_inf))
                gq = (qc.astype(F32) * g_c).astype(BF16)
                chains.append((d, j, ch, t0, jnp.concatenate([p.astype(BF16), gq], axis=1)))

        for d, j, ch, t0, lhs in chains:
            hs = slice(j * HEAD_DIM, (j + 1) * HEAD_DIM)
            base = 3 * d * SUBLANES + j
            en_c = xt[:, base + 2 * SUBLANES:base + 2 * SUBLANES + 1]
            v1 = jnp.concatenate([v_ref[0, pl.ds(t0, CHUNK), hs], ones_blk], axis=1)
            cn = cn_sc[d * hb + j]
            tot = jnp.dot(lhs, jnp.concatenate([v1, cn.astype(BF16)], axis=0),
                          preferred_element_type=F32)
            hval = tot[:, :HEAD_DIM] / jnp.maximum(jnp.abs(tot[:, HEAD_DIM:]), en_c)
            dst = hf_sc if d == 0 else hb_sc
            dst[pl.ds(t0, CHUNK), hs] = hval

            e_row = rows_sc[1, d, ch][j:j + 1, :]
            gs_row = rows_sc[2, d, ch][j:j + 1, :]
            wkt = (kt_ref[0, ch, hs, :].astype(F32) * e_row).astype(BF16)
            dcn = jnp.dot(wkt, v1, preferred_element_type=F32)
            gs_blk = jnp.broadcast_to(gs_row, (HEAD_DIM, LANES))
            cn_sc[d * hb + j] = jnp.concatenate([gs_blk, gs_blk], axis=1) * cn + dcn
        return carry

    if nc <= 2:
        for c in range(nc):
            step(c, 0)
    else:
        lax.fori_loop(0, nc, step, 0, unroll=2)

    def finalize(c, carry):
        t0 = _chunk_start(c)
        hsum = hf_sc[pl.ds(t0, CHUNK), :] + hb_sc[pl.ds(t0, CHUNK), :]
        og = o_ref[0, pl.ds(t0, CHUNK), :]
        for j in range(hb):
            hs = slice(j * HEAD_DIM, (j + 1) * HEAD_DIM)
            hn = _rms(hsum[:, hs], nw_ref[0:1, hs])
            out_ref[0, pl.ds(t0, CHUNK), hs] = (hn * og[:, hs]).astype(BF16)
        return carry

    if nc <= 2:
        for c in range(nc):
            finalize(c, 0)
    else:
        lax.fori_loop(0, nc, finalize, 0, unroll=FINALIZE_UNROLL)

    if emit_state:
        pad_rows = jnp.zeros((SUBLANES - hb, LANES), F32)
        for d in range(2):
            n_rows = []
            for j in range(hb):
                cout_ref[0, d, j] = cn_sc[d * hb + j, :, 0:HEAD_DIM]
                n_rows.append(cn_sc[d * hb + j, :, HEAD_DIM:].T[0:1, :])
            nout_ref[0, 0, d] = jnp.concatenate(n_rows + ([pad_rows] if hb < SUBLANES else []), axis=0)
            mout_ref[0, 0, d] = m_sc[d]


def _mlstm_call(q, kt, v, o, stats, nw, init, cast_weight=None, *, batch, seq_len, emit_state):
    hb = HEADS_PER_STEP
    nhb = N_HEADS // hb
    nc = seq_len // CHUNK
    w = hb * HEAD_DIM
    seq_map = lambda b, k: (b, 0, k)
    in_specs = [pl.BlockSpec((1, seq_len, w), seq_map),
                pl.BlockSpec((1, nc, w, CHUNK), lambda b, k: (b, 0, k, 0)),
                pl.BlockSpec((1, seq_len, w), seq_map),
                pl.BlockSpec((1, seq_len, w), seq_map),
                pl.BlockSpec((1, nc, STAT_ROWS, CHUNK), lambda b, k: (b, 0, k, 0)),
                pl.BlockSpec((1, w), lambda b, k: (0, k))]
    args = [q, kt, v, o, stats, nw]
    state_map = lambda b, k: (b, 0, k, 0, 0)
    vec_map = lambda b, k: (b, k, 0, 0, 0)
    vec_spec = pl.BlockSpec((1, 1, 2, SUBLANES, LANES), vec_map)
    if init is not None:
        c0, n0, m0 = init
        in_specs += [pl.BlockSpec((1, 2, hb, HEAD_DIM, HEAD_DIM), state_map), vec_spec, vec_spec]
        args += [c0, n0, m0]
    out_specs = [pl.BlockSpec((1, seq_len, w), seq_map)]
    out_shape = [jax.ShapeDtypeStruct((batch, seq_len, MLSTM_WIDTH), BF16)]
    if emit_state:
        out_specs += [pl.BlockSpec((1, 2, hb, HEAD_DIM, HEAD_DIM), state_map), vec_spec, vec_spec]
        out_shape += [jax.ShapeDtypeStruct((batch, 2, N_HEADS, HEAD_DIM, HEAD_DIM), F32),
                      jax.ShapeDtypeStruct((batch, nhb, 2, SUBLANES, LANES), F32),
                      jax.ShapeDtypeStruct((batch, nhb, 2, SUBLANES, LANES), F32)]
    if cast_weight is not None:
        slab = cast_weight.shape[0] // (batch * nhb)
        assert slab * batch * nhb == cast_weight.shape[0] and slab % (2 * SUBLANES) == 0
        slab_spec = pl.BlockSpec((slab, D_MODEL), lambda b, k: (b * nhb + k, 0))
        in_specs.append(slab_spec)
        args.append(cast_weight)
        out_specs.append(slab_spec)
        out_shape.append(jax.ShapeDtypeStruct(cast_weight.shape, BF16))
    kern = functools.partial(_mlstm_kernel, nc=nc, hb=hb, has_init=init is not None,
                             emit_state=emit_state, cast_weight=cast_weight is not None)
    return pl.pallas_call(
        kern,
        grid=(batch, nhb),
        in_specs=in_specs,
        out_specs=out_specs,
        out_shape=out_shape,
        scratch_shapes=[pltpu.VMEM((seq_len, w), F32),
                        pltpu.VMEM((seq_len, w), F32),
                        pltpu.VMEM((2 * hb, HEAD_DIM, 2 * HEAD_DIM), F32),
                        pltpu.VMEM((2, SUBLANES, LANES), F32),
                        pltpu.VMEM((nc, LANES, LANES), F32),
                        pltpu.VMEM((3, 2, nc, SUBLANES, LANES), F32)],
        compiler_params=_cparams(("arbitrary", "arbitrary")),
        name="mlstm",
    )(*args)


def _shift_rows(x, dlt):
    return pltpu.roll(x, (-dlt) % x.shape[0], axis=0)


def _pool_minus_self(x, half, pos, seg, inv_cnt):
    fwd = x
    k = 1
    while k < half:
        fwd = fwd + jnp.where(pos + k < seg, _shift_rows(fwd, k), 0.0)
        k *= 2
    bwd = jnp.where(pos >= 1, _shift_rows(x, -1), 0.0)
    k = 1
    while k < half:
        bwd = bwd + jnp.where(pos >= k, _shift_rows(bwd, -k), 0.0)
        k *= 2
    return (fwd + bwd) * inv_cnt - x


OUTPROJ_ROW_BLOCKS = 4
OUTPROJ_EPI_ROWS = 64
OUTPROJ_SLICE_COLS = 256


def _outproj_kernel(*refs, cast_mlp_weights):
    hm_ref, hp_ref, x_ref, mod_ref, nw_ref, wout_ref = refs[:6]
    if cast_mlp_weights:
        w1_ref, x1_ref, h2_ref, w1b_ref, mix_ref = refs[6:]
        w1b_ref[...] = w1_ref[...].astype(BF16)
    else:
        x1_ref, h2_ref, mix_ref = refs[6:]
    tm = x_ref.shape[0]
    nblk = OUTPROJ_ROW_BLOCKS
    rb = tm // nblk
    nslice = D_MODEL // OUTPROJ_SLICE_COLS
    mod = mod_ref[0]
    gain1 = nw_ref[1:2, :] * mod[2:3, :]
    gain2 = nw_ref[2:3, :] * (1.0 + mod[4:5, :])
    shift2 = mod[3:4, :]

    def epilogue_slices(r):
        def one(lo):
            rows = slice(lo, lo + OUTPROJ_EPI_ROWS)
            mix = mix_ref[rows, :]
            ms = jnp.mean(mix * mix, axis=-1, keepdims=True)
            x1 = x_ref[rows, :] + (mix * lax.rsqrt(ms + EPS)) * gain1
            x1_ref[rows, :] = x1
            ms2 = jnp.mean(x1 * x1, axis=-1, keepdims=True)
            h2_ref[rows, :] = ((x1 * lax.rsqrt(ms2 + EPS)) * gain2 + shift2).astype(BF16)
        return [functools.partial(one, r * rb + k * OUTPROJ_EPI_ROWS)
                for k in range(rb // OUTPROJ_EPI_ROWS)]

    def run_block(r, pending):
        rows = slice(r * rb, (r + 1) * rb)
        lhs = jnp.concatenate([hm_ref[rows, :], hp_ref[rows, :]], axis=1)
        for idx in range(nslice):
            ns = slice(idx * OUTPROJ_SLICE_COLS, (idx + 1) * OUTPROJ_SLICE_COLS)
            mix_ref[rows, ns] = jnp.dot(lhs, wout_ref[:, ns], preferred_element_type=F32)
            if pending and idx % 4 == 3:
                pending.pop(0)()
        while pending:
            pending.pop(0)()

    pending = []
    for r in range(nblk):
        run_block(r, pending)
        pending = epilogue_slices(r)
    for fn in pending:
        fn()


def _outproj_call(hm, hp, x, mod, mod_map, norm_w, w_out, cast_w1=None, *, tm):
    tokens = x.shape[0]
    steps = tokens // tm
    tok = lambda i: (i, 0)
    in_specs = [pl.BlockSpec((tm, MLSTM_WIDTH), tok),
                pl.BlockSpec((tm, POOL_WIDTH), tok),
                pl.BlockSpec((tm, D_MODEL), tok),
                pl.BlockSpec((1, N_MOD, D_MODEL), mod_map),
                _resident((4, D_MODEL)),
                _resident(w_out.shape)]
    out_specs = [pl.BlockSpec((tm, D_MODEL), tok),
                 pl.BlockSpec((tm, D_MODEL), tok)]
    out_shape = [jax.ShapeDtypeStruct((tokens, D_MODEL), F32),
                 jax.ShapeDtypeStruct((tokens, D_MODEL), BF16)]
    args = [hm, hp, x, mod, norm_w, w_out]
    if cast_w1 is not None:
        slab = D_FF // steps
        assert slab * steps == D_FF and slab % LANES == 0
        slab_spec = pl.BlockSpec((D_MODEL, slab), lambda i: (0, i))
        in_specs.append(slab_spec)
        out_specs.append(slab_spec)
        out_shape.append(jax.ShapeDtypeStruct((D_MODEL, D_FF), BF16))
        args.append(cast_w1)
    return pl.pallas_call(
        functools.partial(_outproj_kernel, cast_mlp_weights=cast_w1 is not None),
        grid=(steps,),
        in_specs=in_specs,
        out_specs=out_specs,
        out_shape=out_shape,
        scratch_shapes=[pltpu.VMEM((tm, D_MODEL), F32)],
        compiler_params=_cparams(("arbitrary",)),
        name="out_proj",
    )(*args)


MLP_LAST_ROW_BLOCKS = (256, 256)
MLP_SLICE_COLS = 256


def _mlp_kernel(h_ref, w1_ref, w2_ref, x1_ref, mod_ref, nw_ref, out_ref, acc_ref):
    k = pl.program_id(1)
    last = pl.num_programs(1) - 1
    tm, tf = h_ref.shape[0], w1_ref.shape[1]

    def hidden(rows, cols=slice(None)):
        a = jnp.dot(h_ref[rows, :], w1_ref[:, cols], preferred_element_type=F32)
        return jnp.square(jnp.maximum(a, 0.0)).astype(BF16)

    @pl.when(k == 0)
    def _():
        acc_ref[...] = jnp.dot(hidden(slice(None)), w2_ref[...], preferred_element_type=F32)

    @pl.when((k > 0) & (k < last))
    def _():
        acc_ref[...] += jnp.dot(hidden(slice(None)), w2_ref[...], preferred_element_type=F32)

    @pl.when(k == last)
    def _():
        assert sum(MLP_LAST_ROW_BLOCKS) == tm
        gain = nw_ref[3:4, :] * mod_ref[0, 5:6, :]

        def epilogue_slices(lo, nrows):
            def one(lo):
                rows = slice(lo, lo + NORM_SLICE_ROWS)
                y = acc_ref[rows, :]
                ms = jnp.mean(y * y, axis=-1, keepdims=True)
                out_ref[rows, :] = x1_ref[rows, :] + (y * lax.rsqrt(ms + EPS)) * gain
            return [functools.partial(one, lo + j * NORM_SLICE_ROWS)
                    for j in range(nrows // NORM_SLICE_ROWS)]

        pending = []
        n1, n2 = tf // MLP_SLICE_COLS, D_MODEL // MLP_SLICE_COLS
        lo = 0
        for nrows in MLP_LAST_ROW_BLOCKS:
            rows = slice(lo, lo + nrows)
            parts = []
            for idx in range(n1):
                parts.append(hidden(rows, slice(idx * MLP_SLICE_COLS, (idx + 1) * MLP_SLICE_COLS)))
                if pending and idx % 2 == 1:
                    pending.pop(0)()
            a = jnp.concatenate(parts, axis=1)
            for idx in range(n2):
                ns = slice(idx * MLP_SLICE_COLS, (idx + 1) * MLP_SLICE_COLS)
                acc_ref[rows, ns] += jnp.dot(a, w2_ref[:, ns], preferred_element_type=F32)
                if pending and idx % 2 == 1:
                    pending.pop(0)()
            while pending:
                pending.pop(0)()
            pending = epilogue_slices(lo, nrows)
            lo += nrows
        for fn in pending:
            fn()


def _mlp_call(h2, x1, mod, mod_map, norm_w, w1, w2, *, tm, tf):
    tokens = x1.shape[0]
    tok = lambda i, k: (i, 0)
    return pl.pallas_call(
        _mlp_kernel,
        grid=(tokens // tm, D_FF // tf),
        in_specs=[pl.BlockSpec((tm, D_MODEL), tok),
                  pl.BlockSpec((D_MODEL, tf), lambda i, k: (0, k)),
                  pl.BlockSpec((tf, D_MODEL), lambda i, k: (k, 0)),
                  pl.BlockSpec((tm, D_MODEL), tok),
                  pl.BlockSpec((1, N_MOD, D_MODEL), lambda i, k: mod_map(i)),
                  pl.BlockSpec((4, D_MODEL), lambda i, k: (0, 0))],
        out_specs=pl.BlockSpec((tm, D_MODEL), tok),
        out_shape=jax.ShapeDtypeStruct((tokens, D_MODEL), F32),
        scratch_shapes=[pltpu.VMEM((tm, D_MODEL), F32)],
        compiler_params=_cparams(("arbitrary", "arbitrary")),
        name="mlp",
    )(h2, w1, w2, x1, mod, norm_w)


def _gate_weights(w_g, gate_bias):
    hb = HEADS_PER_STEP
    nhb = N_HEADS // hb
    wg = w_g.reshape(D_MODEL, N_GATES, nhb, hb)
    wg = jnp.pad(wg, ((0, 0), (0, 0), (0, 0), (0, SUBLANES - hb)))
    wgt = wg.transpose(2, 1, 3, 0).reshape(nhb * GATE_ROWS, D_MODEL)
    gb = jnp.pad(gate_bias.reshape(N_GATES, nhb, hb), ((0, 0), (0, 0), (0, SUBLANES - hb)))
    gb = gb.transpose(1, 0, 2).reshape(nhb * GATE_ROWS)
    return wgt.astype(BF16), gb.astype(F32)


def _pack_head_rows(x):
    hb = HEADS_PER_STEP
    nhb = N_HEADS // hb
    x = x.reshape(x.shape[:2] + (nhb, hb) + x.shape[3:])
    x = jnp.moveaxis(x, 2, 1)
    pad = [(0, 0)] * x.ndim
    pad[3] = (0, SUBLANES - hb)
    return jnp.pad(x, pad)


def _unpack_head_rows(x):
    x = jnp.moveaxis(x[:, :, :, :HEADS_PER_STEP], 1, 2)
    return x.reshape(x.shape[:2] + (N_HEADS,) + x.shape[4:])


def _mod_map(mod_row, seq_len, tm):
    if mod_row is None:
        per = seq_len // tm
        return lambda i: (1 + i // per, 0, 0)
    return lambda i: (mod_row, 0, 0)


def _mixer_stage(x, mod, mod_row, init, params, w_out, cast_rows, cast_w1=None, *, seg, tm, emit_state):
    batch, seq_len, _ = x.shape
    wts, mlstm_nw, pool_w, pool_scale, norm_w = params
    tokens = batch * seq_len
    xf = x.reshape(tokens, D_MODEL)
    q, kt, v, og, hp, stats = _inproj_call(xf, mod, mod_row, norm_w, wts, pool_w, pool_scale,
                                           batch=batch, seq_len=seq_len, tm=tm, seg=seg)
    shp = (batch, seq_len, MLSTM_WIDTH)
    res = _mlstm_call(q.reshape(shp), kt, v.reshape(shp), og.reshape(shp), stats, mlstm_nw, init,
                      cast_rows, batch=batch, seq_len=seq_len, emit_state=emit_state)
    hm = res[0].reshape(tokens, MLSTM_WIDTH)
    outs = _outproj_call(hm, hp, xf, mod, _mod_map(mod_row, seq_len, tm), norm_w,
                         res[-1] if w_out is None else w_out, cast_w1, tm=tm)
    return outs, res[1:]


def kernel(x_prompt, x_sample, c, state_C, state_n, state_m, c_ctx, w_in, gate_bias, mlstm_norm_w,
           pool_w, pool_scale, w_out, ada_w, ada_b, norm_w, w1, w2):
    m4 = 4 * MLSTM_WIDTH
    yp, ys = x_prompt, x_sample
    new_c, new_n, new_m = [], [], []
    for layer in range(w_in.shape[0]):
        n_lat = c.shape[0]
        rows = 16
        cond = jnp.concatenate([c_ctx[None, :], c, jnp.zeros((rows - 1 - n_lat, D_MODEL), F32)], axis=0)
        mod, wt = _mod_call(cond, ada_w[layer], ada_b[layer], w_in[layer].T)
        mod = mod.reshape(rows, N_MOD, D_MODEL)
        wgt, gb = _gate_weights(w_in[layer][:, m4:m4 + N_GATES * N_HEADS], gate_bias[layer])
        wkg = jnp.concatenate([wt[MLSTM_WIDTH:2 * MLSTM_WIDTH], wgt], axis=0)
        wts = (wt, wkg, wt[m4 + N_GATES * N_HEADS:], gb)
        params = (wts, mlstm_norm_w[layer].reshape(1, MLSTM_WIDTH), pool_w[layer].astype(BF16),
                  pool_scale[layer], norm_w[layer])

        tm = TOKEN_TILE
        (x1p, h2p), (cc, nn, mm, w_out_b) = _mixer_stage(yp, mod, 0, None, params, None, w_out[layer],
                                                         seg=yp.shape[1], tm=tm, emit_state=True)
        new_c.append(cc)
        new_n.append(_unpack_head_rows(nn))
        new_m.append(_unpack_head_rows(mm)[..., 0])

        sc = state_C[:, layer].astype(F32)
        sn = _pack_head_rows(state_n[:, layer].astype(F32))
        sm = _pack_head_rows(jnp.broadcast_to(state_m[:, layer].astype(F32)[..., None],
                                              (n_lat, 2, N_HEADS, LANES)))
        (x1s, h2s, w1b), (w2b,) = _mixer_stage(ys, mod, None, (sc, sn, sm), params, w_out_b, w2[layer],
                                               w1[layer], seg=GRID_W, tm=tm, emit_state=False)
        yp = _mlp_call(h2p, x1p, mod, _mod_map(0, yp.shape[1], tm), norm_w[layer], w1b, w2b,
                       tm=tm, tf=MLP_TF).reshape(yp.shape)
        ys = _mlp_call(h2s, x1s, mod, _mod_map(None, ys.shape[1], tm), norm_w[layer], w1b, w2b,
                       tm=tm, tf=MLP_TF).reshape(ys.shape)
    return (yp, ys, jnp.stack(new_c, axis=1), jnp.stack(new_n, axis=1), jnp.stack(new_m, axis=1))
```

```python
import functools

import jax
import jax.numpy as jnp
from jax import lax
from jax.experimental import pallas as pl
from jax.experimental.pallas import tpu as pltpu

F32 = jnp.float32
BF16 = jnp.bfloat16

D_MODEL = 2048
MLSTM_WIDTH = 1024
N_HEADS = 8
HEAD_DIM = 128
POOL_WIDTH = 1024
POOL_WINDOWS = (2, 4, 8, 16)
POOL_GROUP_DIM = 256
N_GATES = 4
D_FF = 8192
CHUNK = 128
N_MOD = 6
GRID_W = 64
EPS = 1e-6

LANES = 128
SUBLANES = 8
VMEM_LIMIT = 56 * 1024 * 1024

TOKEN_TILE = 512
MLP_TF = 1024
MOD_TN = 1536
HEADS_PER_STEP = 4
GATE_ROWS = N_GATES * SUBLANES
GATE_STATS = 5
STAT_ROWS = 2 * GATE_STATS * SUBLANES
FINALIZE_UNROLL = 4


def _cparams(sem, vmem_limit=VMEM_LIMIT):
    return pltpu.CompilerParams(dimension_semantics=sem, vmem_limit_bytes=vmem_limit)


def _resident(shape):
    nd = len(shape)
    return pl.BlockSpec(shape, lambda *_: (0,) * nd, pipeline_mode=pl.Buffered(1))


def _rms(x, w):
    ms = jnp.mean(x * x, axis=-1, keepdims=True)
    return x * lax.rsqrt(ms + EPS) * w


def _mod_kernel(cond_ref, w_ref, b_ref, wf_ref, out_ref, wb_ref):
    c = cond_ref[...]
    s = c * jax.nn.sigmoid(c)
    out_ref[...] = jnp.dot(s.astype(BF16), w_ref[...].astype(BF16),
                           preferred_element_type=F32) + b_ref[...]
    wb_ref[...] = wf_ref[...].astype(BF16)


def _mod_call(cond, ada_w, ada_b, w_rows):
    rows = cond.shape[0]
    n = ada_w.shape[1]
    tn = MOD_TN
    steps = n // tn
    slabs = steps - 1
    slab = w_rows.shape[0] // slabs
    assert slab * slabs == w_rows.shape[0] and slab % (2 * SUBLANES) == 0
    slab_spec = pl.BlockSpec((slab, D_MODEL), lambda j: (jnp.minimum(j, slabs - 1), 0))
    return pl.pallas_call(
        _mod_kernel,
        grid=(steps,),
        in_specs=[pl.BlockSpec((rows, D_MODEL), lambda j: (0, 0)),
                  pl.BlockSpec((D_MODEL, tn), lambda j: (0, j)),
                  pl.BlockSpec((1, tn), lambda j: (0, j)),
                  slab_spec],
        out_specs=[pl.BlockSpec((rows, tn), lambda j: (0, j)), slab_spec],
        out_shape=[jax.ShapeDtypeStruct((rows, n), F32),
                   jax.ShapeDtypeStruct(w_rows.shape, BF16)],
        compiler_params=_cparams(("arbitrary",)),
        name="mod",
    )(cond, ada_w, ada_b.reshape(1, n), w_rows)


NORM_SLICE_ROWS = 64


def _lane_scan_pair(fwd, bwd, op, ident):
    lane = lax.broadcasted_iota(jnp.int32, fwd.shape, 1)
    k = 1
    while k < LANES:
        f_shift = pltpu.roll(fwd, k, axis=1)
        b_shift = pltpu.roll(bwd, LANES - k, axis=1)
        fwd = op(fwd, jnp.where(lane >= k, f_shift, ident))
        bwd = op(bwd, jnp.where(lane < LANES - k, b_shift, ident))
        k *= 2
    return fwd, bwd


def _inproj_kernel(x_ref, mod_ref, nw_ref, wq_ref, wkg_ref, wv_ref, wo_ref, wu_ref, gb_ref, pw_ref, ps_ref,
                   q_ref, kt_ref, v_ref, og_ref, hp_ref, g_ref, *, seqs, chunks, seg):
    x = x_ref[...]
    mod = mod_ref[0]
    tm = x.shape[0]
    h = (_rms(x, nw_ref[0:1, :]) * (1.0 + mod[1:2, :]) + mod[0:1, :]).astype(BF16)
    ktg = lax.dot_general(wkg_ref[...], h, (((1,), (1,)), ((), ())), preferred_element_type=F32)
    kt = ktg[0:MLSTM_WIDTH].astype(BF16)
    gt = ktg[MLSTM_WIDTH:] + gb_ref[...]
    nhb = gt.shape[0] // GATE_ROWS
    for s in range(seqs):
        for c in range(chunks):
            lo = (s * chunks + c) * CHUNK
            kt_ref[s, c] = kt[:, lo:lo + CHUNK]
    def proj(wt_ref):
        return lax.dot_general(h, wt_ref[...], (((1,), (1,)), ((), ())), preferred_element_type=F32)

    u = proj(wu_ref)
    pos = lax.broadcasted_iota(jnp.int32, (tm, 1), 0) % seg

    def pool_group(gi):
        half = POOL_WINDOWS[gi] // 2
        inv_cnt = 1.0 / (jnp.minimum(pos + half, seg) - jnp.maximum(pos - half, 0)).astype(F32)
        cs = slice(gi * POOL_GROUP_DIM, (gi + 1) * POOL_GROUP_DIM)
        p = jnp.concatenate(
            [_pool_minus_self(u[:, lo:lo + LANES], half, pos, seg, inv_cnt).astype(BF16)
             for lo in range(cs.start, cs.stop, LANES)], axis=1)
        y = jnp.dot(p, pw_ref[gi], preferred_element_type=F32)
        hp_ref[:, cs] = (y * ps_ref[0:1, cs]).astype(BF16)

    q = proj(wq_ref) * (HEAD_DIM ** -0.5)
    q_ref[...] = q.astype(BF16)
    pool_group(0)
    pool_group(1)
    v_ref[...] = proj(wv_ref).astype(BF16)
    pool_group(2)
    og_ref[...] = jax.nn.sigmoid(proj(wo_ref))
    pool_group(3)

    groups = [(s, c, hbk) for s in range(seqs) for c in range(chunks) for hbk in range(nhb)]

    def gate(g):
        return jnp.concatenate(
            [gt[hbk * GATE_ROWS + g * SUBLANES:hbk * GATE_ROWS + (g + 1) * SUBLANES,
                (s * chunks + c) * CHUNK:(s * chunks + c + 1) * CHUNK] for s, c, hbk in groups], axis=0)

    li = (gate(0), gate(2))
    lf = (jax.nn.log_sigmoid(gate(1)), jax.nn.log_sigmoid(gate(3)))
    b = _lane_scan_pair(lf[0], lf[1], jnp.add, 0.0)
    a = (li[0] - b[0], li[1] - b[1])
    pm = _lane_scan_pair(a[0], a[1], jnp.maximum, jnp.float32(-jnp.inf))
    stats = []
    for d in range(2):
        amax = jnp.broadcast_to(jnp.max(a[d], axis=1, keepdims=True), a[d].shape)
        bsum = jnp.broadcast_to(jnp.sum(lf[d], axis=1, keepdims=True), a[d].shape)
        stats += [a[d], b[d], pm[d], amax, bsum]
    for gi, (s, c, hbk) in enumerate(groups):
        rs = slice(gi * SUBLANES, (gi + 1) * SUBLANES)
        g_ref[s, c, hbk * STAT_ROWS:(hbk + 1) * STAT_ROWS, :] = jnp.concatenate(
            [st[rs] for st in stats], axis=0)


def _inproj_call(x, mod, mod_row, norm_w, wts, pool_w, pool_scale, *, batch, seq_len, tm, seg):
    wt, wkg, wu, gb = wts
    row_block = lambda j: pl.BlockSpec((MLSTM_WIDTH, D_MODEL), lambda *_: (j, 0),
                                       pipeline_mode=pl.Buffered(1))
    tokens = batch * seq_len
    nc = seq_len // CHUNK
    grows = wkg.shape[0] - MLSTM_WIDTH
    srows = grows // GATE_ROWS * STAT_ROWS
    if tm >= seq_len:
        seqs, chunks = tm // seq_len, nc
        kt_map = lambda i: (i, 0, 0, 0)
    else:
        seqs, chunks = 1, tm // CHUNK
        per = seq_len // tm
        kt_map = lambda i: (i // per, i % per, 0, 0)
    tiles_per_seq = max(seq_len // tm, 1)
    if mod_row is None:
        mod_map = lambda i: (1 + i // tiles_per_seq, 0, 0)
    else:
        mod_map = lambda i: (mod_row, 0, 0)
    tok = lambda i: (i, 0)
    gbb = jnp.broadcast_to(gb[:, None], (grows, tm))
    kern = functools.partial(_inproj_kernel, seqs=seqs, chunks=chunks, seg=seg)
    return pl.pallas_call(
        kern,
        grid=(tokens // tm,),
        in_specs=[pl.BlockSpec((tm, D_MODEL), tok),
                  pl.BlockSpec((1, N_MOD, D_MODEL), mod_map),
                  _resident((4, D_MODEL)),
                  row_block(0), _resident(wkg.shape), row_block(2),
                  row_block(3), _resident(wu.shape),
                  _resident((grows, tm)),
                  _resident(pool_w.shape),
                  _resident((1, POOL_WIDTH))],
        out_specs=[pl.BlockSpec((tm, MLSTM_WIDTH), tok),
                   pl.BlockSpec((seqs, chunks, MLSTM_WIDTH, CHUNK), kt_map),
                   pl.BlockSpec((tm, MLSTM_WIDTH), tok),
                   pl.BlockSpec((tm, MLSTM_WIDTH), tok),
                   pl.BlockSpec((tm, POOL_WIDTH), tok),
                   pl.BlockSpec((seqs, chunks, srows, CHUNK), kt_map)],
        out_shape=[jax.ShapeDtypeStruct((tokens, MLSTM_WIDTH), BF16),
                   jax.ShapeDtypeStruct((batch, nc, MLSTM_WIDTH, CHUNK), BF16),
                   jax.ShapeDtypeStruct((tokens, MLSTM_WIDTH), BF16),
                   jax.ShapeDtypeStruct((tokens, MLSTM_WIDTH), F32),
                   jax.ShapeDtypeStruct((tokens, POOL_WIDTH), BF16),
                   jax.ShapeDtypeStruct((batch, nc, srows, CHUNK), F32)],
        compiler_params=_cparams(("arbitrary",)),
        name="in_proj",
    )(x, mod, norm_w, wt, wkg, wt, wt, wu, gbb, pool_w, pool_scale.reshape(1, POOL_WIDTH))


def _chunk_start(c):
    return c * CHUNK if isinstance(c, int) else pl.multiple_of(c * CHUNK, CHUNK)


def _mlstm_kernel(*refs, nc, hb, has_init, emit_state, cast_weight):
    it = iter(refs)
    q_ref, kt_ref, v_ref, o_ref, g_ref, nw_ref = (next(it) for _ in range(6))
    if has_init:
        c0_ref, n0_ref, m0_ref = (next(it) for _ in range(3))
    if cast_weight:
        wf_ref = next(it)
    out_ref = next(it)
    if emit_state:
        cout_ref, nout_ref, mout_ref = (next(it) for _ in range(3))
    if cast_weight:
        next(it)[...] = wf_ref[...].astype(BF16)
    hf_sc, hb_sc, cn_sc, m_sc, xt_sc, rows_sc = (next(it) for _ in range(6))

    if has_init:
        for d in range(2):
            for j in range(hb):
                cn_sc[d * hb + j, :, 0:HEAD_DIM] = c0_ref[0, d, j]
                n_row = n0_ref[0, 0, d][j:j + 1, :]
                cn_sc[d * hb + j, :, HEAD_DIM:] = jnp.broadcast_to(n_row, (HEAD_DIM, LANES)).T
    else:
        cn_sc[...] = jnp.zeros_like(cn_sc)

    neg_inf = jnp.float32(-jnp.inf)

    cols = []
    for d in range(2):
        def stat(k):
            lo = (d * GATE_STATS + k) * SUBLANES
            return g_ref[0, :, lo:lo + SUBLANES, :].reshape(nc * SUBLANES, LANES)
        a, b, pm, amax, bsum = (stat(k) for k in range(GATE_STATS))
        m = m0_ref[0, 0, d] if has_init else jnp.zeros((SUBLANES, LANES), F32)
        m_prev = [None] * nc
        for ci in (range(nc) if d == 0 else range(nc - 1, -1, -1)):
            rs = slice(ci * SUBLANES, (ci + 1) * SUBLANES)
            m_prev[ci] = m
            m = bsum[rs] + jnp.maximum(m, amax[rs])
        m_sc[d] = m
        mp = jnp.concatenate(m_prev, axis=0)
        mu = jnp.maximum(mp, pm)
        mu_last = jnp.maximum(mp, amax)
        rows_sc[0, d] = a.reshape(nc, SUBLANES, LANES)
        rows_sc[1, d] = jnp.exp(a - mu_last).reshape(nc, SUBLANES, LANES)
        rows_sc[2, d] = jnp.exp(mp - mu_last).reshape(nc, SUBLANES, LANES)
        cols.append((mu, jnp.exp(mp - mu), jnp.exp(-(b + mu))))
    zpad = jnp.zeros((LANES - 6 * SUBLANES, LANES), F32)
    for c in range(nc):
        pieces = []
        for d, ci in ((0, c), (1, nc - 1 - c)):
            pieces += [x[ci * SUBLANES:(ci + 1) * SUBLANES] for x in cols[d]]
        xt_sc[c] = jnp.concatenate(pieces + [zpad], axis=0).T

    row = lax.broadcasted_iota(jnp.int32, (CHUNK, CHUNK), 0)
    col = lax.broadcasted_iota(jnp.int32, (CHUNK, CHUNK), 1)
    causal = (col <= row, col >= row)
    ones_blk = jnp.ones((CHUNK, HEAD_DIM), BF16)

    def step(c, carry):
        chunk_of = (c, nc - 1 - c)
        xt = xt_sc[c]
        chains = []
        for d in range(2):
            ch = chunk_of[d]
            t0 = _chunk_start(ch)
            a_rows = rows_sc[0, d, ch]
            for j in range(hb):
                hs = slice(j * HEAD_DIM, (j + 1) * HEAD_DIM)
                qc = q_ref[0, pl.ds(t0, CHUNK), hs]
                base = 3 * d * SUBLANES + j
                mu_c = xt[:, base:base + 1]
                g_c = xt[:, base + SUBLANES:base + SUBLANES + 1]
                s = jnp.dot(qc, kt_ref[0, ch, hs, :], preferred_element_type=F32)
                p = s * jnp.exp(jnp.where(causal[d], a_rows[j:j + 1, :] - mu_c, neg_inf))
                gq = (qc.astype(F32) * g_c).astype(BF16)
                chains.append((d, j, ch, t0, jnp.concatenate([p.astype(BF16), gq], axis=1)))

        for d, j, ch, t0, lhs in chains:
            hs = slice(j * HEAD_DIM, (j + 1) * HEAD_DIM)
            base = 3 * d * SUBLANES + j
            en_c = xt[:, base + 2 * SUBLANES:base + 2 * SUBLANES + 1]
            v1 = jnp.concatenate([v_ref[0, pl.ds(t0, CHUNK), hs], ones_blk], axis=1)
            cn = cn_sc[d * hb + j]
            tot = jnp.dot(lhs, jnp.concatenate([v1, cn.astype(BF16)], axis=0),
                          preferred_element_type=F32)
            hval = tot[:, :HEAD_DIM] / jnp.maximum(jnp.abs(tot[:, HEAD_DIM:]), en_c)
            dst = hf_sc if d == 0 else hb_sc
            dst[pl.ds(t0, CHUNK), hs] = hval

            e_row = rows_sc[1, d, ch][j:j + 1, :]
            gs_row = rows_sc[2, d, ch][j:j + 1, :]
            wkt = (kt_ref[0, ch, hs, :].astype(F32) * e_row).astype(BF16)
            dcn = jnp.dot(wkt, v1, preferred_element_type=F32)
            gs_blk = jnp.broadcast_to(gs_row, (HEAD_DIM, LANES))
            cn_sc[d * hb + j] = jnp.concatenate([gs_blk, gs_blk], axis=1) * cn + dcn
        return carry

    if nc <= 2:
        for c in range(nc):
            step(c, 0)
    else:
        lax.fori_loop(0, nc, step, 0, unroll=2)

    def finalize(c, carry):
        t0 = _chunk_start(c)
        hsum = hf_sc[pl.ds(t0, CHUNK), :] + hb_sc[pl.ds(t0, CHUNK), :]
        og = o_ref[0, pl.ds(t0, CHUNK), :]
        for j in range(hb):
            hs = slice(j * HEAD_DIM, (j + 1) * HEAD_DIM)
            hn = _rms(hsum[:, hs], nw_ref[0:1, hs])
            out_ref[0, pl.ds(t0, CHUNK), hs] = (hn * og[:, hs]).astype(BF16)
        return carry

    if nc <= 2:
        for c in range(nc):
            finalize(c, 0)
    else:
        lax.fori_loop(0, nc, finalize, 0, unroll=FINALIZE_UNROLL)

    if emit_state:
        pad_rows = jnp.zeros((SUBLANES - hb, LANES), F32)
        for d in range(2):
            n_rows = []
            for j in range(hb):
                cout_ref[0, d, j] = cn_sc[d * hb + j, :, 0:HEAD_DIM]
                n_rows.append(cn_sc[d * hb + j, :, HEAD_DIM:].T[0:1, :])
            nout_ref[0, 0, d] = jnp.concatenate(n_rows + ([pad_rows] if hb < SUBLANES else []), axis=0)
            mout_ref[0, 0, d] = m_sc[d]


def _mlstm_call(q, kt, v, o, stats, nw, init, cast_weight=None, *, batch, seq_len, emit_state):
    hb = HEADS_PER_STEP
    nhb = N_HEADS // hb
    nc = seq_len // CHUNK
    w = hb * HEAD_DIM
    seq_map = lambda b, k: (b, 0, k)
    in_specs = [pl.BlockSpec((1, seq_len, w), seq_map),
                pl.BlockSpec((1, nc, w, CHUNK), lambda b, k: (b, 0, k, 0)),
                pl.BlockSpec((1, seq_len, w), seq_map),
                pl.BlockSpec((1, seq_len, w), seq_map),
                pl.BlockSpec((1, nc, STAT_ROWS, CHUNK), lambda b, k: (b, 0, k, 0)),
                pl.BlockSpec((1, w), lambda b, k: (0, k))]
    args = [q, kt, v, o, stats, nw]
    state_map = lambda b, k: (b, 0, k, 0, 0)
    vec_map = lambda b, k: (b, k, 0, 0, 0)
    vec_spec = pl.BlockSpec((1, 1, 2, SUBLANES, LANES), vec_map)
    if init is not None:
        c0, n0, m0 = init
        in_specs += [pl.BlockSpec((1, 2, hb, HEAD_DIM, HEAD_DIM), state_map), vec_spec, vec_spec]
        args += [c0, n0, m0]
    out_specs = [pl.BlockSpec((1, seq_len, w), seq_map)]
    out_shape = [jax.ShapeDtypeStruct((batch, seq_len, MLSTM_WIDTH), BF16)]
    if emit_state:
        out_specs += [pl.BlockSpec((1, 2, hb, HEAD_DIM, HEAD_DIM), state_map), vec_spec, vec_spec]
        out_shape += [jax.ShapeDtypeStruct((batch, 2, N_HEADS, HEAD_DIM, HEAD_DIM), F32),
                      jax.ShapeDtypeStruct((batch, nhb, 2, SUBLANES, LANES), F32),
                      jax.ShapeDtypeStruct((batch, nhb, 2, SUBLANES, LANES), F32)]
    if cast_weight is not None:
        slab = cast_weight.shape[0] // (batch * nhb)
        assert slab * batch * nhb == cast_weight.shape[0] and slab % (2 * SUBLANES) == 0
        slab_spec = pl.BlockSpec((slab, D_MODEL), lambda b, k: (b * nhb + k, 0))
        in_specs.append(slab_spec)
        args.append(cast_weight)
        out_specs.append(slab_spec)
        out_shape.append(jax.ShapeDtypeStruct(cast_weight.shape, BF16))
    kern = functools.partial(_mlstm_kernel, nc=nc, hb=hb, has_init=init is not None,
                             emit_state=emit_state, cast_weight=cast_weight is not None)
    return pl.pallas_call(
        kern,
        grid=(batch, nhb),
        in_specs=in_specs,
        out_specs=out_specs,
        out_shape=out_shape,
        scratch_shapes=[pltpu.VMEM((seq_len, w), F32),
                        pltpu.VMEM((seq_len, w), F32),
                        pltpu.VMEM((2 * hb, HEAD_DIM, 2 * HEAD_DIM), F32),
                        pltpu.VMEM((2, SUBLANES, LANES), F32),
                        pltpu.VMEM((nc, LANES, LANES), F32),
                        pltpu.VMEM((3, 2, nc, SUBLANES, LANES), F32)],
        compiler_params=_cparams(("arbitrary", "arbitrary")),
        name="mlstm",
    )(*args)


def _shift_rows(x, dlt):
    return pltpu.roll(x, (-dlt) % x.shape[0], axis=0)


def _pool_minus_self(x, half, pos, seg, inv_cnt):
    fwd = x
    k = 1
    while k < half:
        fwd = fwd + jnp.where(pos + k < seg, _shift_rows(fwd, k), 0.0)
        k *= 2
    bwd = jnp.where(pos >= 1, _shift_rows(x, -1), 0.0)
    k = 1
    while k < half:
        bwd = bwd + jnp.where(pos >= k, _shift_rows(bwd, -k), 0.0)
        k *= 2
    return (fwd + bwd) * inv_cnt - x


OUTPROJ_ROW_BLOCKS = 4
OUTPROJ_EPI_ROWS = 64
OUTPROJ_SLICE_COLS = 256


def _outproj_kernel(*refs, cast_mlp_weights):
    hm_ref, hp_ref, x_ref, mod_ref, nw_ref, wout_ref = refs[:6]
    if cast_mlp_weights:
        w1_ref, x1_ref, h2_ref, w1b_ref, mix_ref = refs[6:]
        w1b_ref[...] = w1_ref[...].astype(BF16)
    else:
        x1_ref, h2_ref, mix_ref = refs[6:]
    tm = x_ref.shape[0]
    nblk = OUTPROJ_ROW_BLOCKS
    rb = tm // nblk
    nslice = D_MODEL // OUTPROJ_SLICE_COLS
    mod = mod_ref[0]
    gain1 = nw_ref[1:2, :] * mod[2:3, :]
    gain2 = nw_ref[2:3, :] * (1.0 + mod[4:5, :])
    shift2 = mod[3:4, :]

    def epilogue_slices(r):
        def one(lo):
            rows = slice(lo, lo + OUTPROJ_EPI_ROWS)
            mix = mix_ref[rows, :]
            ms = jnp.mean(mix * mix, axis=-1, keepdims=True)
            x1 = x_ref[rows, :] + (mix * lax.rsqrt(ms + EPS)) * gain1
            x1_ref[rows, :] = x1
            ms2 = jnp.mean(x1 * x1, axis=-1, keepdims=True)
            h2_ref[rows, :] = ((x1 * lax.rsqrt(ms2 + EPS)) * gain2 + shift2).astype(BF16)
        return [functools.partial(one, r * rb + k * OUTPROJ_EPI_ROWS)
                for k in range(rb // OUTPROJ_EPI_ROWS)]

    def run_block(r, pending):
        rows = slice(r * rb, (r + 1) * rb)
        lhs = jnp.concatenate([hm_ref[rows, :], hp_ref[rows, :]], axis=1)
        for idx in range(nslice):
            ns = slice(idx * OUTPROJ_SLICE_COLS, (idx + 1) * OUTPROJ_SLICE_COLS)
            mix_ref[rows, ns] = jnp.dot(lhs, wout_ref[:, ns], preferred_element_type=F32)
            if pending and idx % 4 == 3:
                pending.pop(0)()
        while pending:
            pending.pop(0)()

    pending = []
    for r in range(nblk):
        run_block(r, pending)
        pending = epilogue_slices(r)
    for fn in pending:
        fn()


def _outproj_call(hm, hp, x, mod, mod_map, norm_w, w_out, cast_w1=None, *, tm):
    tokens = x.shape[0]
    steps = tokens // tm
    tok = lambda i: (i, 0)
    in_specs = [pl.BlockSpec((tm, MLSTM_WIDTH), tok),
                pl.BlockSpec((tm, POOL_WIDTH), tok),
                pl.BlockSpec((tm, D_MODEL), tok),
                pl.BlockSpec((1, N_MOD, D_MODEL), mod_map),
                _resident((4, D_MODEL)),
                _resident(w_out.shape)]
    out_specs = [pl.BlockSpec((tm, D_MODEL), tok),
                 pl.BlockSpec((tm, D_MODEL), tok)]
    out_shape = [jax.ShapeDtypeStruct((tokens, D_MODEL), F32),
                 jax.ShapeDtypeStruct((tokens, D_MODEL), BF16)]
    args = [hm, hp, x, mod, norm_w, w_out]
    if cast_w1 is not None:
        slab = D_FF // steps
        assert slab * steps == D_FF and slab % LANES == 0
        slab_spec = pl.BlockSpec((D_MODEL, slab), lambda i: (0, i))
        in_specs.append(slab_spec)
        out_specs.append(slab_spec)
        out_shape.append(jax.ShapeDtypeStruct((D_MODEL, D_FF), BF16))
        args.append(cast_w1)
    return pl.pallas_call(
        functools.partial(_outproj_kernel, cast_mlp_weights=cast_w1 is not None),
        grid=(steps,),
        in_specs=in_specs,
        out_specs=out_specs,
        out_shape=out_shape,
        scratch_shapes=[pltpu.VMEM((tm, D_MODEL), F32)],
        compiler_params=_cparams(("arbitrary",)),
        name="out_proj",
    )(*args)


MLP_LAST_ROW_BLOCKS = (256, 256)
MLP_SLICE_COLS = 256
MLP_VMEM_LIMIT = 62 * 1024 * 1024


def _mlp_kernel(h_ref, w1_hbm, w2_hbm, x1_ref, mod_ref, nw_ref, out_ref, acc_ref, w1_buf, w2_buf, sem,
                *, nk):
    i = pl.program_id(0)
    tm, tf = h_ref.shape[0], w1_buf.shape[2]
    assert nk % 2 == 0

    def copies(kb):
        slot = kb % 2
        return (pltpu.make_async_copy(w1_hbm.at[:, pl.ds(kb * tf, tf)], w1_buf.at[slot], sem.at[0, slot]),
                pltpu.make_async_copy(w2_hbm.at[pl.ds(kb * tf, tf), :], w2_buf.at[slot], sem.at[1, slot]))

    @pl.when(i == 0)
    def _():
        for cp in copies(0):
            cp.start()

    def hidden(slot, rows, cols=slice(None)):
        a = jnp.dot(h_ref[rows, :], w1_buf[slot, :, cols], preferred_element_type=F32)
        return jnp.square(jnp.maximum(a, 0.0)).astype(BF16)

    def last_step(slot):
        assert sum(MLP_LAST_ROW_BLOCKS) == tm
        gain = nw_ref[3:4, :] * mod_ref[0, 5:6, :]

        def epilogue_slices(lo, nrows):
            def one(lo):
                rows = slice(lo, lo + NORM_SLICE_ROWS)
                y = acc_ref[rows, :]
                ms = jnp.mean(y * y, axis=-1, keepdims=True)
                out_ref[rows, :] = x1_ref[rows, :] + (y * lax.rsqrt(ms + EPS)) * gain
            return [functools.partial(one, lo + j * NORM_SLICE_ROWS)
                    for j in range(nrows // NORM_SLICE_ROWS)]

        pending = []
        n1, n2 = tf // MLP_SLICE_COLS, D_MODEL // MLP_SLICE_COLS
        lo = 0
        for nrows in MLP_LAST_ROW_BLOCKS:
            rows = slice(lo, lo + nrows)
            parts = []
            for idx in range(n1):
                parts.append(hidden(slot, rows, slice(idx * MLP_SLICE_COLS, (idx + 1) * MLP_SLICE_COLS)))
                if pending and idx % 2 == 1:
                    pending.pop(0)()
            a = jnp.concatenate(parts, axis=1)
            for idx in range(n2):
                ns = slice(idx * MLP_SLICE_COLS, (idx + 1) * MLP_SLICE_COLS)
                acc_ref[rows, ns] += jnp.dot(a, w2_buf[slot, :, ns], preferred_element_type=F32)
                if pending and idx % 2 == 1:
                    pending.pop(0)()
            while pending:
                pending.pop(0)()
            pending = epilogue_slices(lo, nrows)
            lo += nrows
        for fn in pending:
            fn()

    for kb in range(nk):
        slot = kb % 2
        for cp in copies((kb + 1) % nk):
            cp.start()
        for cp in copies(kb):
            cp.wait()
        if kb == 0:
            acc_ref[...] = jnp.dot(hidden(slot, slice(None)), w2_buf[slot], preferred_element_type=F32)
        elif kb < nk - 1:
            acc_ref[...] += jnp.dot(hidden(slot, slice(None)), w2_buf[slot], preferred_element_type=F32)
        else:
            last_step(slot)

    @pl.when(i == pl.num_programs(0) - 1)
    def _():
        for cp in copies(0):
            cp.wait()


def _mlp_call(h2, x1, mod, mod_map, norm_w, w1, w2, *, tm, tf):
    tokens = x1.shape[0]
    tok = lambda i: (i, 0)
    return pl.pallas_call(
        functools.partial(_mlp_kernel, nk=D_FF // tf),
        grid=(tokens // tm,),
        in_specs=[pl.BlockSpec((tm, D_MODEL), tok),
                  pl.BlockSpec(memory_space=pl.ANY),
                  pl.BlockSpec(memory_space=pl.ANY),
                  pl.BlockSpec((tm, D_MODEL), tok),
                  pl.BlockSpec((1, N_MOD, D_MODEL), lambda i: mod_map(i)),
                  pl.BlockSpec((4, D_MODEL), lambda i: (0, 0))],
        out_specs=pl.BlockSpec((tm, D_MODEL), tok),
        out_shape=jax.ShapeDtypeStruct((tokens, D_MODEL), F32),
        scratch_shapes=[pltpu.VMEM((tm, D_MODEL), F32),
                        pltpu.VMEM((2, D_MODEL, tf), BF16),
                        pltpu.VMEM((2, tf, D_MODEL), BF16),
                        pltpu.SemaphoreType.DMA((2, 2))],
        compiler_params=_cparams(("arbitrary",), MLP_VMEM_LIMIT),
        name="mlp",
    )(h2, w1, w2, x1, mod, norm_w)


def _gate_weights(w_g, gate_bias):
    hb = HEADS_PER_STEP
    nhb = N_HEADS // hb
    wg = w_g.reshape(D_MODEL, N_GATES, nhb, hb)
    wg = jnp.pad(wg, ((0, 0), (0, 0), (0, 0), (0, SUBLANES - hb)))
    wgt = wg.transpose(2, 1, 3, 0).reshape(nhb * GATE_ROWS, D_MODEL)
    gb = jnp.pad(gate_bias.reshape(N_GATES, nhb, hb), ((0, 0), (0, 0), (0, SUBLANES - hb)))
    gb = gb.transpose(1, 0, 2).reshape(nhb * GATE_ROWS)
    return wgt.astype(BF16), gb.astype(F32)


def _pack_head_rows(x):
    hb = HEADS_PER_STEP
    nhb = N_HEADS // hb
    x = x.reshape(x.shape[:2] + (nhb, hb) + x.shape[3:])
    x = jnp.moveaxis(x, 2, 1)
    pad = [(0, 0)] * x.ndim
    pad[3] = (0, SUBLANES - hb)
    return jnp.pad(x, pad)


def _unpack_head_rows(x):
    x = jnp.moveaxis(x[:, :, :, :HEADS_PER_STEP], 1, 2)
    return x.reshape(x.shape[:2] + (N_HEADS,) + x.shape[4:])


def _mod_map(mod_row, seq_len, tm):
    if mod_row is None:
        per = seq_len // tm
        return lambda i: (1 + i // per, 0, 0)
    return lambda i: (mod_row, 0, 0)


def _mixer_stage(x, mod, mod_row, init, params, w_out, cast_rows, cast_w1=None, *, seg, tm, emit_state):
    batch, seq_len, _ = x.shape
    wts, mlstm_nw, pool_w, pool_scale, norm_w = params
    tokens = batch * seq_len
    xf = x.reshape(tokens, D_MODEL)
    q, kt, v, og, hp, stats = _inproj_call(xf, mod, mod_row, norm_w, wts, pool_w, pool_scale,
                                           batch=batch, seq_len=seq_len, tm=tm, seg=seg)
    shp = (batch, seq_len, MLSTM_WIDTH)
    res = _mlstm_call(q.reshape(shp), kt, v.reshape(shp), og.reshape(shp), stats, mlstm_nw, init,
                      cast_rows, batch=batch, seq_len=seq_len, emit_state=emit_state)
    hm = res[0].reshape(tokens, MLSTM_WIDTH)
    outs = _outproj_call(hm, hp, xf, mod, _mod_map(mod_row, seq_len, tm), norm_w,
                         res[-1] if w_out is None else w_out, cast_w1, tm=tm)
    return outs, res[1:]


def kernel(x_prompt, x_sample, c, state_C, state_n, state_m, c_ctx, w_in, gate_bias, mlstm_norm_w,
           pool_w, pool_scale, w_out, ada_w, ada_b, norm_w, w1, w2):
    m4 = 4 * MLSTM_WIDTH
    yp, ys = x_prompt, x_sample
    new_c, new_n, new_m = [], [], []
    for layer in range(w_in.shape[0]):
        n_lat = c.shape[0]
        rows = 16
        cond = jnp.concatenate([c_ctx[None, :], c, jnp.zeros((rows - 1 - n_lat, D_MODEL), F32)], axis=0)
        mod, wt = _mod_call(cond, ada_w[layer], ada_b[layer], w_in[layer].T)
        mod = mod.reshape(rows, N_MOD, D_MODEL)
        wgt, gb = _gate_weights(w_in[layer][:, m4:m4 + N_GATES * N_HEADS], gate_bias[layer])
        wkg = jnp.concatenate([wt[MLSTM_WIDTH:2 * MLSTM_WIDTH], wgt], axis=0)
        wts = (wt, wkg, wt[m4 + N_GATES * N_HEADS:], gb)
        params = (wts, mlstm_norm_w[layer].reshape(1, MLSTM_WIDTH), pool_w[layer].astype(BF16),
                  pool_scale[layer], norm_w[layer])

        tm = TOKEN_TILE
        (x1p, h2p), (cc, nn, mm, w_out_b) = _mixer_stage(yp, mod, 0, None, params, None, w_out[layer],
                                                         seg=yp.shape[1], tm=tm, emit_state=True)
        new_c.append(cc)
        new_n.append(_unpack_head_rows(nn))
        new_m.append(_unpack_head_rows(mm)[..., 0])

        sc = state_C[:, layer].astype(F32)
        sn = _pack_head_rows(state_n[:, layer].astype(F32))
        sm = _pack_head_rows(jnp.broadcast_to(state_m[:, layer].astype(F32)[..., None],
                                              (n_lat, 2, N_HEADS, LANES)))
        (x1s, h2s, w1b), (w2b,) = _mixer_stage(ys, mod, None, (sc, sn, sm), params, w_out_b, w2[layer],
                                               w1[layer], seg=GRID_W, tm=tm, emit_state=False)
        yp = _mlp_call(h2p, x1p, mod, _mod_map(0, yp.shape[1], tm), norm_w[layer], w1b, w2b,
                       tm=tm, tf=MLP_TF).reshape(yp.shape)
        ys = _mlp_call(h2s, x1s, mod, _mod_map(None, ys.shape[1], tm), norm_w[layer], w1b, w2b,
                       tm=tm, tf=MLP_TF).reshape(ys.shape)
    return (yp, ys, jnp.stack(new_c, axis=1), jnp.stack(new_n, axis=1), jnp.stack(new_m, axis=1))
```
